```python
import jax, jax.numpy as jnp
from jax import lax
import numpy as np

D_MODEL = 2048
BATCH = 16
SEQ = 256
DEPTH = 4
DEC_BATCH = 4
DEC_SEQ = 2048
PAST_LEN = 512

N_HEADS = 16
HEAD_DIM = D_MODEL // N_HEADS
GRID_W = 64
WIN_ROWS = 8
WIN_COLS = 16
CONV_WIDTH = 3
D_FF = 5632
N_EXPERTS = 8
TOP_K = 2
D_FF_EXPERT = 5632
N_CONV_LAYERS = (DEPTH + 1) // 2
N_ATTN_LAYERS = DEPTH // 2
N_DENSE_LAYERS = (DEPTH + 1) // 2
N_MOE_LAYERS = DEPTH // 2
Q_BLOCK = 128
EPS = 1e-6
NEG_INF = -1e30
ATTN_SCALE = HEAD_DIM ** -0.5
ADA_SCALE = 0.2

kernel_name = 'hybrid_conv_natten_moe_dit_step'


def rms_norm(x, g):
    xf = x.astype(jnp.float32)
    y = xf * lax.rsqrt(jnp.mean(xf * xf, axis=-1, keepdims=True) + EPS)
    return (y * g.astype(jnp.float32)).astype(x.dtype)


def ada_params(cond, w_ada_i, b_ada_i):
    m = jax.nn.silu(cond) @ w_ada_i + b_ada_i
    return [t[:, None, :] for t in jnp.split(m, 6, axis=-1)]


def conv_mixer(h, w_in, conv_w, w_out):
    b_g, c_g, v = jnp.split(h @ w_in, 3, axis=-1)
    u = c_g * v
    L = u.shape[1]
    up = jnp.pad(u, ((0, 0), (1, 1), (0, 0)))
    conv = up[:, :L] * conv_w[0] + up[:, 1:L + 1] * conv_w[1] + up[:, 2:L + 2] * conv_w[2]
    return (b_g * conv) @ w_out


def qkv_heads(h, w_qkv, q_gain, k_gain):
    B, L, _ = h.shape
    q, k, v = jnp.split(h @ w_qkv, 3, axis=-1)
    q = rms_norm(q.reshape(B, L, N_HEADS, HEAD_DIM), q_gain)
    k = rms_norm(k.reshape(B, L, N_HEADS, HEAD_DIM), k_gain)
    v = v.reshape(B, L, N_HEADS, HEAD_DIM)
    return q, k, v


def context_attention(q, k, v):
    B, L = q.shape[:2]
    nb = L // Q_BLOCK
    qb = jnp.moveaxis(q.reshape(B, nb, Q_BLOCK, N_HEADS, HEAD_DIM), 1, 0)

    def block(qi):
        s = jnp.einsum('bqhd,bkhd->bhqk', qi, k).astype(jnp.float32) * ATTN_SCALE
        p = jax.nn.softmax(s, axis=-1).astype(v.dtype)
        return jnp.einsum('bhqk,bkhd->bqhd', p, v)

    o = lax.map(block, qb)
    return jnp.moveaxis(o, 0, 1).reshape(B, L, N_HEADS * HEAD_DIM)


def latent_neighbourhood_attention(q, k, v, k_ctx, v_ctx, rpb):
    B, T = q.shape[:2]
    rows = T // GRID_W
    wr = min(WIN_ROWS, rows)
    n_band = wr * GRID_W
    kg = k.reshape(B, rows, GRID_W, N_HEADS, HEAD_DIM)
    vg = v.reshape(B, rows, GRID_W, N_HEADS, HEAD_DIM)
    qg = q.reshape(B, rows, GRID_W, N_HEADS, HEAD_DIM)
    cols = jnp.arange(GRID_W)
    col_start = jnp.clip(cols - WIN_COLS // 2, 0, GRID_W - WIN_COLS)
    col_mask = (cols[None, :] >= col_start[:, None]) & (cols[None, :] < col_start[:, None] + WIN_COLS)
    mask = jnp.broadcast_to(col_mask[:, None, :], (GRID_W, wr, GRID_W)).reshape(GRID_W, n_band)
    dc_idx = jnp.clip(cols[None, :] - cols[:, None], -(WIN_COLS - 1), WIN_COLS - 1) + WIN_COLS - 1
    rpb_cols = rpb.astype(jnp.float32)[:, :, dc_idx]

    def row_block(args):
        r, q_r = args
        start = jnp.clip(r - wr // 2, 0, rows - wr)
        k_band = lax.dynamic_slice_in_dim(kg, start, wr, axis=1).reshape(B, n_band, N_HEADS, HEAD_DIM)
        v_band = lax.dynamic_slice_in_dim(vg, start, wr, axis=1).reshape(B, n_band, N_HEADS, HEAD_DIM)
        dr_idx = start + jnp.arange(wr) - r + WIN_ROWS - 1
        bias = jnp.transpose(rpb_cols[:, dr_idx], (0, 2, 1, 3)).reshape(N_HEADS, GRID_W, n_band)
        s_loc = jnp.einsum('bqhd,bkhd->bhqk', q_r, k_band).astype(jnp.float32) * ATTN_SCALE + bias
        s_loc = jnp.where(mask, s_loc, NEG_INF)
        s_ctx = jnp.einsum('bqhd,bkhd->bhqk', q_r, k_ctx).astype(jnp.float32) * ATTN_SCALE
        p = jax.nn.softmax(jnp.concatenate([s_loc, s_ctx], axis=-1), axis=-1).astype(v.dtype)
        return (jnp.einsum('bhqk,bkhd->bqhd', p[..., :n_band], v_band)
                + jnp.einsum('bhqk,bkhd->bqhd', p[..., n_band:], v_ctx))

    o = lax.map(row_block, (jnp.arange(rows), jnp.moveaxis(qg, 1, 0)))
    return jnp.moveaxis(o, 0, 1).reshape(B, T, N_HEADS * HEAD_DIM)


def swiglu(h, w_gate, w_up, w_down):
    return (jax.nn.silu(h @ w_gate) * (h @ w_up)) @ w_down


def moe_ffn(h, w_router, w_gate, w_up, w_down):
    logits = (h @ w_router).astype(jnp.float32)
    top_v, top_i = lax.top_k(logits, TOP_K)
    wts = jax.nn.softmax(top_v, axis=-1)
    gates = jnp.sum(jax.nn.one_hot(top_i, N_EXPERTS, dtype=jnp.float32) * wts[..., None], axis=-2).astype(h.dtype)
    out = jnp.zeros_like(h)
    for e in range(N_EXPERTS):
        out = out + gates[..., e:e + 1] * swiglu(h, w_gate[e], w_up[e], w_down[e])
    return out


def channel_mixer(i, h, w_ffn_gate, w_ffn_up, w_ffn_down, w_router, w_moe_gate, w_moe_up, w_moe_down):
    j = i // 2
    if i % 2 == 0:
        return swiglu(h, w_ffn_gate[j], w_ffn_up[j], w_ffn_down[j])
    return moe_ffn(h, w_router[j], w_moe_gate[j], w_moe_up[j], w_moe_down[j])


def setup_inputs(seed: int = 0) -> dict:
    key = jax.random.key(seed)
    ks = jax.random.split(key, 26)

    def nrm(k, shape, scale):
        return jax.random.normal(k, shape, jnp.float32) * scale

    D = D_MODEL
    return {
        'x_prompt': nrm(ks[0], (BATCH, SEQ, D), 1.0),
        'x_sample': nrm(ks[1], (DEC_BATCH, DEC_SEQ, D), 1.0),
        'cache_k': nrm(ks[2], (DEC_BATCH, N_ATTN_LAYERS, PAST_LEN, N_HEADS, HEAD_DIM), 1.0),
        'cache_v': nrm(ks[3], (DEC_BATCH, N_ATTN_LAYERS, PAST_LEN, N_HEADS, HEAD_DIM), 1.0),
        'c': nrm(ks[4], (DEC_BATCH, D), 1.0),
        'c_ctx': nrm(ks[5], (D,), 1.0),
        'norm1': 1.0 + nrm(ks[6], (DEPTH, D), 0.01),
        'norm2': 1.0 + nrm(ks[7], (DEPTH, D), 0.01),
        'w_ada': nrm(ks[8], (DEPTH, D, 6 * D), ADA_SCALE * D ** -0.5),
        'b_ada': nrm(ks[9], (DEPTH, 6 * D), 0.01),
        'w_conv_in': nrm(ks[10], (N_CONV_LAYERS, D, 3 * D), D ** -0.5),
        'conv_w': nrm(ks[11], (N_CONV_LAYERS, CONV_WIDTH, D), CONV_WIDTH ** -0.5),
        'w_conv_out': nrm(ks[12], (N_CONV_LAYERS, D, D), D ** -0.5),
        'w_qkv': nrm(ks[13], (N_ATTN_LAYERS, D, 3 * D), D ** -0.5),
        'q_norm': 1.0 + nrm(ks[14], (N_ATTN_LAYERS, HEAD_DIM), 0.01),
        'k_norm': 1.0 + nrm(ks[15], (N_ATTN_LAYERS, HEAD_DIM), 0.01),
        'rpb': nrm(ks[16], (N_ATTN_LAYERS, N_HEADS, 2 * WIN_ROWS - 1, 2 * WIN_COLS - 1), 0.1),
        'w_attn_out': nrm(ks[17], (N_ATTN_LAYERS, D, D), D ** -0.5),
        'w_ffn_gate': nrm(ks[18], (N_DENSE_LAYERS, D, D_FF), D ** -0.5),
        'w_ffn_up': nrm(ks[19], (N_DENSE_LAYERS, D, D_FF), D ** -0.5),
        'w_ffn_down': nrm(ks[20], (N_DENSE_LAYERS, D_FF, D), D_FF ** -0.5),
        'w_router': nrm(ks[21], (N_MOE_LAYERS, D, N_EXPERTS), D ** -0.5),
        'w_moe_gate': nrm(ks[22], (N_MOE_LAYERS, N_EXPERTS, D, D_FF_EXPERT), D ** -0.5),
        'w_moe_up': nrm(ks[23], (N_MOE_LAYERS, N_EXPERTS, D, D_FF_EXPERT), D ** -0.5),
        'w_moe_down': nrm(ks[24], (N_MOE_LAYERS, N_EXPERTS, D_FF_EXPERT, D), D_FF_EXPERT ** -0.5),
    }


def reference(x_prompt, x_sample, cache_k, cache_v, c, c_ctx, norm1, norm2, w_ada, b_ada,
              w_conv_in, conv_w, w_conv_out, w_qkv, q_norm, k_norm, rpb, w_attn_out,
              w_ffn_gate, w_ffn_up, w_ffn_down, w_router, w_moe_gate, w_moe_up, w_moe_down):
    x = x_prompt
    ks_new = []
    vs_new = []
    for i in range(DEPTH):
        j = i // 2
        sh1, sc1, g1, sh2, sc2, g2 = ada_params(c_ctx[None, :], w_ada[i], b_ada[i])
        h = rms_norm(x, norm1[i]) * (1.0 + sc1) + sh1
        if i % 2 == 0:
            mix = conv_mixer(h, w_conv_in[j], conv_w[j], w_conv_out[j])
        else:
            q, k, v = qkv_heads(h, w_qkv[j], q_norm[j], k_norm[j])
            ks_new.append(k)
            vs_new.append(v)
            mix = context_attention(q, k, v) @ w_attn_out[j]
        x = x + g1 * mix
        h = rms_norm(x, norm2[i]) * (1.0 + sc2) + sh2
        x = x + g2 * channel_mixer(i, h, w_ffn_gate, w_ffn_up, w_ffn_down,
                                   w_router, w_moe_gate, w_moe_up, w_moe_down)
    y_prompt = x
    new_k = jnp.stack(ks_new, axis=1)
    new_v = jnp.stack(vs_new, axis=1)

    x = x_sample
    for i in range(DEPTH):
        j = i // 2
        sh1, sc1, g1, sh2, sc2, g2 = ada_params(c, w_ada[i], b_ada[i])
        h = rms_norm(x, norm1[i]) * (1.0 + sc1) + sh1
        if i % 2 == 0:
            mix = conv_mixer(h, w_conv_in[j], conv_w[j], w_conv_out[j])
        else:
            q, k, v = qkv_heads(h, w_qkv[j], q_norm[j], k_norm[j])
            mix = latent_neighbourhood_attention(q, k, v, cache_k[:, j], cache_v[:, j], rpb[j]) @ w_attn_out[j]
        x = x + g1 * mix
        h = rms_norm(x, norm2[i]) * (1.0 + sc2) + sh2
        x = x + g2 * channel_mixer(i, h, w_ffn_gate, w_ffn_up, w_ffn_down,
                                   w_router, w_moe_gate, w_moe_up, w_moe_down)
    y_sample = x
    return (y_prompt, y_sample, new_k, new_v)
```

```python
import functools

import numpy as np
import jax
import jax.numpy as jnp
from jax import lax
from jax.experimental import pallas as pl
from jax.experimental.pallas import tpu as pltpu

D = 2048
N_CTX_SEQ = 16
CTX_LEN = 256
N_DEC_SEQ = 4
DEC_LEN = 2048
CTX_ROWS = N_CTX_SEQ * CTX_LEN
DEC_ROWS = N_DEC_SEQ * DEC_LEN
T = CTX_ROWS + DEC_ROWS
DEPTH = 4
PAST_LEN = 512
N_HEADS = 16
HEAD_DIM = 128
GRID_W = 64
GRID_ROWS = DEC_LEN // GRID_W
WIN_ROWS = 8
WIN_COLS = 16
D_FF = 5632
N_EXPERTS = 8
TOP_K = 2
EPS = 1e-6
NEG_INF = -1e30
ATTN_SCALE = HEAD_DIM ** -0.5

N_GROUPS_PADDED = 8
LANES = 128
SUBLANES = 8
VMEM_CAP_BYTES = 56 * 2 ** 20
VMEM_HEADROOM = 4 * 2 ** 20

Q_ROWS_PER_BLOCK = 4
Q_BLOCK = Q_ROWS_PER_BLOCK * GRID_W
N_Q_BLOCKS = GRID_ROWS // Q_ROWS_PER_BLOCK
BAND_ROWS = Q_ROWS_PER_BLOCK + WIN_ROWS
BAND = BAND_ROWS * GRID_W

MOE_TM = 512
N_ASSIGN = T * TOP_K
MOE_ROWS = N_ASSIGN + N_EXPERTS * MOE_TM
MOE_TILES = MOE_ROWS // MOE_TM
ROW_SLABS = D // LANES

bf16 = jnp.bfloat16
f32 = jnp.float32


def _mm(a, b):
    return jnp.dot(a, b, preferred_element_type=f32)


def _mm_nt(a, b):
    return lax.dot_general(a, b, (((1,), (1,)), ((), ())), preferred_element_type=f32)


def _nbytes(shape, dtype):
    return int(np.prod(shape)) * jnp.dtype(dtype).itemsize


def _params(semantics, pipelined_bytes, resident_bytes):
    need = 2 * pipelined_bytes + resident_bytes + VMEM_HEADROOM
    return pltpu.CompilerParams(dimension_semantics=semantics,
                                vmem_limit_bytes=int(min(max(need, 16 * 2 ** 20), VMEM_CAP_BYTES)))


def _group_of_tile(i, tm):
    n_ctx = CTX_ROWS // tm
    per_seq = DEC_LEN // tm
    return jnp.where(i < n_ctx, 0, 1 + (i - n_ctx) // per_seq)


def _mod_spec(layer, which, tm, width=D):
    def index(*grid_and_prefetch):
        i = grid_and_prefetch[0]
        j = grid_and_prefetch[1] if width != D else 0
        return ((layer * N_GROUPS_PADDED + _group_of_tile(i, tm)) * 6 + which, 0, j)
    return pl.BlockSpec((None, 1, width), index)


def _norm_mod_rows(x, gain, scale1p, shift):
    ms = jnp.mean(x * x, axis=-1, keepdims=True)
    y = x * lax.rsqrt(ms + EPS)
    return (y * gain) * scale1p + shift


def _norm_mod_to_scratch(x_ref, gain_ref, sc_ref, sh_ref, h_scr, rows=256):
    gain = gain_ref[...]
    scale1p = 1.0 + sc_ref[...]
    shift = sh_ref[...]

    def body(c, carry):
        r = pl.multiple_of(c * rows, rows)
        h = _norm_mod_rows(x_ref[pl.ds(r, rows), :], gain, scale1p, shift)
        h_scr[pl.ds(r, rows), :] = h.astype(bf16)
        return carry

    lax.fori_loop(0, x_ref.shape[0] // rows, body, 0)


def _ada_kernel(cond_ref, w_ref, b_ref, o_ref):
    c = cond_ref[...]
    s = (c * jax.nn.sigmoid(c)).astype(bf16)
    o_ref[...] = _mm(s, w_ref[...].astype(bf16)) + b_ref[...]


def _ada_params(cond, w_ada, b_ada):
    tn = 1024
    out = pl.pallas_call(
        _ada_kernel,
        grid=(DEPTH, 6 * D // tn),
        in_specs=[
            pl.BlockSpec((N_GROUPS_PADDED, D), lambda l, j: (0, 0)),
            pl.BlockSpec((None, D, tn), lambda l, j: (l, 0, j)),
            pl.BlockSpec((None, 1, tn), lambda l, j: (l, 0, j)),
        ],
        out_specs=pl.BlockSpec((None, N_GROUPS_PADDED, tn), lambda l, j: (l, 0, j)),
        out_shape=jax.ShapeDtypeStruct((DEPTH, N_GROUPS_PADDED, 6 * D), f32),
        compiler_params=_params(("arbitrary", "arbitrary"),
                                _nbytes((D, tn), f32) + _nbytes((8, tn), f32) * 2,
                                _nbytes((D, tn), bf16) + _nbytes((8, D), f32) * 2),
        name="ada_params",
    )(cond, w_ada, b_ada.reshape(DEPTH, 1, 6 * D))
    return out.reshape(DEPTH * N_GROUPS_PADDED * 6, 1, D)


def _conv_in_kernel(x_ref, gain_ref, sc_ref, sh_ref, wb_ref, wc_ref, wv_ref, b_ref, u_ref, h_scr):
    @pl.when(pl.program_id(1) == 0)
    def _():
        _norm_mod_to_scratch(x_ref, gain_ref, sc_ref, sh_ref, h_scr)

    h = h_scr[...]
    b_ref[...] = _mm(h, wb_ref[...].astype(bf16))
    u_ref[...] = _mm(h, wc_ref[...].astype(bf16)) * _mm(h, wv_ref[...].astype(bf16))


def _conv_in(x, mod, norm, w_conv_in, layer):
    tm, tn = 1024, 256
    jl = layer // 2
    nb = D // tn
    return pl.pallas_call(
        _conv_in_kernel,
        grid=(T // tm, nb),
        in_specs=[
            pl.BlockSpec((tm, D), lambda i, j: (i, 0)),
            pl.BlockSpec((None, 1, D), lambda i, j: (layer, 0, 0)),
            _mod_spec(layer, 1, tm), _mod_spec(layer, 0, tm),
            pl.BlockSpec((None, D, tn), lambda i, j: (jl, 0, j)),
            pl.BlockSpec((None, D, tn), lambda i, j: (jl, 0, nb + j)),
            pl.BlockSpec((None, D, tn), lambda i, j: (jl, 0, 2 * nb + j)),
        ],
        out_specs=[pl.BlockSpec((tm, tn), lambda i, j: (i, j))] * 2,
        out_shape=[jax.ShapeDtypeStruct((T, D), f32)] * 2,
        scratch_shapes=[pltpu.VMEM((tm, D), bf16)],
        compiler_params=_params(("arbitrary", "arbitrary"),
                                _nbytes((tm, D), f32) + 3 * _nbytes((D, tn), f32) + 2 * _nbytes((tm, tn), f32),
                                _nbytes((tm, D), bf16) + 3 * _nbytes((D, tn), bf16) + 3 * _nbytes((tm, tn), f32)),
        name="conv_in",
    )(x, norm.reshape(DEPTH, 1, D), mod, mod, w_conv_in, w_conv_in, w_conv_in)


def _ffn_up_kernel(x_ref, gain_ref, sc_ref, sh_ref, wg_ref, wu_ref, a_ref, h_scr):
    @pl.when(pl.program_id(1) == 0)
    def _():
        _norm_mod_to_scratch(x_ref, gain_ref, sc_ref, sh_ref, h_scr)

    h = h_scr[...]
    g = _mm(h, wg_ref[...].astype(bf16))
    u = _mm(h, wu_ref[...].astype(bf16))
    a_ref[...] = ((g * jax.nn.sigmoid(g)) * u).astype(bf16)


def _ffn_up(x, mod, norm, w_gate, w_up, layer):
    tm, tn = 1024, 512
    jl = layer // 2
    return pl.pallas_call(
        _ffn_up_kernel,
        grid=(T // tm, D_FF // tn),
        in_specs=[
            pl.BlockSpec((tm, D), lambda i, j: (i, 0)),
            pl.BlockSpec((None, 1, D), lambda i, j: (layer, 0, 0)),
            _mod_spec(layer, 4, tm), _mod_spec(layer, 3, tm),
            pl.BlockSpec((None, D, tn), lambda i, j: (jl, 0, j)),
            pl.BlockSpec((None, D, tn), lambda i, j: (jl, 0, j)),
        ],
        out_specs=pl.BlockSpec((tm, tn), lambda i, j: (i, j)),
        out_shape=jax.ShapeDtypeStruct((T, D_FF), bf16),
        scratch_shapes=[pltpu.VMEM((tm, D), bf16)],
        compiler_params=_params(("arbitrary", "arbitrary"),
                                _nbytes((tm, D), f32) + 2 * _nbytes((D, tn), f32) + _nbytes((tm, tn), bf16),
                                _nbytes((tm, D), bf16) + 2 * _nbytes((D, tn), bf16) + 3 * _nbytes((tm, tn), f32)),
        name="ffn_up",
    )(x, norm.reshape(DEPTH, 1, D), mod, mod, w_gate, w_up)


def _qkv_kernel(x_ref, gain_ref, sc_ref, sh_ref, w_ref, qg_ref, kg_ref, o_ref, h_scr, *, n_q_blocks):
    j = pl.program_id(1)

    @pl.when(j == 0)
    def _():
        _norm_mod_to_scratch(x_ref, gain_ref, sc_ref, sh_ref, h_scr)

    acc = _mm(h_scr[...], w_ref[...].astype(bf16))

    @pl.when(j >= 2 * n_q_blocks)
    def _():
        o_ref[...] = acc

    @pl.when(j < 2 * n_q_blocks)
    def _():
        head_gain = jnp.where(j < n_q_blocks, qg_ref[...], kg_ref[...])
        for hh in range(acc.shape[1] // HEAD_DIM):
            a = acc[:, hh * HEAD_DIM:(hh + 1) * HEAD_DIM]
            ms = jnp.mean(a * a, axis=-1, keepdims=True)
            o_ref[:, hh * HEAD_DIM:(hh + 1) * HEAD_DIM] = (a * lax.rsqrt(ms + EPS)) * head_gain


def _qkv(x, mod, norm, w_qkv, q_norm, k_norm, layer):
    tm, tn = 1024, 512
    jl = layer // 2
    return pl.pallas_call(
        functools.partial(_qkv_kernel, n_q_blocks=D // tn),
        grid=(T // tm, 3 * D // tn),
        in_specs=[
            pl.BlockSpec((tm, D), lambda i, j: (i, 0)),
            pl.BlockSpec((None, 1, D), lambda i, j: (layer, 0, 0)),
            _mod_spec(layer, 1, tm), _mod_spec(layer, 0, tm),
            pl.BlockSpec((None, D, tn), lambda i, j: (jl, 0, j)),
            pl.BlockSpec((None, 1, HEAD_DIM), lambda i, j: (jl, 0, 0)),
            pl.BlockSpec((None, 1, HEAD_DIM), lambda i, j: (jl, 0, 0)),
        ],
        out_specs=pl.BlockSpec((tm, tn), lambda i, j: (i, j)),
        out_shape=jax.ShapeDtypeStruct((T, 3 * D), f32),
        scratch_shapes=[pltpu.VMEM((tm, D), bf16)],
        compiler_params=_params(("arbitrary", "arbitrary"),
                                _nbytes((tm, D), f32) + _nbytes((D, tn), f32) + _nbytes((tm, tn), f32),
                                _nbytes((tm, D), bf16) + _nbytes((D, tn), bf16) + 2 * _nbytes((tm, tn), f32)),
        name="qkv",
    )(x, norm.reshape(DEPTH, 1, D), mod, mod, w_qkv,
      q_norm.reshape(-1, 1, HEAD_DIM), k_norm.reshape(-1, 1, HEAD_DIM))


def _mm_res_kernel(a_ref, w_ref, x_ref, gate_ref, o_ref):
    o_ref[...] = x_ref[...] + gate_ref[...] * _mm(a_ref[...], w_ref[...].astype(bf16))


def _mm_residual(a, w, x, mod, layer, which, tm, tn, name):
    k = a.shape[1]
    jl = layer // 2
    return pl.pallas_call(
        _mm_res_kernel,
        grid=(T // tm, D // tn),
        in_specs=[
            pl.BlockSpec((tm, k), lambda i, j: (i, 0)),
            pl.BlockSpec((None, k, tn), lambda i, j: (jl, 0, j)),
            pl.BlockSpec((tm, tn), lambda i, j: (i, j)),
            _mod_spec(layer, which, tm, tn),
        ],
        out_specs=pl.BlockSpec((tm, tn), lambda i, j: (i, j)),
        out_shape=jax.ShapeDtypeStruct((T, D), f32),
        compiler_params=_params(("arbitrary", "arbitrary"),
                                _nbytes((tm, k), bf16) + _nbytes((k, tn), f32) + 2 * _nbytes((tm, tn), f32),
                                _nbytes((k, tn), bf16) + 2 * _nbytes((tm, tn), f32)),
        name=name,
    )(a, w, x, mod)


def _conv_gate_kernel(b_ref, u_ref, up_ref, un_ref, cw_ref, o_ref, *, tm, cols):
    row0 = pl.program_id(0) * tm
    seq_len = jnp.where(row0 < CTX_ROWS, CTX_LEN, DEC_LEN)
    local = lax.broadcasted_iota(jnp.int32, (tm, cols), 0)
    pos = (row0 + local) & (seq_len - 1)
    is_first = pos == 0
    is_last = pos == seq_len - 1
    for c in range(D // cols):
        sl = slice(c * cols, (c + 1) * cols)
        u = u_ref[:, sl]
        prev_row = up_ref[SUBLANES - 1:SUBLANES, sl]
        next_row = un_ref[0:1, sl]
        u_prev = jnp.where(local == 0, prev_row, pltpu.roll(u, 1, axis=0))
        u_next = jnp.where(local == tm - 1, next_row, pltpu.roll(u, tm - 1, axis=0))
        u_prev = jnp.where(is_first, 0.0, u_prev)
        u_next = jnp.where(is_last, 0.0, u_next)
        conv = u_prev * cw_ref[0:1, sl] + u * cw_ref[1:2, sl] + u_next * cw_ref[2:3, sl]
        o_ref[:, sl] = (b_ref[:, sl] * conv).astype(bf16)


def _conv_gate(b, u, conv_w, layer):
    tm, cols = 512, 512
    jl = layer // 2
    blocks_per_tile = tm // SUBLANES
    return pl.pallas_call(
        functools.partial(_conv_gate_kernel, tm=tm, cols=cols),
        grid=(T // tm,),
        in_specs=[
            pl.BlockSpec((tm, D), lambda i: (i, 0)),
            pl.BlockSpec((tm, D), lambda i: (i, 0)),
            pl.BlockSpec((SUBLANES, D), lambda i: (jnp.maximum(i * blocks_per_tile - 1, 0), 0)),
            pl.BlockSpec((SUBLANES, D), lambda i: (jnp.minimum((i + 1) * blocks_per_tile, T // SUBLANES - 1), 0)),
            pl.BlockSpec((None, 3, D), lambda i: (jl, 0, 0)),
        ],
        out_specs=pl.BlockSpec((tm, D), lambda i: (i, 0)),
        out_shape=jax.ShapeDtypeStruct((T, D), bf16),
        compiler_params=_params(("arbitrary",),
                                2 * _nbytes((tm, D), f32) + _nbytes((tm, D), bf16) + 2 * _nbytes((SUBLANES, D), f32),
                                8 * _nbytes((tm, cols), f32)),
        name="conv_gate",
    )(b, u, u, u, conv_w)


def _softmax_pv(parts):
    m = parts[0][0].max(axis=-1, keepdims=True)
    for s, _ in parts[1:]:
        m = jnp.maximum(m, s.max(axis=-1, keepdims=True))
    den = None
    out = None
    for s, v in parts:
        p = jnp.exp(s - m)
        d = p.sum(axis=-1, keepdims=True)
        o = _mm(p.astype(bf16), v)
        den = d if den is None else den + d
        out = o if out is None else out + o
    return out * (1.0 / den)


def _ctx_attn_kernel(q_ref, k_ref, v_ref, o_ref):
    for hh in range(q_ref.shape[1] // HEAD_DIM):
        sl = slice(hh * HEAD_DIM, (hh + 1) * HEAD_DIM)
        q = q_ref[:, sl].astype(bf16)
        k = k_ref[:, sl].astype(bf16)
        v = v_ref[:, sl].astype(bf16)
        s = _mm_nt(q, k) * ATTN_SCALE
        o_ref[:, sl] = _softmax_pv([(s, v)]).astype(bf16)


def _ctx_attention(qkv):
    tn = 512
    nb = D // tn
    return pl.pallas_call(
        _ctx_attn_kernel,
        grid=(N_CTX_SEQ, nb),
        in_specs=[
            pl.BlockSpec((CTX_LEN, tn), lambda b, g: (b, g)),
            pl.BlockSpec((CTX_LEN, tn), lambda b, g: (b, nb + g)),
            pl.BlockSpec((CTX_LEN, tn), lambda b, g: (b, 2 * nb + g)),
        ],
        out_specs=pl.BlockSpec((CTX_LEN, tn), lambda b, g: (b, g)),
        out_shape=jax.ShapeDtypeStruct((CTX_ROWS, D), bf16),
        compiler_params=_params(("arbitrary", "arbitrary"),
                                3 * _nbytes((CTX_LEN, tn), f32) + _nbytes((CTX_LEN, tn), bf16),
                                8 * _nbytes((CTX_LEN, CTX_LEN), f32)),
        name="ctx_attention",
    )(qkv, qkv, qkv)


def _band_start_row(m):
    return jnp.clip(Q_ROWS_PER_BLOCK * m - WIN_ROWS // 2, 0, GRID_ROWS - BAND_ROWS)


def _loc_attn_kernel(q_ref, k_ref, v_ref, ck_ref, cv_ref, bias_ref, o_ref):
    start = pl.multiple_of(_band_start_row(pl.program_id(2)) * GRID_W, GRID_W)
    for hh in range(q_ref.shape[1] // HEAD_DIM):
        sl = slice(hh * HEAD_DIM, (hh + 1) * HEAD_DIM)
        q = q_ref[:, sl].astype(bf16)
        k_band = k_ref[pl.ds(start, BAND), sl].astype(bf16)
        v_band = v_ref[pl.ds(start, BAND), sl].astype(bf16)
        k_ctx = ck_ref[:, sl].astype(bf16)
        v_ctx = cv_ref[:, sl].astype(bf16)
        s_loc = _mm_nt(q, k_band) * ATTN_SCALE + bias_ref[hh]
        s_ctx = _mm_nt(q, k_ctx) * ATTN_SCALE
        o_ref[:, sl] = _softmax_pv([(s_loc, v_band), (s_ctx, v_ctx)]).astype(bf16)


def _bias_block_kind(m):
    return jnp.where(m == 0, 0, jnp.where(m == N_Q_BLOCKS - 1, 2, 1))


def _band_bias_tables(rpb_l):
    cols = np.arange(GRID_W)
    col_start = np.clip(cols - WIN_COLS // 2, 0, GRID_W - WIN_COLS)
    col_ok = (cols[None, :] >= col_start[:, None]) & (cols[None, :] < col_start[:, None] + WIN_COLS)
    pad = GRID_W - WIN_COLS
    padded = jnp.pad(rpb_l.astype(f32), ((0, 0), (0, 0), (pad, pad)))
    toeplitz = jnp.stack([padded[:, :, GRID_W - 1 - qc:2 * GRID_W - 1 - qc] for qc in range(GRID_W)], axis=2)
    toeplitz = jnp.where(col_ok[None, None], toeplitz, NEG_INF)
    masked = jnp.full((N_HEADS, GRID_W, GRID_W), NEG_INF, f32)
    tables = []
    for m in (0, 1, N_Q_BLOCKS - 1):
        band0 = int(np.clip(Q_ROWS_PER_BLOCK * m - WIN_ROWS // 2, 0, GRID_ROWS - BAND_ROWS))
        rows = []
        for rho in range(Q_ROWS_PER_BLOCK):
            r = Q_ROWS_PER_BLOCK * m + rho
            win0 = int(np.clip(r - WIN_ROWS // 2, 0, GRID_ROWS - WIN_ROWS))
            blocks = []
            for kap in range(BAND_ROWS):
                kr = band0 + kap
                blocks.append(toeplitz[:, kr - r + WIN_ROWS - 1] if win0 <= kr < win0 + WIN_ROWS else masked)
            rows.append(jnp.concatenate(blocks, axis=2))
        tables.append(jnp.concatenate(rows, axis=1))
    return jnp.stack(tables, axis=1)


def _latent_attention(qkv, cache_k, cache_v, layer_slot, bias_tables):
    tn = 512
    nb = D // tn
    hb = tn // HEAD_DIM
    q_block0 = CTX_ROWS // Q_BLOCK
    seq_block0 = CTX_ROWS // DEC_LEN
    return pl.pallas_call(
        _loc_attn_kernel,
        grid=(N_DEC_SEQ, nb, N_Q_BLOCKS),
        in_specs=[
            pl.BlockSpec((Q_BLOCK, tn), lambda b, g, m: (q_block0 + b * N_Q_BLOCKS + m, g)),
            pl.BlockSpec((DEC_LEN, tn), lambda b, g, m: (seq_block0 + b, nb + g)),
            pl.BlockSpec((DEC_LEN, tn), lambda b, g, m: (seq_block0 + b, 2 * nb + g)),
            pl.BlockSpec((None, None, PAST_LEN, tn), lambda b, g, m: (b, layer_slot, 0, g)),
            pl.BlockSpec((None, None, PAST_LEN, tn), lambda b, g, m: (b, layer_slot, 0, g)),
            pl.BlockSpec((hb, None, Q_BLOCK, BAND), lambda b, g, m: (g, _bias_block_kind(m), 0, 0)),
        ],
        out_specs=pl.BlockSpec((Q_BLOCK, tn), lambda b, g, m: (b * N_Q_BLOCKS + m, g)),
        out_shape=jax.ShapeDtypeStruct((DEC_ROWS, D), bf16),
        compiler_params=_params(("arbitrary", "arbitrary", "arbitrary"),
                                _nbytes((Q_BLOCK, tn), f32) + 2 * _nbytes((DEC_LEN, tn), f32)
                                + 2 * _nbytes((PAST_LEN, tn), f32) + _nbytes((hb, Q_BLOCK, BAND), f32)
                                + _nbytes((Q_BLOCK, tn), bf16),
                                8 * _nbytes((Q_BLOCK, BAND), f32)),
        name="latent_attention",
    )(qkv, qkv, qkv, cache_k, cache_v, bias_tables)


def _router_kernel(x_ref, gain_ref, sc_ref, sh_ref, wr_ref, h_ref, idx_ref, wt_ref):
    tm = x_ref.shape[0]
    h = _norm_mod_rows(x_ref[...], gain_ref[...], 1.0 + sc_ref[...], sh_ref[...]).astype(bf16)
    h_ref[...] = h
    logits = _mm(h, wr_ref[...].astype(bf16))
    lane = lax.broadcasted_iota(jnp.int32, (tm, LANES), 1)
    l1 = jnp.where(lane < N_EXPERTS, logits, -jnp.inf)
    m1 = l1.max(axis=-1, keepdims=True)
    i1 = jnp.where(l1 == m1, lane, LANES).min(axis=-1, keepdims=True)
    l2 = jnp.where(lane == i1, -jnp.inf, l1)
    m2 = l2.max(axis=-1, keepdims=True)
    i2 = jnp.where(l2 == m2, lane, LANES).min(axis=-1, keepdims=True)
    e = jnp.exp(m2 - m1)
    w1 = 1.0 / (1.0 + e)
    w2 = e / (1.0 + e)
    idx_ref[...] = jnp.where(lane == 0, i1, jnp.where(lane == 1, i2, 0))
    wt_ref[...] = jnp.where(lane == 0, w1, jnp.where(lane == 1, w2, 0.0))


def _router(x, mod, norm, w_router_padded, layer):
    tm = 512
    jl = layer // 2
    return pl.pallas_call(
        _router_kernel,
        grid=(T // tm,),
        in_specs=[
            pl.BlockSpec((tm, D), lambda i: (i, 0)),
            pl.BlockSpec((None, 1, D), lambda i: (layer, 0, 0)),
            _mod_spec(layer, 4, tm), _mod_spec(layer, 3, tm),
            pl.BlockSpec((None, D, LANES), lambda i: (jl, 0, 0)),
        ],
        out_specs=[pl.BlockSpec((tm, D), lambda i: (i, 0)),
                   pl.BlockSpec((tm, LANES), lambda i: (i, 0)),
                   pl.BlockSpec((tm, LANES), lambda i: (i, 0))],
        out_shape=[jax.ShapeDtypeStruct((T, D), bf16),
                   jax.ShapeDtypeStruct((T, LANES), jnp.int32),
                   jax.ShapeDtypeStruct((T, LANES), f32)],
        compiler_params=_params(("arbitrary",),
                                _nbytes((tm, D), f32) + _nbytes((tm, D), bf16) + 2 * _nbytes((tm, LANES), f32),
                                _nbytes((D, LANES), f32) + 3 * _nbytes((tm, D), f32)),
        name="router",
    )(x, norm.reshape(DEPTH, 1, D), mod, mod, w_router_padded)


def _routing_plan(expert_ids):
    e_flat = expert_ids.reshape(-1)
    onehot = (e_flat[:, None] == jnp.arange(N_EXPERTS, dtype=jnp.int32)[None, :]).astype(jnp.int32)
    running = jnp.cumsum(onehot, axis=0)
    counts = running[-1]
    rank = jnp.sum((running - onehot) * onehot, axis=1)
    padded = ((counts + MOE_TM - 1) // MOE_TM) * MOE_TM
    group_end = jnp.cumsum(padded)
    group_start = group_end - padded
    dest = group_start[e_flat] + rank
    src_token = jnp.zeros((MOE_ROWS,), jnp.int32).at[dest].set(
        jnp.arange(N_ASSIGN, dtype=jnp.int32) // TOP_K, unique_indices=True)
    n_tiles = (group_end[-1] // MOE_TM).astype(jnp.int32)
    tile_row0 = jnp.arange(MOE_TILES, dtype=jnp.int32) * MOE_TM
    tile_expert = jnp.sum((tile_row0[:, None] >= group_end[None, :]).astype(jnp.int32), axis=1)
    tile_expert = jnp.minimum(tile_expert, N_EXPERTS - 1)
    last_used = tile_expert[jnp.maximum(n_tiles - 1, 0)]
    tile_expert = jnp.where(jnp.arange(MOE_TILES) < n_tiles, tile_expert, last_used).astype(jnp.int32)
    return src_token, dest.astype(jnp.int32), tile_expert, n_tiles.reshape(1)


def _row_copy(src_hbm, dst_hbm, src_row, dst_row, sem):
    return pltpu.make_async_copy(src_hbm.at[src_row], dst_hbm.at[dst_row], sem)


def _dispatch_kernel(src_ref, h_hbm, o_hbm, sem, *, chunk):
    base = pl.program_id(0) * chunk

    def start(r, carry):
        _row_copy(h_hbm, o_hbm, src_ref[base + r], base + r, sem).start()
        return carry

    def wait(r, carry):
        _row_copy(h_hbm, o_hbm, src_ref[base + r], base + r, sem).wait()
        return carry

    lax.fori_loop(0, chunk, start, 0)
    lax.fori_loop(0, chunk, wait, 0)


def _dispatch(h_slabs, src_token):
    chunk = MOE_TM
    return pl.pallas_call(
        functools.partial(_dispatch_kernel, chunk=chunk),
        grid_spec=pltpu.PrefetchScalarGridSpec(
            num_scalar_prefetch=1,
            grid=(MOE_ROWS // chunk,),
            in_specs=[pl.BlockSpec(memory_space=pl.ANY)],
            out_specs=pl.BlockSpec(memory_space=pl.ANY),
            scratch_shapes=[pltpu.SemaphoreType.DMA(())],
        ),
        out_shape=jax.ShapeDtypeStruct((MOE_ROWS,) + h_slabs.shape[1:], h_slabs.dtype),
        compiler_params=pltpu.CompilerParams(dimension_semantics=("arbitrary",)),
        name="moe_dispatch",
    )(src_token, h_slabs)


def _expert_changed(te_ref, i):
    return (i == 0) | (te_ref[i] != te_ref[jnp.maximum(i - 1, 0)])


def _moe_up_kernel(te_ref, nt_ref, hs_ref, wg_ref, wu_ref, a_ref, wg_s, wu_s):
    i = pl.program_id(1)

    @pl.when(_expert_changed(te_ref, i))
    def _():
        wg_s[...] = wg_ref[...].astype(bf16)
        wu_s[...] = wu_ref[...].astype(bf16)

    @pl.when(i < nt_ref[0])
    def _():
        h = hs_ref[...]
        g = _mm(h, wg_s[...])
        u = _mm(h, wu_s[...])
        a_ref[...] = ((g * jax.nn.sigmoid(g)) * u).astype(bf16)

    @pl.when(i >= nt_ref[0])
    def _():
        a_ref[...] = jnp.zeros(a_ref.shape, a_ref.dtype)


def _moe_up(hs, w_gate, w_up, tile_expert, n_tiles, layer):
    tm, tn = MOE_TM, 512
    jl = layer // 2
    return pl.pallas_call(
        _moe_up_kernel,
        grid_spec=pltpu.PrefetchScalarGridSpec(
            num_scalar_prefetch=2,
            grid=(D_FF // tn, MOE_TILES),
            in_specs=[
                pl.BlockSpec((tm, D), lambda j, i, te, nt: (i, 0)),
                pl.BlockSpec((None, None, D, tn), lambda j, i, te, nt: (jl, te[i], 0, j)),
                pl.BlockSpec((None, None, D, tn), lambda j, i, te, nt: (jl, te[i], 0, j)),
            ],
            out_specs=pl.BlockSpec((tm, tn), lambda j, i, te, nt: (i, j)),
            scratch_shapes=[pltpu.VMEM((D, tn), bf16), pltpu.VMEM((D, tn), bf16)],
        ),
        out_shape=jax.ShapeDtypeStruct((MOE_ROWS, D_FF), bf16),
        compiler_params=_params(("arbitrary", "arbitrary"),
                                _nbytes((tm, D), bf16) + 2 * _nbytes((D, tn), f32) + _nbytes((tm, tn), bf16),
                                2 * _nbytes((D, tn), bf16) + 3 * _nbytes((tm, tn), f32)),
        name="moe_up",
    )(tile_expert, n_tiles, hs, w_gate, w_up)


def _moe_down_kernel(te_ref, nt_ref, a_ref, wd_ref, y_ref, wd_s):
    i = pl.program_id(1)

    @pl.when(_expert_changed(te_ref, i))
    def _():
        wd_s[...] = wd_ref[...].astype(bf16)

    @pl.when(i < nt_ref[0])
    def _():
        y_ref[...] = _mm(a_ref[...], wd_s[...])

    @pl.when(i >= nt_ref[0])
    def _():
        y_ref[...] = jnp.zeros(y_ref.shape, y_ref.dtype)


def _moe_down(a, w_down, tile_expert, n_tiles, layer):
    tm, tn = MOE_TM, 512
    jl = layer // 2
    return pl.pallas_call(
        _moe_down_kernel,
        grid_spec=pltpu.PrefetchScalarGridSpec(
            num_scalar_prefetch=2,
            grid=(D // tn, MOE_TILES),
            in_specs=[
                pl.BlockSpec((tm, D_FF), lambda j, i, te, nt: (i, 0)),
                pl.BlockSpec((None, None, D_FF, tn), lambda j, i, te, nt: (jl, te[i], 0, j)),
            ],
            out_specs=pl.BlockSpec((tm, tn), lambda j, i, te, nt: (i, j)),
            scratch_shapes=[pltpu.VMEM((D_FF, tn), bf16)],
        ),
        out_shape=jax.ShapeDtypeStruct((MOE_ROWS, D), f32),
        compiler_params=_params(("arbitrary", "arbitrary"),
                                _nbytes((tm, D_FF), bf16) + _nbytes((D_FF, tn), f32) + _nbytes((tm, tn), f32),
                                _nbytes((D_FF, tn), bf16) + _nbytes((tm, tn), f32)),
        name="moe_down",
    )(tile_expert, n_tiles, a, w_down)


def _combine_kernel(pos_ref, x_ref, gate_ref, wt_ref, y_hbm, o_ref, ybuf, sem, *, tm):
    base = pl.program_id(0) * tm * TOP_K

    def copies(t):
        return [pltpu.make_async_copy(y_hbm.at[pos_ref[base + t * TOP_K + c]], ybuf.at[c, t], sem.at[c])
                for c in range(TOP_K)]

    def start(t, carry):
        for cp in copies(t):
            cp.start()
        return carry

    def wait(t, carry):
        for cp in copies(t):
            cp.wait()
        return carry

    lax.fori_loop(0, tm, start, 0)
    lax.fori_loop(0, tm, wait, 0)
    w1 = wt_ref[:, 0:1]
    w2 = wt_ref[:, 1:2]
    for s in range(ROW_SLABS):
        sl = slice(s * LANES, (s + 1) * LANES)
        mix = w1 * ybuf[0, :, s, :] + w2 * ybuf[1, :, s, :]
        o_ref[:, sl] = x_ref[:, sl] + gate_ref[:, sl] * mix


def _combine(x, y_slabs, pos, wts, mod, layer):
    tm = 256
    return pl.pallas_call(
        functools.partial(_combine_kernel, tm=tm),
        grid_spec=pltpu.PrefetchScalarGridSpec(
            num_scalar_prefetch=1,
            grid=(T // tm,),
            in_specs=[
                pl.BlockSpec((tm, D), lambda i, pos: (i, 0)),
                _mod_spec(layer, 5, tm),
                pl.BlockSpec((tm, LANES), lambda i, pos: (i, 0)),
                pl.BlockSpec(memory_space=pl.ANY),
            ],
            out_specs=pl.BlockSpec((tm, D), lambda i, pos: (i, 0)),
            scratch_shapes=[pltpu.VMEM((TOP_K, tm, ROW_SLABS, LANES), f32),
                            pltpu.SemaphoreType.DMA((TOP_K,))],
        ),
        out_shape=jax.ShapeDtypeStruct((T, D), f32),
        compiler_params=_params(("arbitrary",),
                                2 * _nbytes((tm, D), f32) + _nbytes((tm, LANES), f32),
                                _nbytes((TOP_K, tm, D), f32) + 4 * _nbytes((tm, LANES), f32)),
        name="moe_combine",
    )(pos, x, mod, wts, y_slabs)


def _moe_layer(x, mod, norm2, w_router, w_gate, w_up, w_down, layer):
    wr = jnp.pad(w_router, ((0, 0), (0, 0), (0, LANES - N_EXPERTS)))
    h, idx, wts = _router(x, mod, norm2, wr, layer)
    src_token, pos, tile_expert, n_tiles = _routing_plan(idx[:, :TOP_K])
    hs = _dispatch(h.reshape(T, ROW_SLABS, LANES), src_token).reshape(MOE_ROWS, D)
    a = _moe_up(hs, w_gate, w_up, tile_expert, n_tiles, layer)
    y = _moe_down(a, w_down, tile_expert, n_tiles, layer)
    return _combine(x, y.reshape(MOE_ROWS, ROW_SLABS, LANES), pos, wts, mod, layer)


def kernel(x_prompt, x_sample, cache_k, cache_v, c, c_ctx, norm1, norm2, w_ada, b_ada, w_conv_in, conv_w,
           w_conv_out, w_qkv, q_norm, k_norm, rpb, w_attn_out, w_ffn_gate, w_ffn_up, w_ffn_down, w_router,
           w_moe_gate, w_moe_up, w_moe_down):
    x = jnp.concatenate([x_prompt.reshape(CTX_ROWS, D), x_sample.reshape(DEC_ROWS, D)], axis=0)
    cond = jnp.concatenate([c_ctx[None, :], c, jnp.zeros((N_GROUPS_PADDED - 1 - N_DEC_SEQ, D), f32)], axis=0)
    mod = _ada_params(cond, w_ada, b_ada)
    cache_k2 = cache_k.reshape(N_DEC_SEQ, -1, PAST_LEN, D)
    cache_v2 = cache_v.reshape(N_DEC_SEQ, -1, PAST_LEN, D)

    new_k, new_v = [], []
    for layer in range(DEPTH):
        jl = layer // 2
        if layer % 2 == 0:
            b, u = _conv_in(x, mod, norm1, w_conv_in, layer)
            gated = _conv_gate(b, u, conv_w, layer)
            x = _mm_residual(gated, w_conv_out, x, mod, layer, 2, 1024, 512, "conv_out")
            a = _ffn_up(x, mod, norm2, w_ffn_gate, w_ffn_up, layer)
            x = _mm_residual(a, w_ffn_down, x, mod, layer, 5, 1024, 256, "ffn_down")
        else:
            qkv = _qkv(x, mod, norm1, w_qkv, q_norm, k_norm, layer)
            new_k.append(qkv[:CTX_ROWS, D:2 * D].reshape(N_CTX_SEQ, CTX_LEN, N_HEADS, HEAD_DIM))
            new_v.append(qkv[:CTX_ROWS, 2 * D:].reshape(N_CTX_SEQ, CTX_LEN, N_HEADS, HEAD_DIM))
            o_ctx = _ctx_attention(qkv)
            o_dec = _latent_attention(qkv, cache_k2, cache_v2, jl, _band_bias_tables(rpb[jl]))
            o = jnp.concatenate([o_ctx, o_dec], axis=0)
            x = _mm_residual(o, w_attn_out, x, mod, layer, 2, 1024, 512, "attn_out")
            x = _moe_layer(x, mod, norm2, w_router, w_moe_gate, w_moe_up, w_moe_down, layer)

    y_prompt = x[:CTX_ROWS].reshape(N_CTX_SEQ, CTX_LEN, D)
    y_sample = x[CTX_ROWS:].reshape(N_DEC_SEQ, DEC_LEN, D)
    return y_prompt, y_sample, jnp.stack(new_k, axis=1), jnp.stack(new_v, axis=1)
```

```python
import functools

import numpy as np
import jax
import jax.numpy as jnp
from jax import lax
from jax.experimental import pallas as pl
from jax.experimental.pallas import tpu as pltpu

D = 2048
N_CTX_SEQ = 16
CTX_LEN = 256
N_DEC_SEQ = 4
DEC_LEN = 2048
CTX_ROWS = N_CTX_SEQ * CTX_LEN
DEC_ROWS = N_DEC_SEQ * DEC_LEN
T = CTX_ROWS + DEC_ROWS
DEPTH = 4
PAST_LEN = 512
N_HEADS = 16
HEAD_DIM = 128
GRID_W = 64
GRID_ROWS = DEC_LEN // GRID_W
WIN_ROWS = 8
WIN_COLS = 16
D_FF = 5632
N_EXPERTS = 8
TOP_K = 2
EPS = 1e-6
NEG_INF = -1e30
ATTN_SCALE = HEAD_DIM ** -0.5

N_GROUPS_PADDED = 8
LANES = 128
SUBLANES = 8
VMEM_CAP_BYTES = 56 * 2 ** 20
VMEM_HEADROOM = 4 * 2 ** 20

Q_ROWS_PER_BLOCK = 4
Q_BLOCK = Q_ROWS_PER_BLOCK * GRID_W
N_Q_BLOCKS = GRID_ROWS // Q_ROWS_PER_BLOCK
BAND_ROWS = Q_ROWS_PER_BLOCK + WIN_ROWS
BAND = BAND_ROWS * GRID_W
MASKED_BLOCK = 2 * WIN_ROWS - 1
N_BIAS_BLOCKS = MASKED_BLOCK + 1

MOE_TM = 512
DMA_UNROLL = 8
N_ASSIGN = T * TOP_K
MOE_ROWS = N_ASSIGN + N_EXPERTS * MOE_TM
MOE_TILES = MOE_ROWS // MOE_TM
ROW_SLABS = D // LANES

bf16 = jnp.bfloat16
f32 = jnp.float32


def _mm(a, b):
    return jnp.dot(a, b, preferred_element_type=f32)


def _mm_nt(a, b):
    return lax.dot_general(a, b, (((1,), (1,)), ((), ())), preferred_element_type=f32)


def _nbytes(shape, dtype):
    return int(np.prod(shape)) * jnp.dtype(dtype).itemsize


def _params(semantics, pipelined_bytes, resident_bytes):
    need = 2 * pipelined_bytes + resident_bytes + VMEM_HEADROOM
    return pltpu.CompilerParams(dimension_semantics=semantics,
                                vmem_limit_bytes=int(min(max(need, 16 * 2 ** 20), VMEM_CAP_BYTES)))


def _group_of_tile(i, tm):
    n_ctx = CTX_ROWS // tm
    per_seq = DEC_LEN // tm
    return jnp.where(i < n_ctx, 0, 1 + (i - n_ctx) // per_seq)


def _mod_spec(layer, which, tm, width=D):
    def index(*grid_and_prefetch):
        i = grid_and_prefetch[0]
        j = grid_and_prefetch[1] if width != D else 0
        return ((layer * N_GROUPS_PADDED + _group_of_tile(i, tm)) * 6 + which, 0, j)
    return pl.BlockSpec((None, 1, width), index)


def _norm_mod_rows(x, gain, scale1p, shift):
    ms = jnp.mean(x * x, axis=-1, keepdims=True)
    y = x * lax.rsqrt(ms + EPS)
    return (y * gain) * scale1p + shift


def _norm_mod_to_scratch(x_ref, gain_ref, sc_ref, sh_ref, h_scr, rows=256):
    gain = gain_ref[...]
    scale1p = 1.0 + sc_ref[...]
    shift = sh_ref[...]

    def body(c, carry):
        r = pl.multiple_of(c * rows, rows)
        h = _norm_mod_rows(x_ref[pl.ds(r, rows), :], gain, scale1p, shift)
        h_scr[pl.ds(r, rows), :] = h.astype(bf16)
        return carry

    lax.fori_loop(0, x_ref.shape[0] // rows, body, 0)


def _ada_kernel(cond_ref, w_ref, b_ref, o_ref):
    c = cond_ref[...]
    s = (c * jax.nn.sigmoid(c)).astype(bf16)
    o_ref[...] = _mm(s, w_ref[...].astype(bf16)) + b_ref[...]


def _ada_params(cond, w_ada, b_ada):
    tn = 1024
    out = pl.pallas_call(
        _ada_kernel,
        grid=(DEPTH, 6 * D // tn),
        in_specs=[
            pl.BlockSpec((N_GROUPS_PADDED, D), lambda l, j: (0, 0)),
            pl.BlockSpec((None, D, tn), lambda l, j: (l, 0, j)),
            pl.BlockSpec((None, 1, tn), lambda l, j: (l, 0, j)),
        ],
        out_specs=pl.BlockSpec((None, N_GROUPS_PADDED, tn), lambda l, j: (l, 0, j)),
        out_shape=jax.ShapeDtypeStruct((DEPTH, N_GROUPS_PADDED, 6 * D), f32),
        compiler_params=_params(("arbitrary", "arbitrary"),
                                _nbytes((D, tn), f32) + _nbytes((8, tn), f32) * 2,
                                _nbytes((D, tn), bf16) + _nbytes((8, D), f32) * 2),
        name="ada_params",
    )(cond, w_ada, b_ada.reshape(DEPTH, 1, 6 * D))
    return out.reshape(DEPTH * N_GROUPS_PADDED * 6, 1, D)


def _conv_in_kernel(x_ref, gain_ref, sc_ref, sh_ref, wb_ref, wc_ref, wv_ref, b_ref, u_ref, h_scr):
    @pl.when(pl.program_id(1) == 0)
    def _():
        _norm_mod_to_scratch(x_ref, gain_ref, sc_ref, sh_ref, h_scr)

    h = h_scr[...]
    b_ref[...] = _mm(h, wb_ref[...])
    u_ref[...] = _mm(h, wc_ref[...]) * _mm(h, wv_ref[...])


def _conv_in(x, mod, norm, w_conv_in, layer):
    tm, tn = 1024, 256
    jl = layer // 2
    nb = D // tn
    return pl.pallas_call(
        _conv_in_kernel,
        grid=(T // tm, nb),
        in_specs=[
            pl.BlockSpec((tm, D), lambda i, j: (i, 0)),
            pl.BlockSpec((None, 1, D), lambda i, j: (layer, 0, 0)),
            _mod_spec(layer, 1, tm), _mod_spec(layer, 0, tm),
            pl.BlockSpec((None, D, tn), lambda i, j: (jl, 0, j)),
            pl.BlockSpec((None, D, tn), lambda i, j: (jl, 0, nb + j)),
            pl.BlockSpec((None, D, tn), lambda i, j: (jl, 0, 2 * nb + j)),
        ],
        out_specs=[pl.BlockSpec((tm, tn), lambda i, j: (i, j))] * 2,
        out_shape=[jax.ShapeDtypeStruct((T, D), f32)] * 2,
        scratch_shapes=[pltpu.VMEM((tm, D), bf16)],
        compiler_params=_params(("arbitrary", "arbitrary"),
                                _nbytes((tm, D), f32) + 3 * _nbytes((D, tn), bf16) + 2 * _nbytes((tm, tn), f32),
                                _nbytes((tm, D), bf16) + 3 * _nbytes((tm, tn), f32)),
        name="conv_in",
    )(x, norm.reshape(DEPTH, 1, D), mod, mod, w_conv_in, w_conv_in, w_conv_in)


def _ffn_up_kernel(x_ref, gain_ref, sc_ref, sh_ref, wg_ref, wu_ref, a_ref, h_scr):
    @pl.when(pl.program_id(1) == 0)
    def _():
        _norm_mod_to_scratch(x_ref, gain_ref, sc_ref, sh_ref, h_scr)

    h = h_scr[...]
    g = _mm(h, wg_ref[...])
    u = _mm(h, wu_ref[...])
    a_ref[...] = ((g * jax.nn.sigmoid(g)) * u).astype(bf16)


def _ffn_up(x, mod, norm, w_gate, w_up, layer):
    tm, tn = 1024, 512
    jl = layer // 2
    return pl.pallas_call(
        _ffn_up_kernel,
        grid=(T // tm, D_FF // tn),
        in_specs=[
            pl.BlockSpec((tm, D), lambda i, j: (i, 0)),
            pl.BlockSpec((None, 1, D), lambda i, j: (layer, 0, 0)),
            _mod_spec(layer, 4, tm), _mod_spec(layer, 3, tm),
            pl.BlockSpec((None, D, tn), lambda i, j: (jl, 0, j)),
            pl.BlockSpec((None, D, tn), lambda i, j: (jl, 0, j)),
        ],
        out_specs=pl.BlockSpec((tm, tn), lambda i, j: (i, j)),
        out_shape=jax.ShapeDtypeStruct((T, D_FF), bf16),
        scratch_shapes=[pltpu.VMEM((tm, D), bf16)],
        compiler_params=_params(("arbitrary", "arbitrary"),
                                _nbytes((tm, D), f32) + 2 * _nbytes((D, tn), bf16) + _nbytes((tm, tn), bf16),
                                _nbytes((tm, D), bf16) + 3 * _nbytes((tm, tn), f32)),
        name="ffn_up",
    )(x, norm.reshape(DEPTH, 1, D), mod, mod, w_gate, w_up)


def _qkv_kernel(x_ref, gain_ref, sc_ref, sh_ref, w_ref, qg_ref, kg_ref, o_ref, h_scr, *, n_q_blocks):
    j = pl.program_id(1)

    @pl.when(j == 0)
    def _():
        _norm_mod_to_scratch(x_ref, gain_ref, sc_ref, sh_ref, h_scr)

    acc = _mm(h_scr[...], w_ref[...])

    @pl.when(j >= 2 * n_q_blocks)
    def _():
        o_ref[...] = acc

    @pl.when(j < 2 * n_q_blocks)
    def _():
        head_gain = jnp.where(j < n_q_blocks, qg_ref[...], kg_ref[...])
        for hh in range(acc.shape[1] // HEAD_DIM):
            a = acc[:, hh * HEAD_DIM:(hh + 1) * HEAD_DIM]
            ms = jnp.mean(a * a, axis=-1, keepdims=True)
            o_ref[:, hh * HEAD_DIM:(hh + 1) * HEAD_DIM] = (a * lax.rsqrt(ms + EPS)) * head_gain


def _qkv(x, mod, norm, w_qkv, q_norm, k_norm, layer):
    tm, tn = 1024, 1024
    jl = layer // 2
    return pl.pallas_call(
        functools.partial(_qkv_kernel, n_q_blocks=D // tn),
        grid=(T // tm, 3 * D // tn),
        in_specs=[
            pl.BlockSpec((tm, D), lambda i, j: (i, 0)),
            pl.BlockSpec((None, 1, D), lambda i, j: (layer, 0, 0)),
            _mod_spec(layer, 1, tm), _mod_spec(layer, 0, tm),
            pl.BlockSpec((None, D, tn), lambda i, j: (jl, 0, j)),
            pl.BlockSpec((None, 1, HEAD_DIM), lambda i, j: (jl, 0, 0)),
            pl.BlockSpec((None, 1, HEAD_DIM), lambda i, j: (jl, 0, 0)),
        ],
        out_specs=pl.BlockSpec((tm, tn), lambda i, j: (i, j)),
        out_shape=jax.ShapeDtypeStruct((T, 3 * D), f32),
        scratch_shapes=[pltpu.VMEM((tm, D), bf16)],
        compiler_params=_params(("arbitrary", "arbitrary"),
                                _nbytes((tm, D), f32) + _nbytes((D, tn), bf16) + _nbytes((tm, tn), f32),
                                _nbytes((tm, D), bf16) + 2 * _nbytes((tm, tn), f32)),
        name="qkv",
    )(x, norm.reshape(DEPTH, 1, D), mod, mod, w_qkv,
      q_norm.reshape(-1, 1, HEAD_DIM), k_norm.reshape(-1, 1, HEAD_DIM))


def _mm_res_kernel(*refs, n_ctx_tiles):
    *a_refs, w_ref, x_ref, gate_ref, o_ref = refs

    def update(a_ref):
        o_ref[...] = x_ref[...] + gate_ref[...] * _mm(a_ref[...], w_ref[...])

    if len(a_refs) == 1:
        update(a_refs[0])
    else:
        i = pl.program_id(0)

        @pl.when(i < n_ctx_tiles)
        def _():
            update(a_refs[0])

        @pl.when(i >= n_ctx_tiles)
        def _():
            update(a_refs[1])


def _mm_residual(a_parts, w, x, mod, layer, which, tm, tn, name):
    k = a_parts[0].shape[1]
    jl = layer // 2
    n_ctx_tiles = CTX_ROWS // tm
    if len(a_parts) == 1:
        a_specs = [pl.BlockSpec((tm, k), lambda i, j: (i, 0))]
    else:
        a_specs = [pl.BlockSpec((tm, k), lambda i, j: (jnp.minimum(i, n_ctx_tiles - 1), 0)),
                   pl.BlockSpec((tm, k), lambda i, j: (jnp.maximum(i - n_ctx_tiles, 0), 0))]
    return pl.pallas_call(
        functools.partial(_mm_res_kernel, n_ctx_tiles=n_ctx_tiles),
        grid=(T // tm, D // tn),
        in_specs=a_specs + [
            pl.BlockSpec((None, k, tn), lambda i, j: (jl, 0, j)),
            pl.BlockSpec((tm, tn), lambda i, j: (i, j)),
            _mod_spec(layer, which, tm, tn),
        ],
        out_specs=pl.BlockSpec((tm, tn), lambda i, j: (i, j)),
        out_shape=jax.ShapeDtypeStruct((T, D), f32),
        compiler_params=_params(("arbitrary", "arbitrary"),
                                len(a_parts) * _nbytes((tm, k), bf16) + _nbytes((k, tn), bf16)
                                + 2 * _nbytes((tm, tn), f32),
                                2 * _nbytes((tm, tn), f32)),
        name=name,
    )(*a_parts, w, x, mod)


def _conv_gate_kernel(b_ref, u_ref, up_ref, un_ref, cw_ref, o_ref, *, tm, cols):
    row0 = pl.program_id(0) * tm
    seq_len = jnp.where(row0 < CTX_ROWS, CTX_LEN, DEC_LEN)
    local = lax.broadcasted_iota(jnp.int32, (tm, cols), 0)
    pos = (row0 + local) & (seq_len - 1)
    is_first = pos == 0
    is_last = pos == seq_len - 1
    for c in range(D // cols):
        sl = slice(c * cols, (c + 1) * cols)
        u = u_ref[:, sl]
        prev_row = up_ref[SUBLANES - 1:SUBLANES, sl]
        next_row = un_ref[0:1, sl]
        u_prev = jnp.where(local == 0, prev_row, pltpu.roll(u, 1, axis=0))
        u_next = jnp.where(local == tm - 1, next_row, pltpu.roll(u, tm - 1, axis=0))
        u_prev = jnp.where(is_first, 0.0, u_prev)
        u_next = jnp.where(is_last, 0.0, u_next)
        conv = u_prev * cw_ref[0:1, sl] + u * cw_ref[1:2, sl] + u_next * cw_ref[2:3, sl]
        o_ref[:, sl] = (b_ref[:, sl] * conv).astype(bf16)


def _conv_gate(b, u, conv_w, layer):
    tm, cols = 512, 512
    jl = layer // 2
    blocks_per_tile = tm // SUBLANES
    return pl.pallas_call(
        functools.partial(_conv_gate_kernel, tm=tm, cols=cols),
        grid=(T // tm,),
        in_specs=[
            pl.BlockSpec((tm, D), lambda i: (i, 0)),
            pl.BlockSpec((tm, D), lambda i: (i, 0)),
            pl.BlockSpec((SUBLANES, D), lambda i: (jnp.maximum(i * blocks_per_tile - 1, 0), 0)),
            pl.BlockSpec((SUBLANES, D), lambda i: (jnp.minimum((i + 1) * blocks_per_tile, T // SUBLANES - 1), 0)),
            pl.BlockSpec((None, 3, D), lambda i: (jl, 0, 0)),
        ],
        out_specs=pl.BlockSpec((tm, D), lambda i: (i, 0)),
        out_shape=jax.ShapeDtypeStruct((T, D), bf16),
        compiler_params=_params(("arbitrary",),
                                2 * _nbytes((tm, D), f32) + _nbytes((tm, D), bf16) + 2 * _nbytes((SUBLANES, D), f32),
                                8 * _nbytes((tm, cols), f32)),
        name="conv_gate",
    )(b, u, u, u, conv_w)


def _softmax_pv(parts):
    m = parts[0][0].max(axis=-1, keepdims=True)
    for s, _ in parts[1:]:
        m = jnp.maximum(m, s.max(axis=-1, keepdims=True))
    den = None
    out = None
    for s, v in parts:
        p = jnp.exp(s - m)
        d = p.sum(axis=-1, keepdims=True)
        o = _mm(p.astype(bf16), v)
        den = d if den is None else den + d
        out = o if out is None else out + o
    return out * (1.0 / den)


def _ctx_attn_kernel(q_ref, k_ref, v_ref, o_ref, k_out_ref, v_out_ref):
    k_out_ref[...] = k_ref[...]
    v_out_ref[...] = v_ref[...]
    for hh in range(q_ref.shape[1] // HEAD_DIM):
        sl = slice(hh * HEAD_DIM, (hh + 1) * HEAD_DIM)
        q = q_ref[:, sl].astype(bf16)
        k = k_ref[:, sl].astype(bf16)
        v = v_ref[:, sl].astype(bf16)
        s = _mm_nt(q, k) * ATTN_SCALE
        o_ref[:, sl] = _softmax_pv([(s, v)]).astype(bf16)


def _ctx_attention(qkv):
    tn = 512
    nb = D // tn
    return pl.pallas_call(
        _ctx_attn_kernel,
        grid=(N_CTX_SEQ, nb),
        in_specs=[
            pl.BlockSpec((CTX_LEN, tn), lambda b, g: (b, g)),
            pl.BlockSpec((CTX_LEN, tn), lambda b, g: (b, nb + g)),
            pl.BlockSpec((CTX_LEN, tn), lambda b, g: (b, 2 * nb + g)),
        ],
        out_specs=[pl.BlockSpec((CTX_LEN, tn), lambda b, g: (b, g))] * 3,
        out_shape=[jax.ShapeDtypeStruct((CTX_ROWS, D), bf16),
                   jax.ShapeDtypeStruct((CTX_ROWS, D), f32),
                   jax.ShapeDtypeStruct((CTX_ROWS, D), f32)],
        compiler_params=_params(("arbitrary", "arbitrary"),
                                5 * _nbytes((CTX_LEN, tn), f32) + _nbytes((CTX_LEN, tn), bf16),
                                8 * _nbytes((CTX_LEN, CTX_LEN), f32)),
        name="ctx_attention",
    )(qkv, qkv, qkv)


def _band_start_row(m):
    return jnp.clip(Q_ROWS_PER_BLOCK * m - WIN_ROWS // 2, 0, GRID_ROWS - BAND_ROWS)


def _band_block_ids(m):
    band0 = _band_start_row(m)
    ids = []
    for rho in range(Q_ROWS_PER_BLOCK):
        r = Q_ROWS_PER_BLOCK * m + rho
        win0 = jnp.clip(r - WIN_ROWS // 2, 0, GRID_ROWS - WIN_ROWS)
        row_ids = []
        for kap in range(BAND_ROWS):
            kr = band0 + kap
            inside = (kr >= win0) & (kr < win0 + WIN_ROWS)
            row_ids.append(jnp.where(inside, kr - r + WIN_ROWS - 1, MASKED_BLOCK))
        ids.append(row_ids)
    return ids


def _band_bias(bias_ref, hh, ids):
    left_half = lax.broadcasted_iota(jnp.int32, (GRID_W, 2 * GRID_W), 1) < GRID_W
    rows = []
    for rho in range(Q_ROWS_PER_BLOCK):
        pairs = [jnp.where(left_half, bias_ref[hh, ids[rho][kap]], bias_ref[hh, ids[rho][kap + 1]])
                 for kap in range(0, BAND_ROWS, 2)]
        rows.append(jnp.concatenate(pairs, axis=1))
    return jnp.concatenate(rows, axis=0)


def _loc_attn_kernel(q_ref, k_ref, v_ref, ck_ref, cv_ref, bias_ref, o_ref):
    start = pl.multiple_of(_band_start_row(pl.program_id(2)) * GRID_W, GRID_W)
    ids = _band_block_ids(pl.program_id(2))
    for hh in range(q_ref.shape[1] // HEAD_DIM):
        sl = slice(hh * HEAD_DIM, (hh + 1) * HEAD_DIM)
        q = q_ref[:, sl].astype(bf16)
        k_band = k_ref[pl.ds(start, BAND), sl].astype(bf16)
        v_band = v_ref[pl.ds(start, BAND), sl].astype(bf16)
        k_ctx = ck_ref[:, sl].astype(bf16)
        v_ctx = cv_ref[:, sl].astype(bf16)
        s_loc = _mm_nt(q, k_band) * ATTN_SCALE + _band_bias(bias_ref, hh, ids)
        s_ctx = _mm_nt(q, k_ctx) * ATTN_SCALE
        o_ref[:, sl] = _softmax_pv([(s_loc, v_band), (s_ctx, v_ctx)]).astype(bf16)


def _rel_bias_blocks(rpb_l):
    cols = np.arange(GRID_W)
    col_start = np.clip(cols - WIN_COLS // 2, 0, GRID_W - WIN_COLS)
    col_ok = (cols[None, :] >= col_start[:, None]) & (cols[None, :] < col_start[:, None] + WIN_COLS)
    pad = GRID_W - WIN_COLS
    padded = jnp.pad(rpb_l.astype(f32), ((0, 0), (0, 0), (pad, pad)))
    toeplitz = jnp.stack([padded[:, :, GRID_W - 1 - qc:2 * GRID_W - 1 - qc] for qc in range(GRID_W)], axis=2)
    toeplitz = jnp.where(col_ok[None, None], toeplitz, NEG_INF)
    masked = jnp.full((N_HEADS, 1, GRID_W, GRID_W), NEG_INF, f32)
    blocks = jnp.concatenate([toeplitz, masked], axis=1)
    return jnp.concatenate([blocks, blocks], axis=3)


def _latent_attention(qkv, cache_k, cache_v, layer_slot, bias_blocks):
    tn = 512
    nb = D // tn
    hb = tn // HEAD_DIM
    q_block0 = CTX_ROWS // Q_BLOCK
    seq_block0 = CTX_ROWS // DEC_LEN
    return pl.pallas_call(
        _loc_attn_kernel,
        grid=(N_DEC_SEQ, nb, N_Q_BLOCKS),
        in_specs=[
            pl.BlockSpec((Q_BLOCK, tn), lambda b, g, m: (q_block0 + b * N_Q_BLOCKS + m, g)),
            pl.BlockSpec((DEC_LEN, tn), lambda b, g, m: (seq_block0 + b, nb + g)),
            pl.BlockSpec((DEC_LEN, tn), lambda b, g, m: (seq_block0 + b, 2 * nb + g)),
            pl.BlockSpec((None, None, PAST_LEN, tn), lambda b, g, m: (b, layer_slot, 0, g)),
            pl.BlockSpec((None, None, PAST_LEN, tn), lambda b, g, m: (b, layer_slot, 0, g)),
            pl.BlockSpec((hb, N_BIAS_BLOCKS, GRID_W, 2 * GRID_W), lambda b, g, m: (g, 0, 0, 0)),
        ],
        out_specs=pl.BlockSpec((Q_BLOCK, tn), lambda b, g, m: (b * N_Q_BLOCKS + m, g)),
        out_shape=jax.ShapeDtypeStruct((DEC_ROWS, D), bf16),
        compiler_params=_params(("arbitrary", "arbitrary", "arbitrary"),
                                _nbytes((Q_BLOCK, tn), f32) + 2 * _nbytes((DEC_LEN, tn), f32)
                                + 2 * _nbytes((PAST_LEN, tn), f32)
                                + _nbytes((hb, N_BIAS_BLOCKS, GRID_W, 2 * GRID_W), f32)
                                + _nbytes((Q_BLOCK, tn), bf16),
                                10 * _nbytes((Q_BLOCK, BAND), f32)),
        name="latent_attention",
    )(qkv, qkv, qkv, cache_k, cache_v, bias_blocks)


def _router_kernel(x_ref, gain_ref, sc_ref, sh_ref, wr_ref, h_ref, idx_ref, wt_ref):
    tm = x_ref.shape[0]
    h = _norm_mod_rows(x_ref[...], gain_ref[...], 1.0 + sc_ref[...], sh_ref[...]).astype(bf16)
    h_ref[...] = h
    logits = _mm(h, wr_ref[...].astype(bf16))
    lane = lax.broadcasted_iota(jnp.int32, (tm, LANES), 1)
    l1 = jnp.where(lane < N_EXPERTS, logits, -jnp.inf)
    m1 = l1.max(axis=-1, keepdims=True)
    i1 = jnp.where(l1 == m1, lane, LANES).min(axis=-1, keepdims=True)
    l2 = jnp.where(lane == i1, -jnp.inf, l1)
    m2 = l2.max(axis=-1, keepdims=True)
    i2 = jnp.where(l2 == m2, lane, LANES).min(axis=-1, keepdims=True)
    e = jnp.exp(m2 - m1)
    w1 = 1.0 / (1.0 + e)
    w2 = e / (1.0 + e)
    idx_ref[...] = jnp.where(lane == 0, i1, jnp.where(lane == 1, i2, 0))
    wt_ref[...] = jnp.where(lane == 0, w1, jnp.where(lane == 1, w2, 0.0))


def _router(x, mod, norm, w_router_padded, layer):
    tm = 512
    jl = layer // 2
    return pl.pallas_call(
        _router_kernel,
        grid=(T // tm,),
        in_specs=[
            pl.BlockSpec((tm, D), lambda i: (i, 0)),
            pl.BlockSpec((None, 1, D), lambda i: (layer, 0, 0)),
            _mod_spec(layer, 4, tm), _mod_spec(layer, 3, tm),
            pl.BlockSpec((None, D, LANES), lambda i: (jl, 0, 0)),
        ],
        out_specs=[pl.BlockSpec((tm, D), lambda i: (i, 0)),
                   pl.BlockSpec((tm, LANES), lambda i: (i, 0)),
                   pl.BlockSpec((tm, LANES), lambda i: (i, 0))],
        out_shape=[jax.ShapeDtypeStruct((T, D), bf16),
                   jax.ShapeDtypeStruct((T, LANES), jnp.int32),
                   jax.ShapeDtypeStruct((T, LANES), f32)],
        compiler_params=_params(("arbitrary",),
                                _nbytes((tm, D), f32) + _nbytes((tm, D), bf16) + 2 * _nbytes((tm, LANES), f32),
                                _nbytes((D, LANES), f32) + 3 * _nbytes((tm, D), f32)),
        name="router",
    )(x, norm.reshape(DEPTH, 1, D), mod, mod, w_router_padded)


def _routing_plan(expert_ids):
    e_flat = expert_ids.reshape(-1)
    onehot = (e_flat[:, None] == jnp.arange(N_EXPERTS, dtype=jnp.int32)[None, :]).astype(jnp.int32)
    running = jnp.cumsum(onehot, axis=0)
    counts = running[-1]
    rank = jnp.sum((running - onehot) * onehot, axis=1)
    padded = ((counts + MOE_TM - 1) // MOE_TM) * MOE_TM
    group_end = jnp.cumsum(padded)
    group_start = group_end - padded
    dest = group_start[e_flat] + rank
    src_token = jnp.zeros((MOE_ROWS,), jnp.int32).at[dest].set(
        jnp.arange(N_ASSIGN, dtype=jnp.int32) // TOP_K, unique_indices=True)
    n_tiles = (group_end[-1] // MOE_TM).astype(jnp.int32)
    tile_row0 = jnp.arange(MOE_TILES, dtype=jnp.int32) * MOE_TM
    tile_expert = jnp.sum((tile_row0[:, None] >= group_end[None, :]).astype(jnp.int32), axis=1)
    tile_expert = jnp.minimum(tile_expert, N_EXPERTS - 1)
    last_used = tile_expert[jnp.maximum(n_tiles - 1, 0)]
    tile_expert = jnp.where(jnp.arange(MOE_TILES) < n_tiles, tile_expert, last_used).astype(jnp.int32)
    return src_token, dest.astype(jnp.int32), tile_expert, n_tiles.reshape(1)


def _dispatch_kernel(src_ref, nt_ref, h_hbm, o_ref, sem):
    i = pl.program_id(0)
    chunk = o_ref.shape[0]
    base = i * chunk

    def row_copy(r):
        return pltpu.make_async_copy(h_hbm.at[src_ref[base + r]], o_ref.at[r], sem)

    @pl.when(i < nt_ref[0])
    def _():
        def start(r, carry):
            row_copy(r).start()
            return carry

        def wait(r, carry):
            row_copy(r).wait()
            return carry

        lax.fori_loop(0, chunk, start, 0, unroll=DMA_UNROLL)
        lax.fori_loop(0, chunk, wait, 0, unroll=DMA_UNROLL)

    @pl.when(i >= nt_ref[0])
    def _():
        o_ref[...] = jnp.zeros(o_ref.shape, o_ref.dtype)


def _dispatch(h_slabs, src_token, n_tiles):
    chunk = MOE_TM
    slab = h_slabs.shape[1:]
    return pl.pallas_call(
        _dispatch_kernel,
        grid_spec=pltpu.PrefetchScalarGridSpec(
            num_scalar_prefetch=2,
            grid=(MOE_ROWS // chunk,),
            in_specs=[pl.BlockSpec(memory_space=pl.ANY)],
            out_specs=pl.BlockSpec((chunk,) + slab, lambda i, src, nt: (i, 0, 0)),
            scratch_shapes=[pltpu.SemaphoreType.DMA(())],
        ),
        out_shape=jax.ShapeDtypeStruct((MOE_ROWS,) + slab, h_slabs.dtype),
        compiler_params=_params(("arbitrary",), _nbytes((chunk,) + slab, h_slabs.dtype), 0),
        name="moe_dispatch",
    )(src_token, n_tiles, h_slabs)


def _expert_changed(te_ref, i):
    return (i == 0) | (te_ref[i] != te_ref[jnp.maximum(i - 1, 0)])


def _moe_up_kernel(te_ref, nt_ref, hs_ref, wg_ref, wu_ref, a_ref, wg_s, wu_s):
    i = pl.program_id(1)

    @pl.when(_expert_changed(te_ref, i))
    def _():
        wg_s[...] = wg_ref[...].astype(bf16)
        wu_s[...] = wu_ref[...].astype(bf16)

    @pl.when(i < nt_ref[0])
    def _():
        h = hs_ref[...]
        g = _mm(h, wg_s[...])
        u = _mm(h, wu_s[...])
        a_ref[...] = ((g * jax.nn.sigmoid(g)) * u).astype(bf16)

    @pl.when(i >= nt_ref[0])
    def _():
        a_ref[...] = jnp.zeros(a_ref.shape, a_ref.dtype)


def _moe_up(hs, w_gate, w_up, tile_expert, n_tiles, layer):
    tm, tn = MOE_TM, 512
    jl = layer // 2
    return pl.pallas_call(
        _moe_up_kernel,
        grid_spec=pltpu.PrefetchScalarGridSpec(
            num_scalar_prefetch=2,
            grid=(D_FF // tn, MOE_TILES),
            in_specs=[
                pl.BlockSpec((tm, D), lambda j, i, te, nt: (i, 0)),
                pl.BlockSpec((None, None, D, tn), lambda j, i, te, nt: (jl, te[i], 0, j)),
                pl.BlockSpec((None, None, D, tn), lambda j, i, te, nt: (jl, te[i], 0, j)),
            ],
            out_specs=pl.BlockSpec((tm, tn), lambda j, i, te, nt: (i, j)),
            scratch_shapes=[pltpu.VMEM((D, tn), bf16), pltpu.VMEM((D, tn), bf16)],
        ),
        out_shape=jax.ShapeDtypeStruct((MOE_ROWS, D_FF), bf16),
        compiler_params=_params(("arbitrary", "arbitrary"),
                                _nbytes((tm, D), bf16) + 2 * _nbytes((D, tn), f32) + _nbytes((tm, tn), bf16),
                                2 * _nbytes((D, tn), bf16) + 3 * _nbytes((tm, tn), f32)),
        name="moe_up",
    )(tile_expert, n_tiles, hs, w_gate, w_up)


def _moe_down_kernel(te_ref, nt_ref, a_ref, wd_ref, y_ref, wd_s):
    i = pl.program_id(1)

    @pl.when(_expert_changed(te_ref, i))
    def _():
        wd_s[...] = wd_ref[...].astype(bf16)

    @pl.when(i < nt_ref[0])
    def _():
        y_ref[...] = _mm(a_ref[...], wd_s[...])

    @pl.when(i >= nt_ref[0])
    def _():
        y_ref[...] = jnp.zeros(y_ref.shape, y_ref.dtype)


def _moe_down(a, w_down, tile_expert, n_tiles, layer):
    tm, tn = MOE_TM, 512
    jl = layer // 2
    return pl.pallas_call(
        _moe_down_kernel,
        grid_spec=pltpu.PrefetchScalarGridSpec(
            num_scalar_prefetch=2,
            grid=(D // tn, MOE_TILES),
            in_specs=[
                pl.BlockSpec((tm, D_FF), lambda j, i, te, nt: (i, 0)),
                pl.BlockSpec((None, None, D_FF, tn), lambda j, i, te, nt: (jl, te[i], 0, j)),
            ],
            out_specs=pl.BlockSpec((tm, tn), lambda j, i, te, nt: (i, j)),
            scratch_shapes=[pltpu.VMEM((D_FF, tn), bf16)],
        ),
        out_shape=jax.ShapeDtypeStruct((MOE_ROWS, D), f32),
        compiler_params=_params(("arbitrary", "arbitrary"),
                                _nbytes((tm, D_FF), bf16) + _nbytes((D_FF, tn), f32) + _nbytes((tm, tn), f32),
                                _nbytes((D_FF, tn), bf16) + _nbytes((tm, tn), f32)),
        name="moe_down",
    )(tile_expert, n_tiles, a, w_down)


def _combine_kernel(pos_ref, x_ref, gate_ref, wt_ref, y_hbm, o_ref, ybuf, sem, *, tm):
    base = pl.program_id(0) * tm * TOP_K

    def copies(t):
        return [pltpu.make_async_copy(y_hbm.at[pos_ref[base + t * TOP_K + c]], ybuf.at[c, t], sem.at[c])
                for c in range(TOP_K)]

    def start(t, carry):
        for cp in copies(t):
            cp.start()
        return carry

    def wait(t, carry):
        for cp in copies(t):
            cp.wait()
        return carry

    lax.fori_loop(0, tm, start, 0, unroll=DMA_UNROLL)
    lax.fori_loop(0, tm, wait, 0, unroll=DMA_UNROLL)
    w1 = wt_ref[:, 0:1]
    w2 = wt_ref[:, 1:2]
    for s in range(ROW_SLABS):
        sl = slice(s * LANES, (s + 1) * LANES)
        mix = w1 * ybuf[0, :, s, :] + w2 * ybuf[1, :, s, :]
        o_ref[:, sl] = x_ref[:, sl] + gate_ref[:, sl] * mix


def _combine(x, y_slabs, pos, wts, mod, layer):
    tm = 256
    return pl.pallas_call(
        functools.partial(_combine_kernel, tm=tm),
        grid_spec=pltpu.PrefetchScalarGridSpec(
            num_scalar_prefetch=1,
            grid=(T // tm,),
            in_specs=[
                pl.BlockSpec((tm, D), lambda i, pos: (i, 0)),
                _mod_spec(layer, 5, tm),
                pl.BlockSpec((tm, LANES), lambda i, pos: (i, 0)),
                pl.BlockSpec(memory_space=pl.ANY),
            ],
            out_specs=pl.BlockSpec((tm, D), lambda i, pos: (i, 0)),
            scratch_shapes=[pltpu.VMEM((TOP_K, tm, ROW_SLABS, LANES), f32),
                            pltpu.SemaphoreType.DMA((TOP_K,))],
        ),
        out_shape=jax.ShapeDtypeStruct((T, D), f32),
        compiler_params=_params(("arbitrary",),
                                2 * _nbytes((tm, D), f32) + _nbytes((tm, LANES), f32),
                                _nbytes((TOP_K, tm, D), f32) + 4 * _nbytes((tm, LANES), f32)),
        name="moe_combine",
    )(pos, x, mod, wts, y_slabs)


def _moe_layer(x, mod, norm2, w_router, w_gate, w_up, w_down, layer):
    wr = jnp.pad(w_router, ((0, 0), (0, 0), (0, LANES - N_EXPERTS)))
    h, idx, wts = _router(x, mod, norm2, wr, layer)
    src_token, pos, tile_expert, n_tiles = _routing_plan(idx[:, :TOP_K])
    hs = _dispatch(h.reshape(T, ROW_SLABS, LANES), src_token, n_tiles).reshape(MOE_ROWS, D)
    a = _moe_up(hs, w_gate, w_up, tile_expert, n_tiles, layer)
    y = _moe_down(a, w_down, tile_expert, n_tiles, layer)
    return _combine(x, y.reshape(MOE_ROWS, ROW_SLABS, LANES), pos, wts, mod, layer)


def kernel(x_prompt, x_sample, cache_k, cache_v, c, c_ctx, norm1, norm2, w_ada, b_ada, w_conv_in, conv_w,
           w_conv_out, w_qkv, q_norm, k_norm, rpb, w_attn_out, w_ffn_gate, w_ffn_up, w_ffn_down, w_router,
           w_moe_gate, w_moe_up, w_moe_down):
    x = jnp.concatenate([x_prompt.reshape(CTX_ROWS, D), x_sample.reshape(DEC_ROWS, D)], axis=0)
    cond = jnp.concatenate([c_ctx[None, :], c, jnp.zeros((N_GROUPS_PADDED - 1 - N_DEC_SEQ, D), f32)], axis=0)
    mod = _ada_params(cond, w_ada, b_ada)
    cache_k2 = cache_k.reshape(N_DEC_SEQ, -1, PAST_LEN, D)
    cache_v2 = cache_v.reshape(N_DEC_SEQ, -1, PAST_LEN, D)
    w_conv_in, w_conv_out, w_qkv, w_attn_out, w_ffn_gate, w_ffn_up, w_ffn_down = (
        w.astype(bf16) for w in (w_conv_in, w_conv_out, w_qkv, w_attn_out, w_ffn_gate, w_ffn_up, w_ffn_down))

    new_k, new_v = [], []
    for layer in range(DEPTH):
        jl = layer // 2
        if layer % 2 == 0:
            b, u = _conv_in(x, mod, norm1, w_conv_in, layer)
            gated = _conv_gate(b, u, conv_w, layer)
            x = _mm_residual([gated], w_conv_out, x, mod, layer, 2, 1024, 1024, "conv_out")
            a = _ffn_up(x, mod, norm2, w_ffn_gate, w_ffn_up, layer)
            x = _mm_residual([a], w_ffn_down, x, mod, layer, 5, 1024, 512, "ffn_down")
        else:
            qkv = _qkv(x, mod, norm1, w_qkv, q_norm, k_norm, layer)
            o_ctx, k_ctx, v_ctx = _ctx_attention(qkv)
            new_k.append(k_ctx.reshape(N_CTX_SEQ, CTX_LEN, N_HEADS, HEAD_DIM))
            new_v.append(v_ctx.reshape(N_CTX_SEQ, CTX_LEN, N_HEADS, HEAD_DIM))
            o_dec = _latent_attention(qkv, cache_k2, cache_v2, jl, _rel_bias_blocks(rpb[jl]))
            x = _mm_residual([o_ctx, o_dec], w_attn_out, x, mod, layer, 2, 1024, 1024, "attn_out")
            x = _moe_layer(x, mod, norm2, w_router, w_moe_gate, w_moe_up, w_moe_down, layer)

    y_prompt = x[:CTX_ROWS].reshape(N_CTX_SEQ, CTX_LEN, D)
    y_sample = x[CTX_ROWS:].reshape(N_DEC_SEQ, DEC_LEN, D)
    return y_prompt, y_sample, jnp.stack(new_k, axis=1), jnp.stack(new_v, axis=1)
```

```python
import functools

import numpy as np
import jax
import jax.numpy as jnp
from jax import lax
from jax.experimental import pallas as pl
from jax.experimental.pallas import tpu as pltpu

D = 2048
N_CTX_SEQ = 16
CTX_LEN = 256
N_DEC_SEQ = 4
DEC_LEN = 2048
CTX_ROWS = N_CTX_SEQ * CTX_LEN
DEC_ROWS = N_DEC_SEQ * DEC_LEN
T = CTX_ROWS + DEC_ROWS
DEPTH = 4
PAST_LEN = 512
N_HEADS = 16
HEAD_DIM = 128
GRID_W = 64
GRID_ROWS = DEC_LEN // GRID_W
WIN_ROWS = 8
WIN_COLS = 16
D_FF = 5632
N_EXPERTS = 8
TOP_K = 2
EPS = 1e-6
NEG_INF = -1e30
ATTN_SCALE = HEAD_DIM ** -0.5

N_GROUPS_PADDED = 8
LANES = 128
SUBLANES = 8
VMEM_CAP_BYTES = 56 * 2 ** 20
VMEM_HEADROOM = 4 * 2 ** 20

Q_ROWS_PER_BLOCK = 4
Q_BLOCK = Q_ROWS_PER_BLOCK * GRID_W
N_Q_BLOCKS = GRID_ROWS // Q_ROWS_PER_BLOCK
BAND_ROWS = Q_ROWS_PER_BLOCK + WIN_ROWS
BAND = BAND_ROWS * GRID_W
MASKED_BLOCK = 2 * WIN_ROWS - 1
N_BIAS_BLOCKS = MASKED_BLOCK + 1

MOE_TM = 512
DMA_UNROLL = 8
N_ASSIGN = T * TOP_K
MOE_ROWS = N_ASSIGN + N_EXPERTS * MOE_TM
MOE_TILES = MOE_ROWS // MOE_TM
ROW_SLABS = D // LANES

bf16 = jnp.bfloat16
f32 = jnp.float32


def _mm(a, b):
    return jnp.dot(a, b, preferred_element_type=f32)


def _mm_nt(a, b):
    return lax.dot_general(a, b, (((1,), (1,)), ((), ())), preferred_element_type=f32)


def _nbytes(shape, dtype):
    return int(np.prod(shape)) * jnp.dtype(dtype).itemsize


def _params(semantics, pipelined_bytes, resident_bytes):
    need = 2 * pipelined_bytes + resident_bytes + VMEM_HEADROOM
    return pltpu.CompilerParams(dimension_semantics=semantics,
                                vmem_limit_bytes=int(min(max(need, 16 * 2 ** 20), VMEM_CAP_BYTES)))


def _group_of_tile(i, tm):
    n_ctx = CTX_ROWS // tm
    per_seq = DEC_LEN // tm
    return jnp.where(i < n_ctx, 0, 1 + (i - n_ctx) // per_seq)


def _mod_spec(layer, which, tm, width=D):
    def index(*grid_and_prefetch):
        i = grid_and_prefetch[0]
        j = grid_and_prefetch[1] if width != D else 0
        return ((layer * N_GROUPS_PADDED + _group_of_tile(i, tm)) * 6 + which, 0, j)
    return pl.BlockSpec((None, 1, width), index)


def _norm_mod_rows(x, gain, scale1p, shift):
    ms = jnp.mean(x * x, axis=-1, keepdims=True)
    y = x * lax.rsqrt(ms + EPS)
    return (y * gain) * scale1p + shift


def _norm_mod_to_scratch(x_ref, gain_ref, sc_ref, sh_ref, h_scr, rows=256):
    gain = gain_ref[...]
    scale1p = 1.0 + sc_ref[...]
    shift = sh_ref[...]

    def body(c, carry):
        r = pl.multiple_of(c * rows, rows)
        h = _norm_mod_rows(x_ref[pl.ds(r, rows), :], gain, scale1p, shift)
        h_scr[pl.ds(r, rows), :] = h.astype(bf16)
        return carry

    lax.fori_loop(0, x_ref.shape[0] // rows, body, 0)


def _ada_kernel(cond_ref, w_ref, b_ref, o_ref):
    c = cond_ref[...]
    s = (c * jax.nn.sigmoid(c)).astype(bf16)
    o_ref[...] = _mm(s, w_ref[...].astype(bf16)) + b_ref[...]


def _ada_params(cond, w_ada, b_ada):
    tn = 1024
    out = pl.pallas_call(
        _ada_kernel,
        grid=(DEPTH, 6 * D // tn),
        in_specs=[
            pl.BlockSpec((N_GROUPS_PADDED, D), lambda l, j: (0, 0)),
            pl.BlockSpec((None, D, tn), lambda l, j: (l, 0, j)),
            pl.BlockSpec((None, 1, tn), lambda l, j: (l, 0, j)),
        ],
        out_specs=pl.BlockSpec((None, N_GROUPS_PADDED, tn), lambda l, j: (l, 0, j)),
        out_shape=jax.ShapeDtypeStruct((DEPTH, N_GROUPS_PADDED, 6 * D), f32),
        compiler_params=_params(("arbitrary", "arbitrary"),
                                _nbytes((D, tn), f32) + _nbytes((8, tn), f32) * 2,
                                _nbytes((D, tn), bf16) + _nbytes((8, D), f32) * 2),
        name="ada_params",
    )(cond, w_ada, b_ada.reshape(DEPTH, 1, 6 * D))
    return out.reshape(DEPTH * N_GROUPS_PADDED * 6, 1, D)


def _conv_in_kernel(x_ref, gain_ref, sc_ref, sh_ref, wb_ref, wc_ref, wv_ref, b_ref, u_ref, h_scr):
    @pl.when(pl.program_id(1) == 0)
    def _():
        _norm_mod_to_scratch(x_ref, gain_ref, sc_ref, sh_ref, h_scr)

    h = h_scr[...]
    b_ref[...] = _mm(h, wb_ref[...])
    u_ref[...] = _mm(h, wc_ref[...]) * _mm(h, wv_ref[...])


def _conv_in(x, mod, norm, w_conv_in, layer):
    tm, tn = 1024, 256
    jl = layer // 2
    nb = D // tn
    return pl.pallas_call(
        _conv_in_kernel,
        grid=(T // tm, nb),
        in_specs=[
            pl.BlockSpec((tm, D), lambda i, j: (i, 0)),
            pl.BlockSpec((None, 1, D), lambda i, j: (layer, 0, 0)),
            _mod_spec(layer, 1, tm), _mod_spec(layer, 0, tm),
            pl.BlockSpec((None, D, tn), lambda i, j: (jl, 0, j)),
            pl.BlockSpec((None, D, tn), lambda i, j: (jl, 0, nb + j)),
            pl.BlockSpec((None, D, tn), lambda i, j: (jl, 0, 2 * nb + j)),
        ],
        out_specs=[pl.BlockSpec((tm, tn), lambda i, j: (i, j))] * 2,
        out_shape=[jax.ShapeDtypeStruct((T, D), f32)] * 2,
        scratch_shapes=[pltpu.VMEM((tm, D), bf16)],
        compiler_params=_params(("arbitrary", "arbitrary"),
                                _nbytes((tm, D), f32) + 3 * _nbytes((D, tn), bf16) + 2 * _nbytes((tm, tn), f32),
                                _nbytes((tm, D), bf16) + 3 * _nbytes((tm, tn), f32)),
        name="conv_in",
    )(x, norm.reshape(DEPTH, 1, D), mod, mod, w_conv_in, w_conv_in, w_conv_in)


def _ffn_up_kernel(x_ref, gain_ref, sc_ref, sh_ref, wg_ref, wu_ref, a_ref, h_scr):
    @pl.when(pl.program_id(1) == 0)
    def _():
        _norm_mod_to_scratch(x_ref, gain_ref, sc_ref, sh_ref, h_scr)

    h = h_scr[...]
    g = _mm(h, wg_ref[...])
    u = _mm(h, wu_ref[...])
    a_ref[...] = ((g * jax.nn.sigmoid(g)) * u).astype(bf16)


def _ffn_up(x, mod, norm, w_gate, w_up, layer):
    tm, tn = 1024, 512
    jl = layer // 2
    return pl.pallas_call(
        _ffn_up_kernel,
        grid=(T // tm, D_FF // tn),
        in_specs=[
            pl.BlockSpec((tm, D), lambda i, j: (i, 0)),
            pl.BlockSpec((None, 1, D), lambda i, j: (layer, 0, 0)),
            _mod_spec(layer, 4, tm), _mod_spec(layer, 3, tm),
            pl.BlockSpec((None, D, tn), lambda i, j: (jl, 0, j)),
            pl.BlockSpec((None, D, tn), lambda i, j: (jl, 0, j)),
        ],
        out_specs=pl.BlockSpec((tm, tn), lambda i, j: (i, j)),
        out_shape=jax.ShapeDtypeStruct((T, D_FF), bf16),
        scratch_shapes=[pltpu.VMEM((tm, D), bf16)],
        compiler_params=_params(("arbitrary", "arbitrary"),
                                _nbytes((tm, D), f32) + 2 * _nbytes((D, tn), bf16) + _nbytes((tm, tn), bf16),
                                _nbytes((tm, D), bf16) + 3 * _nbytes((tm, tn), f32)),
        name="ffn_up",
    )(x, norm.reshape(DEPTH, 1, D), mod, mod, w_gate, w_up)


def _qkv_kernel(x_ref, gain_ref, sc_ref, sh_ref, w_ref, qg_ref, kg_ref, o_ref, h_scr, *, n_q_blocks):
    j = pl.program_id(1)

    @pl.when(j == 0)
    def _():
        _norm_mod_to_scratch(x_ref, gain_ref, sc_ref, sh_ref, h_scr)

    acc = _mm(h_scr[...], w_ref[...])

    @pl.when(j >= 2 * n_q_blocks)
    def _():
        o_ref[...] = acc

    @pl.when(j < 2 * n_q_blocks)
    def _():
        head_gain = jnp.where(j < n_q_blocks, qg_ref[...], kg_ref[...])
        for hh in range(acc.shape[1] // HEAD_DIM):
            a = acc[:, hh * HEAD_DIM:(hh + 1) * HEAD_DIM]
            ms = jnp.mean(a * a, axis=-1, keepdims=True)
            o_ref[:, hh * HEAD_DIM:(hh + 1) * HEAD_DIM] = (a * lax.rsqrt(ms + EPS)) * head_gain


def _qkv(x, mod, norm, w_qkv, q_norm, k_norm, layer):
    tm, tn = 1024, 1024
    jl = layer // 2
    return pl.pallas_call(
        functools.partial(_qkv_kernel, n_q_blocks=D // tn),
        grid=(T // tm, 3 * D // tn),
        in_specs=[
            pl.BlockSpec((tm, D), lambda i, j: (i, 0)),
            pl.BlockSpec((None, 1, D), lambda i, j: (layer, 0, 0)),
            _mod_spec(layer, 1, tm), _mod_spec(layer, 0, tm),
            pl.BlockSpec((None, D, tn), lambda i, j: (jl, 0, j)),
            pl.BlockSpec((None, 1, HEAD_DIM), lambda i, j: (jl, 0, 0)),
            pl.BlockSpec((None, 1, HEAD_DIM), lambda i, j: (jl, 0, 0)),
        ],
        out_specs=pl.BlockSpec((tm, tn), lambda i, j: (i, j)),
        out_shape=jax.ShapeDtypeStruct((T, 3 * D), f32),
        scratch_shapes=[pltpu.VMEM((tm, D), bf16)],
        compiler_params=_params(("arbitrary", "arbitrary"),
                                _nbytes((tm, D), f32) + _nbytes((D, tn), bf16) + _nbytes((tm, tn), f32),
                                _nbytes((tm, D), bf16) + 2 * _nbytes((tm, tn), f32)),
        name="qkv",
    )(x, norm.reshape(DEPTH, 1, D), mod, mod, w_qkv,
      q_norm.reshape(-1, 1, HEAD_DIM), k_norm.reshape(-1, 1, HEAD_DIM))


def _mm_res_kernel(*refs, n_ctx_tiles):
    *a_refs, w_ref, x_ref, gate_ref, o_ref = refs

    def update(a_ref):
        o_ref[...] = x_ref[...] + gate_ref[...] * _mm(a_ref[...], w_ref[...])

    if len(a_refs) == 1:
        update(a_refs[0])
    else:
        i = pl.program_id(0)

        @pl.when(i < n_ctx_tiles)
        def _():
            update(a_refs[0])

        @pl.when(i >= n_ctx_tiles)
        def _():
            update(a_refs[1])


def _mm_residual(a_parts, w, x, mod, layer, which, tm, tn, name):
    k = a_parts[0].shape[1]
    jl = layer // 2
    n_ctx_tiles = CTX_ROWS // tm
    if len(a_parts) == 1:
        a_specs = [pl.BlockSpec((tm, k), lambda i, j: (i, 0))]
    else:
        a_specs = [pl.BlockSpec((tm, k), lambda i, j: (jnp.minimum(i, n_ctx_tiles - 1), 0)),
                   pl.BlockSpec((tm, k), lambda i, j: (jnp.maximum(i - n_ctx_tiles, 0), 0))]
    return pl.pallas_call(
        functools.partial(_mm_res_kernel, n_ctx_tiles=n_ctx_tiles),
        grid=(T // tm, D // tn),
        in_specs=a_specs + [
            pl.BlockSpec((None, k, tn), lambda i, j: (jl, 0, j)),
            pl.BlockSpec((tm, tn), lambda i, j: (i, j)),
            _mod_spec(layer, which, tm, tn),
        ],
        out_specs=pl.BlockSpec((tm, tn), lambda i, j: (i, j)),
        out_shape=jax.ShapeDtypeStruct((T, D), f32),
        compiler_params=_params(("arbitrary", "arbitrary"),
                                len(a_parts) * _nbytes((tm, k), bf16) + _nbytes((k, tn), bf16)
                                + 2 * _nbytes((tm, tn), f32),
                                2 * _nbytes((tm, tn), f32)),
        name=name,
    )(*a_parts, w, x, mod)


def _conv_gate_kernel(b_ref, u_ref, up_ref, un_ref, cw_ref, o_ref, *, tm, cols):
    row0 = pl.program_id(0) * tm
    seq_len = jnp.where(row0 < CTX_ROWS, CTX_LEN, DEC_LEN)
    local = lax.broadcasted_iota(jnp.int32, (tm, cols), 0)
    pos = (row0 + local) & (seq_len - 1)
    is_first = pos == 0
    is_last = pos == seq_len - 1
    for c in range(D // cols):
        sl = slice(c * cols, (c + 1) * cols)
        u = u_ref[:, sl]
        prev_row = up_ref[SUBLANES - 1:SUBLANES, sl]
        next_row = un_ref[0:1, sl]
        u_prev = jnp.where(local == 0, prev_row, pltpu.roll(u, 1, axis=0))
        u_next = jnp.where(local == tm - 1, next_row, pltpu.roll(u, tm - 1, axis=0))
        u_prev = jnp.where(is_first, 0.0, u_prev)
        u_next = jnp.where(is_last, 0.0, u_next)
        conv = u_prev * cw_ref[0:1, sl] + u * cw_ref[1:2, sl] + u_next * cw_ref[2:3, sl]
        o_ref[:, sl] = (b_ref[:, sl] * conv).astype(bf16)


def _conv_gate(b, u, conv_w, layer):
    tm, cols = 512, 512
    jl = layer // 2
    blocks_per_tile = tm // SUBLANES
    return pl.pallas_call(
        functools.partial(_conv_gate_kernel, tm=tm, cols=cols),
        grid=(T // tm,),
        in_specs=[
            pl.BlockSpec((tm, D), lambda i: (i, 0)),
            pl.BlockSpec((tm, D), lambda i: (i, 0)),
            pl.BlockSpec((SUBLANES, D), lambda i: (jnp.maximum(i * blocks_per_tile - 1, 0), 0)),
            pl.BlockSpec((SUBLANES, D), lambda i: (jnp.minimum((i + 1) * blocks_per_tile, T // SUBLANES - 1), 0)),
            pl.BlockSpec((None, 3, D), lambda i: (jl, 0, 0)),
        ],
        out_specs=pl.BlockSpec((tm, D), lambda i: (i, 0)),
        out_shape=jax.ShapeDtypeStruct((T, D), bf16),
        compiler_params=_params(("arbitrary",),
                                2 * _nbytes((tm, D), f32) + _nbytes((tm, D), bf16) + 2 * _nbytes((SUBLANES, D), f32),
                                8 * _nbytes((tm, cols), f32)),
        name="conv_gate",
    )(b, u, u, u, conv_w)


def _softmax_pv(parts):
    m = parts[0][0].max(axis=-1, keepdims=True)
    for s, _ in parts[1:]:
        m = jnp.maximum(m, s.max(axis=-1, keepdims=True))
    den = None
    out = None
    for s, v in parts:
        p = jnp.exp(s - m)
        d = p.sum(axis=-1, keepdims=True)
        o = _mm(p.astype(bf16), v)
        den = d if den is None else den + d
        out = o if out is None else out + o
    return out * (1.0 / den)


def _ctx_attn_kernel(q_ref, k_ref, v_ref, o_ref, k_out_ref, v_out_ref):
    k_out_ref[...] = k_ref[...]
    v_out_ref[...] = v_ref[...]
    for hh in range(q_ref.shape[1] // HEAD_DIM):
        sl = slice(hh * HEAD_DIM, (hh + 1) * HEAD_DIM)
        q = q_ref[:, sl].astype(bf16)
        k = k_ref[:, sl].astype(bf16)
        v = v_ref[:, sl].astype(bf16)
        s = _mm_nt(q, k) * ATTN_SCALE
        o_ref[:, sl] = _softmax_pv([(s, v)]).astype(bf16)


def _ctx_attention(qkv):
    tn = 512
    nb = D // tn
    return pl.pallas_call(
        _ctx_attn_kernel,
        grid=(N_CTX_SEQ, nb),
        in_specs=[
            pl.BlockSpec((CTX_LEN, tn), lambda b, g: (b, g)),
            pl.BlockSpec((CTX_LEN, tn), lambda b, g: (b, nb + g)),
            pl.BlockSpec((CTX_LEN, tn), lambda b, g: (b, 2 * nb + g)),
        ],
        out_specs=[pl.BlockSpec((CTX_LEN, tn), lambda b, g: (b, g))] * 3,
        out_shape=[jax.ShapeDtypeStruct((CTX_ROWS, D), bf16),
                   jax.ShapeDtypeStruct((CTX_ROWS, D), f32),
                   jax.ShapeDtypeStruct((CTX_ROWS, D), f32)],
        compiler_params=_params(("arbitrary", "arbitrary"),
                                5 * _nbytes((CTX_LEN, tn), f32) + _nbytes((CTX_LEN, tn), bf16),
                                8 * _nbytes((CTX_LEN, CTX_LEN), f32)),
        name="ctx_attention",
    )(qkv, qkv, qkv)


def _band_start_row(m):
    return jnp.clip(Q_ROWS_PER_BLOCK * m - WIN_ROWS // 2, 0, GRID_ROWS - BAND_ROWS)


def _band_block_ids(m):
    band0 = _band_start_row(m)
    ids = []
    for rho in range(Q_ROWS_PER_BLOCK):
        r = Q_ROWS_PER_BLOCK * m + rho
        win0 = jnp.clip(r - WIN_ROWS // 2, 0, GRID_ROWS - WIN_ROWS)
        row_ids = []
        for kap in range(BAND_ROWS):
            kr = band0 + kap
            inside = (kr >= win0) & (kr < win0 + WIN_ROWS)
            row_ids.append(jnp.where(inside, kr - r + WIN_ROWS - 1, MASKED_BLOCK))
        ids.append(row_ids)
    return ids


def _band_bias(bias_ref, hh, ids):
    left_half = lax.broadcasted_iota(jnp.int32, (GRID_W, 2 * GRID_W), 1) < GRID_W
    rows = []
    for rho in range(Q_ROWS_PER_BLOCK):
        pairs = [jnp.where(left_half, bias_ref[hh, ids[rho][kap]], bias_ref[hh, ids[rho][kap + 1]])
                 for kap in range(0, BAND_ROWS, 2)]
        rows.append(jnp.concatenate(pairs, axis=1))
    return jnp.concatenate(rows, axis=0)


def _loc_attn_kernel(q_ref, k_ref, v_ref, ck_ref, cv_ref, bias_ref, o_ref):
    start = pl.multiple_of(_band_start_row(pl.program_id(2)) * GRID_W, GRID_W)
    ids = _band_block_ids(pl.program_id(2))
    for hh in range(q_ref.shape[1] // HEAD_DIM):
        sl = slice(hh * HEAD_DIM, (hh + 1) * HEAD_DIM)
        q = q_ref[:, sl].astype(bf16)
        k_band = k_ref[pl.ds(start, BAND), sl].astype(bf16)
        v_band = v_ref[pl.ds(start, BAND), sl].astype(bf16)
        k_ctx = ck_ref[:, sl].astype(bf16)
        v_ctx = cv_ref[:, sl].astype(bf16)
        s_loc = _mm_nt(q, k_band) * ATTN_SCALE + _band_bias(bias_ref, hh, ids)
        s_ctx = _mm_nt(q, k_ctx) * ATTN_SCALE
        o_ref[:, sl] = _softmax_pv([(s_loc, v_band), (s_ctx, v_ctx)]).astype(bf16)


def _rel_bias_blocks(rpb_l):
    cols = np.arange(GRID_W)
    col_start = np.clip(cols - WIN_COLS // 2, 0, GRID_W - WIN_COLS)
    col_ok = (cols[None, :] >= col_start[:, None]) & (cols[None, :] < col_start[:, None] + WIN_COLS)
    pad = GRID_W - WIN_COLS
    padded = jnp.pad(rpb_l.astype(f32), ((0, 0), (0, 0), (pad, pad)))
    skewed = jnp.tile(padded, (1, 1, GRID_W + 1))[:, :, :GRID_W * 2 * GRID_W]
    skewed = skewed.reshape(N_HEADS, MASKED_BLOCK, GRID_W, 2 * GRID_W)[..., :GRID_W]
    toeplitz = jnp.flip(skewed, axis=2)
    toeplitz = jnp.where(col_ok[None, None], toeplitz, NEG_INF)
    masked = jnp.full((N_HEADS, 1, GRID_W, GRID_W), NEG_INF, f32)
    blocks = jnp.concatenate([toeplitz, masked], axis=1)
    return jnp.concatenate([blocks, blocks], axis=3)


def _latent_attention(qkv, cache_k, cache_v, layer_slot, bias_blocks):
    tn = 512
    nb = D // tn
    hb = tn // HEAD_DIM
    q_block0 = CTX_ROWS // Q_BLOCK
    seq_block0 = CTX_ROWS // DEC_LEN
    return pl.pallas_call(
        _loc_attn_kernel,
        grid=(N_DEC_SEQ, nb, N_Q_BLOCKS),
        in_specs=[
            pl.BlockSpec((Q_BLOCK, tn), lambda b, g, m: (q_block0 + b * N_Q_BLOCKS + m, g)),
            pl.BlockSpec((DEC_LEN, tn), lambda b, g, m: (seq_block0 + b, nb + g)),
            pl.BlockSpec((DEC_LEN, tn), lambda b, g, m: (seq_block0 + b, 2 * nb + g)),
            pl.BlockSpec((None, None, PAST_LEN, tn), lambda b, g, m: (b, layer_slot, 0, g)),
            pl.BlockSpec((None, None, PAST_LEN, tn), lambda b, g, m: (b, layer_slot, 0, g)),
            pl.BlockSpec((hb, N_BIAS_BLOCKS, GRID_W, 2 * GRID_W), lambda b, g, m: (g, 0, 0, 0)),
        ],
        out_specs=pl.BlockSpec((Q_BLOCK, tn), lambda b, g, m: (b * N_Q_BLOCKS + m, g)),
        out_shape=jax.ShapeDtypeStruct((DEC_ROWS, D), bf16),
        compiler_params=_params(("arbitrary", "arbitrary", "arbitrary"),
                                _nbytes((Q_BLOCK, tn), f32) + 2 * _nbytes((DEC_LEN, tn), f32)
                                + 2 * _nbytes((PAST_LEN, tn), f32)
                                + _nbytes((hb, N_BIAS_BLOCKS, GRID_W, 2 * GRID_W), f32)
                                + _nbytes((Q_BLOCK, tn), bf16),
                                10 * _nbytes((Q_BLOCK, BAND), f32)),
        name="latent_attention",
    )(qkv, qkv, qkv, cache_k, cache_v, bias_blocks)


HI_HALF_MASK = 0xFFFF0000


def _pack_bf16_halves(h):
    bits = lax.bitcast_convert_type(h.astype(f32), jnp.uint32)
    half = h.shape[1] // 2
    return (bits[:, :half] >> 16) | (bits[:, half:] & jnp.uint32(HI_HALF_MASK))


def _unpack_bf16_halves(words):
    lo = lax.bitcast_convert_type(words << 16, f32).astype(bf16)
    hi = lax.bitcast_convert_type(words & jnp.uint32(HI_HALF_MASK), f32).astype(bf16)
    return lo, hi


def _router_kernel(x_ref, gain_ref, sc_ref, sh_ref, wr_ref, hp_ref, idx_ref, wt_ref, rank_ref, cnt_ref, seen):
    tm = x_ref.shape[0]

    @pl.when(pl.program_id(0) == 0)
    def _():
        seen[...] = jnp.zeros(seen.shape, seen.dtype)

    h = _norm_mod_rows(x_ref[...], gain_ref[...], 1.0 + sc_ref[...], sh_ref[...]).astype(bf16)
    hp_ref[...] = _pack_bf16_halves(h)
    logits = _mm(h, wr_ref[...].astype(bf16))
    lane = lax.broadcasted_iota(jnp.int32, (tm, LANES), 1)
    l1 = jnp.where(lane < N_EXPERTS, logits, -jnp.inf)
    m1 = l1.max(axis=-1, keepdims=True)
    i1 = jnp.where(l1 == m1, lane, LANES).min(axis=-1, keepdims=True)
    l2 = jnp.where(lane == i1, -jnp.inf, l1)
    m2 = l2.max(axis=-1, keepdims=True)
    i2 = jnp.where(l2 == m2, lane, LANES).min(axis=-1, keepdims=True)
    e = jnp.exp(m2 - m1)
    w1 = 1.0 / (1.0 + e)
    w2 = e / (1.0 + e)
    idx_ref[...] = jnp.where(lane == 0, i1, jnp.where(lane == 1, i2, 0))
    wt_ref[...] = jnp.where(lane == 0, w1, jnp.where(lane == 1, w2, 0.0))

    first = lane == i1
    second = lane == i2
    chosen = (first | second).astype(bf16)
    earlier = (lax.broadcasted_iota(jnp.int32, (tm, tm), 1) < lax.broadcasted_iota(jnp.int32, (tm, tm), 0))
    before = _mm(earlier.astype(bf16), chosen) + seen[...]
    r1 = jnp.where(first, before, 0.0).sum(axis=-1, keepdims=True)
    r2 = jnp.where(second, before, 0.0).sum(axis=-1, keepdims=True)
    rank_ref[...] = jnp.where(lane == 0, r1, jnp.where(lane == 1, r2, 0.0)).astype(jnp.int32)
    seen[...] += chosen.astype(f32).sum(axis=0, keepdims=True)
    cnt_ref[...] = seen[...].astype(jnp.int32)


def _router(x, mod, norm, w_router_padded, layer):
    tm = 512
    jl = layer // 2
    lane_block = pl.BlockSpec((tm, LANES), lambda i: (i, 0))
    return pl.pallas_call(
        _router_kernel,
        grid=(T // tm,),
        in_specs=[
            pl.BlockSpec((tm, D), lambda i: (i, 0)),
            pl.BlockSpec((None, 1, D), lambda i: (layer, 0, 0)),
            _mod_spec(layer, 4, tm), _mod_spec(layer, 3, tm),
            pl.BlockSpec((None, D, LANES), lambda i: (jl, 0, 0)),
        ],
        out_specs=[pl.BlockSpec((tm, D // 2), lambda i: (i, 0)), lane_block, lane_block, lane_block,
                   pl.BlockSpec((1, LANES), lambda i: (0, 0))],
        out_shape=[jax.ShapeDtypeStruct((T, D // 2), jnp.uint32),
                   jax.ShapeDtypeStruct((T, LANES), jnp.int32),
                   jax.ShapeDtypeStruct((T, LANES), f32),
                   jax.ShapeDtypeStruct((T, LANES), jnp.int32),
                   jax.ShapeDtypeStruct((1, LANES), jnp.int32)],
        scratch_shapes=[pltpu.VMEM((1, LANES), f32)],
        compiler_params=_params(("arbitrary",),
                                _nbytes((tm, D), f32) + _nbytes((tm, D), bf16) + 3 * _nbytes((tm, LANES), f32),
                                _nbytes((D, LANES), f32) + 4 * _nbytes((tm, D), f32) + 2 * _nbytes((tm, tm), f32)),
        name="router",
    )(x, norm.reshape(DEPTH, 1, D), mod, mod, w_router_padded)


def _routing_plan(expert_ids, ranks, counts):
    e_flat = expert_ids.reshape(-1)
    padded = ((counts + MOE_TM - 1) // MOE_TM) * MOE_TM
    group_end = jnp.cumsum(padded)
    group_start = group_end - padded
    onehot = e_flat[:, None] == jnp.arange(N_EXPERTS, dtype=jnp.int32)[None, :]
    dest = ranks.reshape(-1) + jnp.sum(jnp.where(onehot, group_start[None, :], 0), axis=1)
    src_token = jnp.zeros((MOE_ROWS,), jnp.int32).at[dest].set(
        jnp.arange(N_ASSIGN, dtype=jnp.int32) // TOP_K, unique_indices=True)
    n_tiles = (group_end[-1] // MOE_TM).astype(jnp.int32)
    tile_row0 = jnp.arange(MOE_TILES, dtype=jnp.int32) * MOE_TM
    tile_expert = jnp.sum((tile_row0[:, None] >= group_end[None, :]).astype(jnp.int32), axis=1)
    tile_expert = jnp.minimum(tile_expert, N_EXPERTS - 1)
    last_used = tile_expert[jnp.maximum(n_tiles - 1, 0)]
    tile_expert = jnp.where(jnp.arange(MOE_TILES) < n_tiles, tile_expert, last_used).astype(jnp.int32)
    return src_token, dest.astype(jnp.int32), tile_expert, n_tiles.reshape(1)


def _dispatch_kernel(src_ref, nt_ref, hp_hbm, o_ref, buf, sem):
    i = pl.program_id(0)
    chunk = o_ref.shape[0]
    base = i * chunk

    def row_copy(r):
        return pltpu.make_async_copy(hp_hbm.at[pl.ds(src_ref[base + r], 1), :], buf.at[pl.ds(r, 1), :], sem)

    @pl.when(i < nt_ref[0])
    def _():
        def start(r, carry):
            row_copy(r).start()
            return carry

        def wait(r, carry):
            row_copy(r).wait()
            return carry

        lax.fori_loop(0, chunk, start, 0, unroll=DMA_UNROLL)
        lax.fori_loop(0, chunk, wait, 0, unroll=DMA_UNROLL)
        lo, hi = _unpack_bf16_halves(buf[...])
        o_ref[:, :D // 2] = lo
        o_ref[:, D // 2:] = hi

    @pl.when(i >= nt_ref[0])
    def _():
        o_ref[...] = jnp.zeros(o_ref.shape, o_ref.dtype)


def _dispatch(h_packed, src_token, n_tiles):
    chunk = MOE_TM
    return pl.pallas_call(
        _dispatch_kernel,
        grid_spec=pltpu.PrefetchScalarGridSpec(
            num_scalar_prefetch=2,
            grid=(MOE_ROWS // chunk,),
            in_specs=[pl.BlockSpec(memory_space=pl.ANY)],
            out_specs=pl.BlockSpec((chunk, D), lambda i, src, nt: (i, 0)),
            scratch_shapes=[pltpu.VMEM((chunk, D // 2), jnp.uint32), pltpu.SemaphoreType.DMA(())],
        ),
        out_shape=jax.ShapeDtypeStruct((MOE_ROWS, D), bf16),
        compiler_params=_params(("arbitrary",), _nbytes((chunk, D), bf16), 4 * _nbytes((chunk, D // 2), f32)),
        name="moe_dispatch",
    )(src_token, n_tiles, h_packed)


def _expert_changed(te_ref, i):
    return (i == 0) | (te_ref[i] != te_ref[jnp.maximum(i - 1, 0)])


def _moe_up_kernel(te_ref, nt_ref, hs_ref, wg_ref, wu_ref, a_ref, wg_s, wu_s):
    i = pl.program_id(1)

    @pl.when(_expert_changed(te_ref, i))
    def _():
        wg_s[...] = wg_ref[...].astype(bf16)
        wu_s[...] = wu_ref[...].astype(bf16)

    @pl.when(i < nt_ref[0])
    def _():
        h = hs_ref[...]
        g = _mm(h, wg_s[...])
        u = _mm(h, wu_s[...])
        a_ref[...] = ((g * jax.nn.sigmoid(g)) * u).astype(bf16)

    @pl.when(i >= nt_ref[0])
    def _():
        a_ref[...] = jnp.zeros(a_ref.shape, a_ref.dtype)


def _moe_up(hs, w_gate, w_up, tile_expert, n_tiles, layer):
    tm, tn = MOE_TM, 512
    jl = layer // 2
    return pl.pallas_call(
        _moe_up_kernel,
        grid_spec=pltpu.PrefetchScalarGridSpec(
            num_scalar_prefetch=2,
            grid=(D_FF // tn, MOE_TILES),
            in_specs=[
                pl.BlockSpec((tm, D), lambda j, i, te, nt: (i, 0)),
                pl.BlockSpec((None, None, D, tn), lambda j, i, te, nt: (jl, te[i], 0, j)),
                pl.BlockSpec((None, None, D, tn), lambda j, i, te, nt: (jl, te[i], 0, j)),
            ],
            out_specs=pl.BlockSpec((tm, tn), lambda j, i, te, nt: (i, j)),
            scratch_shapes=[pltpu.VMEM((D, tn), bf16), pltpu.VMEM((D, tn), bf16)],
        ),
        out_shape=jax.ShapeDtypeStruct((MOE_ROWS, D_FF), bf16),
        compiler_params=_params(("arbitrary", "arbitrary"),
                                _nbytes((tm, D), bf16) + 2 * _nbytes((D, tn), f32) + _nbytes((tm, tn), bf16),
                                2 * _nbytes((D, tn), bf16) + 3 * _nbytes((tm, tn), f32)),
        name="moe_up",
    )(tile_expert, n_tiles, hs, w_gate, w_up)


def _moe_down_kernel(te_ref, nt_ref, a_ref, wd_ref, y_ref, wd_s):
    i = pl.program_id(1)

    @pl.when(_expert_changed(te_ref, i))
    def _():
        wd_s[...] = wd_ref[...].astype(bf16)

    @pl.when(i < nt_ref[0])
    def _():
        y_ref[...] = _mm(a_ref[...], wd_s[...])

    @pl.when(i >= nt_ref[0])
    def _():
        y_ref[...] = jnp.zeros(y_ref.shape, y_ref.dtype)


def _moe_down(a, w_down, tile_expert, n_tiles, layer):
    tm, tn = MOE_TM, 512
    jl = layer // 2
    return pl.pallas_call(
        _moe_down_kernel,
        grid_spec=pltpu.PrefetchScalarGridSpec(
            num_scalar_prefetch=2,
            grid=(D // tn, MOE_TILES),
            in_specs=[
                pl.BlockSpec((tm, D_FF), lambda j, i, te, nt: (i, 0)),
                pl.BlockSpec((None, None, D_FF, tn), lambda j, i, te, nt: (jl, te[i], 0, j)),
            ],
            out_specs=pl.BlockSpec((tm, tn), lambda j, i, te, nt: (i, j)),
            scratch_shapes=[pltpu.VMEM((D_FF, tn), bf16)],
        ),
        out_shape=jax.ShapeDtypeStruct((MOE_ROWS, D), f32),
        compiler_params=_params(("arbitrary", "arbitrary"),
                                _nbytes((tm, D_FF), bf16) + _nbytes((D_FF, tn), f32) + _nbytes((tm, tn), f32),
                                _nbytes((D_FF, tn), bf16) + _nbytes((tm, tn), f32)),
        name="moe_down",
    )(tile_expert, n_tiles, a, w_down)


def _combine_kernel(pos_ref, x_ref, gate_ref, wt_ref, y_hbm, o_ref, ybuf, sem, *, tm):
    base = pl.program_id(0) * tm * TOP_K

    def copies(t):
        return [pltpu.make_async_copy(y_hbm.at[pl.ds(pos_ref[base + t * TOP_K + c], 1), :],
                                      ybuf.at[c, pl.ds(t, 1), :], sem.at[c])
                for c in range(TOP_K)]

    def start(t, carry):
        for cp in copies(t):
            cp.start()
        return carry

    def wait(t, carry):
        for cp in copies(t):
            cp.wait()
        return carry

    lax.fori_loop(0, tm, start, 0, unroll=DMA_UNROLL)
    lax.fori_loop(0, tm, wait, 0, unroll=DMA_UNROLL)
    w1 = jnp.broadcast_to(wt_ref[:, 0:1], (tm, LANES))
    w2 = jnp.broadcast_to(wt_ref[:, 1:2], (tm, LANES))
    for s in range(ROW_SLABS):
        sl = slice(s * LANES, (s + 1) * LANES)
        mix = w1 * ybuf[0, :, sl] + w2 * ybuf[1, :, sl]
        o_ref[:, sl] = x_ref[:, sl] + gate_ref[:, sl] * mix


def _combine(x, y_slabs, pos, wts, mod, layer):
    tm = 256
    return pl.pallas_call(
        functools.partial(_combine_kernel, tm=tm),
        grid_spec=pltpu.PrefetchScalarGridSpec(
            num_scalar_prefetch=1,
            grid=(T // tm,),
            in_specs=[
                pl.BlockSpec((tm, D), lambda i, pos: (i, 0)),
                _mod_spec(layer, 5, tm),
                pl.BlockSpec((tm, LANES), lambda i, pos: (i, 0)),
                pl.BlockSpec(memory_space=pl.ANY),
            ],
            out_specs=pl.BlockSpec((tm, D), lambda i, pos: (i, 0)),
            scratch_shapes=[pltpu.VMEM((TOP_K, tm, D), f32),
                            pltpu.SemaphoreType.DMA((TOP_K,))],
        ),
        out_shape=jax.ShapeDtypeStruct((T, D), f32),
        compiler_params=_params(("arbitrary",),
                                2 * _nbytes((tm, D), f32) + _nbytes((tm, LANES), f32),
                                _nbytes((TOP_K, tm, D), f32) + 4 * _nbytes((tm, LANES), f32)),
        name="moe_combine",
    )(pos, x, mod, wts, y_slabs)


def _moe_layer(x, mod, norm2, w_router, w_gate, w_up, w_down, layer):
    wr = jnp.pad(w_router, ((0, 0), (0, 0), (0, LANES - N_EXPERTS)))
    h_packed, idx, wts, ranks, counts = _router(x, mod, norm2, wr, layer)
    src_token, pos, tile_expert, n_tiles = _routing_plan(idx[:, :TOP_K], ranks[:, :TOP_K], counts[0, :N_EXPERTS])
    hs = _dispatch(h_packed, src_token, n_tiles)
    a = _moe_up(hs, w_gate, w_up, tile_expert, n_tiles, layer)
    y = _moe_down(a, w_down, tile_expert, n_tiles, layer)
    return _combine(x, y, pos, wts, mod, layer)


def kernel(x_prompt, x_sample, cache_k, cache_v, c, c_ctx, norm1, norm2, w_ada, b_ada, w_conv_in, conv_w,
           w_conv_out, w_qkv, q_norm, k_norm, rpb, w_attn_out, w_ffn_gate, w_ffn_up, w_ffn_down, w_router,
           w_moe_gate, w_moe_up, w_moe_down):
    x = jnp.concatenate([x_prompt.reshape(CTX_ROWS, D), x_sample.reshape(DEC_ROWS, D)], axis=0)
    cond = jnp.concatenate([c_ctx[None, :], c, jnp.zeros((N_GROUPS_PADDED - 1 - N_DEC_SEQ, D), f32)], axis=0)
    mod = _ada_params(cond, w_ada, b_ada)
    cache_k2 = cache_k.reshape(N_DEC_SEQ, -1, PAST_LEN, D)
    cache_v2 = cache_v.reshape(N_DEC_SEQ, -1, PAST_LEN, D)
    w_conv_in, w_conv_out, w_qkv, w_attn_out, w_ffn_gate, w_ffn_up, w_ffn_down = (
        w.astype(bf16) for w in (w_conv_in, w_conv_out, w_qkv, w_attn_out, w_ffn_gate, w_ffn_up, w_ffn_down))

    new_k, new_v = [], []
    for layer in range(DEPTH):
        jl = layer // 2
        if layer % 2 == 0:
            b, u = _conv_in(x, mod, norm1, w_conv_in, layer)
            gated = _conv_gate(b, u, conv_w, layer)
            x = _mm_residual([gated], w_conv_out, x, mod, layer, 2, 1024, 1024, "conv_out")
            a = _ffn_up(x, mod, norm2, w_ffn_gate, w_ffn_up, layer)
            x = _mm_residual([a], w_ffn_down, x, mod, layer, 5, 1024, 512, "ffn_down")
        else:
            qkv = _qkv(x, mod, norm1, w_qkv, q_norm, k_norm, layer)
            o_ctx, k_ctx, v_ctx = _ctx_attention(qkv)
            new_k.append(k_ctx.reshape(N_CTX_SEQ, CTX_LEN, N_HEADS, HEAD_DIM))
            new_v.append(v_ctx.reshape(N_CTX_SEQ, CTX_LEN, N_HEADS, HEAD_DIM))
            o_dec = _latent_attention(qkv, cache_k2, cache_v2, jl, _rel_bias_blocks(rpb[jl]))
            x = _mm_residual([o_ctx, o_dec], w_attn_out, x, mod, layer, 2, 1024, 1024, "attn_out")
            x = _moe_layer(x, mod, norm2, w_router, w_moe_gate, w_moe_up, w_moe_down, layer)

    y_prompt = x[:CTX_ROWS].reshape(N_CTX_SEQ, CTX_LEN, D)
    y_sample = x[CTX_ROWS:].reshape(N_DEC_SEQ, DEC_LEN, D)
    return y_prompt, y_sample, jnp.stack(new_k, axis=1), jnp.stack(new_v, axis=1)
```

```python
import functools

import numpy as np
import jax
import jax.numpy as jnp
from jax import lax
from jax.experimental import pallas as pl
from jax.experimental.pallas import tpu as pltpu

D = 2048
N_CTX_SEQ = 16
CTX_LEN = 256
N_DEC_SEQ = 4
DEC_LEN = 2048
CTX_ROWS = N_CTX_SEQ * CTX_LEN
DEC_ROWS = N_DEC_SEQ * DEC_LEN
T = CTX_ROWS + DEC_ROWS
DEPTH = 4
PAST_LEN = 512
N_HEADS = 16
HEAD_DIM = 128
GRID_W = 64
GRID_ROWS = DEC_LEN // GRID_W
WIN_ROWS = 8
WIN_COLS = 16
D_FF = 5632
N_EXPERTS = 8
TOP_K = 2
EPS = 1e-6
NEG_INF = -1e30
ATTN_SCALE = HEAD_DIM ** -0.5
LOG2_E = 1.4426950408889634
SCORE_SCALE_LOG2 = ATTN_SCALE * LOG2_E

N_GROUPS_PADDED = 8
LANES = 128
SUBLANES = 8
VMEM_CAP_BYTES = 56 * 2 ** 20
VMEM_HEADROOM = 4 * 2 ** 20

Q_ROWS_PER_BLOCK = 4
Q_BLOCK = Q_ROWS_PER_BLOCK * GRID_W
N_Q_BLOCKS = GRID_ROWS // Q_ROWS_PER_BLOCK
BAND_ROWS = Q_ROWS_PER_BLOCK + WIN_ROWS
BAND = BAND_ROWS * GRID_W
MASKED_BLOCK = 2 * WIN_ROWS - 1
N_BIAS_BLOCKS = MASKED_BLOCK + 1

MOE_TM = 512
DMA_UNROLL = 8
N_ASSIGN = T * TOP_K
MOE_ROWS = N_ASSIGN + N_EXPERTS * MOE_TM
MOE_TILES = MOE_ROWS // MOE_TM
ROW_SLABS = D // LANES

bf16 = jnp.bfloat16
f32 = jnp.float32


def _mm(a, b):
    return jnp.dot(a, b, preferred_element_type=f32)


def _mm_nt(a, b):
    return lax.dot_general(a, b, (((1,), (1,)), ((), ())), preferred_element_type=f32)


def _nbytes(shape, dtype):
    return int(np.prod(shape)) * jnp.dtype(dtype).itemsize


def _params(semantics, pipelined_bytes, resident_bytes):
    need = 2 * pipelined_bytes + resident_bytes + VMEM_HEADROOM
    return pltpu.CompilerParams(dimension_semantics=semantics,
                                vmem_limit_bytes=int(min(max(need, 16 * 2 ** 20), VMEM_CAP_BYTES)))


def _group_of_tile(i, tm):
    n_ctx = CTX_ROWS // tm
    per_seq = DEC_LEN // tm
    return jnp.where(i < n_ctx, 0, 1 + (i - n_ctx) // per_seq)


def _mod_spec(layer, which, tm, width=D, tile0=0):
    def index(*grid_and_prefetch):
        i = tile0 + grid_and_prefetch[0]
        j = grid_and_prefetch[1] if width != D else 0
        return ((layer * N_GROUPS_PADDED + _group_of_tile(i, tm)) * 6 + which, 0, j)
    return pl.BlockSpec((None, 1, width), index)


def _norm_mod_rows(x, gain, scale1p, shift):
    ms = jnp.mean(x * x, axis=-1, keepdims=True)
    y = x * lax.rsqrt(ms + EPS)
    return (y * gain) * scale1p + shift


def _norm_mod_to_scratch(x_ref, gain_ref, sc_ref, sh_ref, h_scr, rows=256):
    gain = gain_ref[...]
    scale1p = 1.0 + sc_ref[...]
    shift = sh_ref[...]

    def body(c, carry):
        r = pl.multiple_of(c * rows, rows)
        h = _norm_mod_rows(x_ref[pl.ds(r, rows), :], gain, scale1p, shift)
        h_scr[pl.ds(r, rows), :] = h.astype(bf16)
        return carry

    lax.fori_loop(0, x_ref.shape[0] // rows, body, 0)


def _ada_kernel(cond_ref, w_ref, b_ref, o_ref):
    c = cond_ref[...]
    s = (c * jax.nn.sigmoid(c)).astype(bf16)
    o_ref[...] = _mm(s, w_ref[...].astype(bf16)) + b_ref[...]


def _ada_params(cond, w_ada, b_ada):
    tn = 1024
    out = pl.pallas_call(
        _ada_kernel,
        grid=(DEPTH, 6 * D // tn),
        in_specs=[
            pl.BlockSpec((N_GROUPS_PADDED, D), lambda l, j: (0, 0)),
            pl.BlockSpec((None, D, tn), lambda l, j: (l, 0, j)),
            pl.BlockSpec((None, 1, tn), lambda l, j: (l, 0, j)),
        ],
        out_specs=pl.BlockSpec((None, N_GROUPS_PADDED, tn), lambda l, j: (l, 0, j)),
        out_shape=jax.ShapeDtypeStruct((DEPTH, N_GROUPS_PADDED, 6 * D), f32),
        compiler_params=_params(("arbitrary", "arbitrary"),
                                _nbytes((D, tn), f32) + _nbytes((8, tn), f32) * 2,
                                _nbytes((D, tn), bf16) + _nbytes((8, D), f32) * 2),
        name="ada_params",
    )(cond, w_ada, b_ada.reshape(DEPTH, 1, 6 * D))
    return out.reshape(DEPTH * N_GROUPS_PADDED * 6, 1, D)


def _conv_in_kernel(x_ref, gain_ref, sc_ref, sh_ref, wb_ref, wc_ref, wv_ref, b_ref, u_ref, h_scr):
    @pl.when(pl.program_id(1) == 0)
    def _():
        _norm_mod_to_scratch(x_ref, gain_ref, sc_ref, sh_ref, h_scr)

    h = h_scr[...]
    b_ref[...] = _mm(h, wb_ref[...])
    u_ref[...] = _mm(h, wc_ref[...]) * _mm(h, wv_ref[...])


def _conv_in(x, mod, norm, w_conv_in, layer):
    tm, tn = 1024, 256
    jl = layer // 2
    nb = D // tn
    return pl.pallas_call(
        _conv_in_kernel,
        grid=(T // tm, nb),
        in_specs=[
            pl.BlockSpec((tm, D), lambda i, j: (i, 0)),
            pl.BlockSpec((None, 1, D), lambda i, j: (layer, 0, 0)),
            _mod_spec(layer, 1, tm), _mod_spec(layer, 0, tm),
            pl.BlockSpec((None, D, tn), lambda i, j: (jl, 0, j)),
            pl.BlockSpec((None, D, tn), lambda i, j: (jl, 0, nb + j)),
            pl.BlockSpec((None, D, tn), lambda i, j: (jl, 0, 2 * nb + j)),
        ],
        out_specs=[pl.BlockSpec((tm, tn), lambda i, j: (i, j))] * 2,
        out_shape=[jax.ShapeDtypeStruct((T, D), f32)] * 2,
        scratch_shapes=[pltpu.VMEM((tm, D), bf16)],
        compiler_params=_params(("arbitrary", "arbitrary"),
                                _nbytes((tm, D), f32) + 3 * _nbytes((D, tn), bf16) + 2 * _nbytes((tm, tn), f32),
                                _nbytes((tm, D), bf16) + 3 * _nbytes((tm, tn), f32)),
        name="conv_in",
    )(x, norm.reshape(DEPTH, 1, D), mod, mod, w_conv_in, w_conv_in, w_conv_in)


def _ffn_up_kernel(x_ref, gain_ref, sc_ref, sh_ref, wg_ref, wu_ref, a_ref, h_scr):
    @pl.when(pl.program_id(1) == 0)
    def _():
        _norm_mod_to_scratch(x_ref, gain_ref, sc_ref, sh_ref, h_scr)

    h = h_scr[...]
    g = _mm(h, wg_ref[...])
    u = _mm(h, wu_ref[...])
    a_ref[...] = ((g * jax.nn.sigmoid(g)) * u).astype(bf16)


def _ffn_up(x, mod, norm, w_gate, w_up, layer):
    tm, tn = 1024, 512
    jl = layer // 2
    return pl.pallas_call(
        _ffn_up_kernel,
        grid=(T // tm, D_FF // tn),
        in_specs=[
            pl.BlockSpec((tm, D), lambda i, j: (i, 0)),
            pl.BlockSpec((None, 1, D), lambda i, j: (layer, 0, 0)),
            _mod_spec(layer, 4, tm), _mod_spec(layer, 3, tm),
            pl.BlockSpec((None, D, tn), lambda i, j: (jl, 0, j)),
            pl.BlockSpec((None, D, tn), lambda i, j: (jl, 0, j)),
        ],
        out_specs=pl.BlockSpec((tm, tn), lambda i, j: (i, j)),
        out_shape=jax.ShapeDtypeStruct((T, D_FF), bf16),
        scratch_shapes=[pltpu.VMEM((tm, D), bf16)],
        compiler_params=_params(("arbitrary", "arbitrary"),
                                _nbytes((tm, D), f32) + 2 * _nbytes((D, tn), bf16) + _nbytes((tm, tn), bf16),
                                _nbytes((tm, D), bf16) + 3 * _nbytes((tm, tn), f32)),
        name="ffn_up",
    )(x, norm.reshape(DEPTH, 1, D), mod, mod, w_gate, w_up)


def _qkv_kernel(x_ref, gain_ref, sc_ref, sh_ref, w_ref, qg_ref, kg_ref, o_ref, h_scr, *, n_q_blocks):
    j = pl.program_id(1)

    @pl.when(j == 0)
    def _():
        _norm_mod_to_scratch(x_ref, gain_ref, sc_ref, sh_ref, h_scr)

    acc = _mm(h_scr[...], w_ref[...])

    @pl.when(j >= 2 * n_q_blocks)
    def _():
        o_ref[...] = acc.astype(o_ref.dtype)

    @pl.when(j < 2 * n_q_blocks)
    def _():
        head_gain = jnp.where(j < n_q_blocks, qg_ref[...], kg_ref[...])
        for hh in range(acc.shape[1] // HEAD_DIM):
            a = acc[:, hh * HEAD_DIM:(hh + 1) * HEAD_DIM]
            ms = jnp.mean(a * a, axis=-1, keepdims=True)
            o_ref[:, hh * HEAD_DIM:(hh + 1) * HEAD_DIM] = ((a * lax.rsqrt(ms + EPS)) * head_gain).astype(o_ref.dtype)


def _qkv(x, mod, norm, w_qkv, q_norm, k_norm, layer, row0, n_rows, out_dtype):
    tm, tn = 1024, 1024
    jl = layer // 2
    tile0 = row0 // tm
    return pl.pallas_call(
        functools.partial(_qkv_kernel, n_q_blocks=D // tn),
        grid=(n_rows // tm, 3 * D // tn),
        in_specs=[
            pl.BlockSpec((tm, D), lambda i, j: (tile0 + i, 0)),
            pl.BlockSpec((None, 1, D), lambda i, j: (layer, 0, 0)),
            _mod_spec(layer, 1, tm, tile0=tile0), _mod_spec(layer, 0, tm, tile0=tile0),
            pl.BlockSpec((None, D, tn), lambda i, j: (jl, 0, j)),
            pl.BlockSpec((None, 1, HEAD_DIM), lambda i, j: (jl, 0, 0)),
            pl.BlockSpec((None, 1, HEAD_DIM), lambda i, j: (jl, 0, 0)),
        ],
        out_specs=pl.BlockSpec((tm, tn), lambda i, j: (i, j)),
        out_shape=jax.ShapeDtypeStruct((n_rows, 3 * D), out_dtype),
        scratch_shapes=[pltpu.VMEM((tm, D), bf16)],
        compiler_params=_params(("arbitrary", "arbitrary"),
                                _nbytes((tm, D), f32) + _nbytes((D, tn), bf16) + _nbytes((tm, tn), out_dtype),
                                _nbytes((tm, D), bf16) + 2 * _nbytes((tm, tn), f32)),
        name="qkv",
    )(x, norm.reshape(DEPTH, 1, D), mod, mod, w_qkv,
      q_norm.reshape(-1, 1, HEAD_DIM), k_norm.reshape(-1, 1, HEAD_DIM))


def _mm_res_kernel(*refs, n_ctx_tiles):
    *a_refs, w_ref, x_ref, gate_ref, o_ref = refs

    def update(a_ref):
        o_ref[...] = x_ref[...] + gate_ref[...] * _mm(a_ref[...], w_ref[...])

    if len(a_refs) == 1:
        update(a_refs[0])
    else:
        i = pl.program_id(0)

        @pl.when(i < n_ctx_tiles)
        def _():
            update(a_refs[0])

        @pl.when(i >= n_ctx_tiles)
        def _():
            update(a_refs[1])


def _mm_residual(a_parts, w, x, mod, layer, which, tm, tn, name):
    k = a_parts[0].shape[1]
    jl = layer // 2
    n_ctx_tiles = CTX_ROWS // tm
    if len(a_parts) == 1:
        a_specs = [pl.BlockSpec((tm, k), lambda i, j: (i, 0))]
    else:
        a_specs = [pl.BlockSpec((tm, k), lambda i, j: (jnp.minimum(i, n_ctx_tiles - 1), 0)),
                   pl.BlockSpec((tm, k), lambda i, j: (jnp.maximum(i - n_ctx_tiles, 0), 0))]
    return pl.pallas_call(
        functools.partial(_mm_res_kernel, n_ctx_tiles=n_ctx_tiles),
        grid=(T // tm, D // tn),
        in_specs=a_specs + [
            pl.BlockSpec((None, k, tn), lambda i, j: (jl, 0, j)),
            pl.BlockSpec((tm, tn), lambda i, j: (i, j)),
            _mod_spec(layer, which, tm, tn),
        ],
        out_specs=pl.BlockSpec((tm, tn), lambda i, j: (i, j)),
        out_shape=jax.ShapeDtypeStruct((T, D), f32),
        compiler_params=_params(("arbitrary", "arbitrary"),
                                len(a_parts) * _nbytes((tm, k), bf16) + _nbytes((k, tn), bf16)
                                + 2 * _nbytes((tm, tn), f32),
                                2 * _nbytes((tm, tn), f32)),
        name=name,
    )(*a_parts, w, x, mod)


def _conv_gate_kernel(b_ref, u_ref, up_ref, un_ref, cw_ref, o_ref, *, tm, cols):
    row0 = pl.program_id(0) * tm
    seq_len = jnp.where(row0 < CTX_ROWS, CTX_LEN, DEC_LEN)
    local = lax.broadcasted_iota(jnp.int32, (tm, cols), 0)
    pos = (row0 + local) & (seq_len - 1)
    is_first = pos == 0
    is_last = pos == seq_len - 1
    for c in range(D // cols):
        sl = slice(c * cols, (c + 1) * cols)
        u = u_ref[:, sl]
        prev_row = up_ref[SUBLANES - 1:SUBLANES, sl]
        next_row = un_ref[0:1, sl]
        u_prev = jnp.where(local == 0, prev_row, pltpu.roll(u, 1, axis=0))
        u_next = jnp.where(local == tm - 1, next_row, pltpu.roll(u, tm - 1, axis=0))
        u_prev = jnp.where(is_first, 0.0, u_prev)
        u_next = jnp.where(is_last, 0.0, u_next)
        conv = u_prev * cw_ref[0:1, sl] + u * cw_ref[1:2, sl] + u_next * cw_ref[2:3, sl]
        o_ref[:, sl] = (b_ref[:, sl] * conv).astype(bf16)


def _conv_gate(b, u, conv_w, layer):
    tm, cols = 512, 512
    jl = layer // 2
    blocks_per_tile = tm // SUBLANES
    return pl.pallas_call(
        functools.partial(_conv_gate_kernel, tm=tm, cols=cols),
        grid=(T // tm,),
        in_specs=[
            pl.BlockSpec((tm, D), lambda i: (i, 0)),
            pl.BlockSpec((tm, D), lambda i: (i, 0)),
            pl.BlockSpec((SUBLANES, D), lambda i: (jnp.maximum(i * blocks_per_tile - 1, 0), 0)),
            pl.BlockSpec((SUBLANES, D), lambda i: (jnp.minimum((i + 1) * blocks_per_tile, T // SUBLANES - 1), 0)),
            pl.BlockSpec((None, 3, D), lambda i: (jl, 0, 0)),
        ],
        out_specs=pl.BlockSpec((tm, D), lambda i: (i, 0)),
        out_shape=jax.ShapeDtypeStruct((T, D), bf16),
        compiler_params=_params(("arbitrary",),
                                2 * _nbytes((tm, D), f32) + _nbytes((tm, D), bf16) + 2 * _nbytes((SUBLANES, D), f32),
                                8 * _nbytes((tm, cols), f32)),
        name="conv_gate",
    )(b, u, u, u, conv_w)


def _softmax_pv(parts):
    m = parts[0][0].max(axis=-1, keepdims=True)
    for s, _ in parts[1:]:
        m = jnp.maximum(m, s.max(axis=-1, keepdims=True))
    den = None
    out = None
    for s, v in parts:
        p = jnp.exp2(s - m)
        d = p.sum(axis=-1, keepdims=True)
        o = _mm(p.astype(bf16), v)
        den = d if den is None else den + d
        out = o if out is None else out + o
    return out * (1.0 / den)


def _ctx_attn_kernel(q_ref, k_ref, v_ref, o_ref, k_out_ref, v_out_ref):
    k_out_ref[...] = k_ref[...]
    v_out_ref[...] = v_ref[...]
    for hh in range(q_ref.shape[1] // HEAD_DIM):
        sl = slice(hh * HEAD_DIM, (hh + 1) * HEAD_DIM)
        q = q_ref[:, sl].astype(bf16)
        k = k_ref[:, sl].astype(bf16)
        v = v_ref[:, sl].astype(bf16)
        s = _mm_nt(q, k) * SCORE_SCALE_LOG2
        o_ref[:, sl] = _softmax_pv([(s, v)]).astype(bf16)


def _ctx_attention(qkv):
    tn = 512
    nb = D // tn
    return pl.pallas_call(
        _ctx_attn_kernel,
        grid=(N_CTX_SEQ, nb),
        in_specs=[
            pl.BlockSpec((CTX_LEN, tn), lambda b, g: (b, g)),
            pl.BlockSpec((CTX_LEN, tn), lambda b, g: (b, nb + g)),
            pl.BlockSpec((CTX_LEN, tn), lambda b, g: (b, 2 * nb + g)),
        ],
        out_specs=[pl.BlockSpec((CTX_LEN, tn), lambda b, g: (b, g))] * 3,
        out_shape=[jax.ShapeDtypeStruct((CTX_ROWS, D), bf16),
                   jax.ShapeDtypeStruct((CTX_ROWS, D), f32),
                   jax.ShapeDtypeStruct((CTX_ROWS, D), f32)],
        compiler_params=_params(("arbitrary", "arbitrary"),
                                5 * _nbytes((CTX_LEN, tn), f32) + _nbytes((CTX_LEN, tn), bf16),
                                8 * _nbytes((CTX_LEN, CTX_LEN), f32)),
        name="ctx_attention",
    )(qkv, qkv, qkv)


def _band_start_row(m):
    return jnp.clip(Q_ROWS_PER_BLOCK * m - WIN_ROWS // 2, 0, GRID_ROWS - BAND_ROWS)


def _band_block_ids(m):
    band0 = _band_start_row(m)
    ids = []
    for rho in range(Q_ROWS_PER_BLOCK):
        r = Q_ROWS_PER_BLOCK * m + rho
        win0 = jnp.clip(r - WIN_ROWS // 2, 0, GRID_ROWS - WIN_ROWS)
        row_ids = []
        for kap in range(BAND_ROWS):
            kr = band0 + kap
            inside = (kr >= win0) & (kr < win0 + WIN_ROWS)
            row_ids.append(jnp.where(inside, kr - r + WIN_ROWS - 1, MASKED_BLOCK))
        ids.append(row_ids)
    return ids


def _band_bias(bias_ref, hh, ids):
    left_half = lax.broadcasted_iota(jnp.int32, (GRID_W, 2 * GRID_W), 1) < GRID_W
    rows = []
    for rho in range(Q_ROWS_PER_BLOCK):
        pairs = [jnp.where(left_half, bias_ref[hh, ids[rho][kap]], bias_ref[hh, ids[rho][kap + 1]])
                 for kap in range(0, BAND_ROWS, 2)]
        rows.append(jnp.concatenate(pairs, axis=1))
    return jnp.concatenate(rows, axis=0)


def _loc_attn_kernel(q_ref, k_ref, v_ref, ck_ref, cv_ref, bias_ref, o_ref):
    start = pl.multiple_of(_band_start_row(pl.program_id(2)) * GRID_W, GRID_W)
    ids = _band_block_ids(pl.program_id(2))
    for hh in range(q_ref.shape[1] // HEAD_DIM):
        sl = slice(hh * HEAD_DIM, (hh + 1) * HEAD_DIM)
        q = q_ref[:, sl]
        k_band = k_ref[pl.ds(start, BAND), sl]
        v_band = v_ref[pl.ds(start, BAND), sl]
        k_ctx = ck_ref[:, sl].astype(bf16)
        v_ctx = cv_ref[:, sl].astype(bf16)
        s_loc = _mm_nt(q, k_band) * SCORE_SCALE_LOG2 + _band_bias(bias_ref, hh, ids)
        s_ctx = _mm_nt(q, k_ctx) * SCORE_SCALE_LOG2
        o_ref[:, sl] = _softmax_pv([(s_loc, v_band), (s_ctx, v_ctx)]).astype(bf16)


def _rel_bias_blocks(rpb_l):
    cols = np.arange(GRID_W)
    col_start = np.clip(cols - WIN_COLS // 2, 0, GRID_W - WIN_COLS)
    col_ok = (cols[None, :] >= col_start[:, None]) & (cols[None, :] < col_start[:, None] + WIN_COLS)
    pad = GRID_W - WIN_COLS
    padded = jnp.pad(rpb_l.astype(f32), ((0, 0), (0, 0), (pad, pad)))
    skewed = jnp.tile(padded, (1, 1, GRID_W + 1))[:, :, :GRID_W * 2 * GRID_W]
    skewed = skewed.reshape(N_HEADS, MASKED_BLOCK, GRID_W, 2 * GRID_W)[..., :GRID_W]
    toeplitz = jnp.flip(skewed, axis=2)
    toeplitz = jnp.where(col_ok[None, None], toeplitz * LOG2_E, NEG_INF)
    masked = jnp.full((N_HEADS, 1, GRID_W, GRID_W), NEG_INF, f32)
    blocks = jnp.concatenate([toeplitz, masked], axis=1)
    return jnp.concatenate([blocks, blocks], axis=3)


def _latent_attention(qkv, cache_k, cache_v, layer_slot, bias_blocks):
    tn = 512
    nb = D // tn
    hb = tn // HEAD_DIM
    return pl.pallas_call(
        _loc_attn_kernel,
        grid=(N_DEC_SEQ, nb, N_Q_BLOCKS),
        in_specs=[
            pl.BlockSpec((Q_BLOCK, tn), lambda b, g, m: (b * N_Q_BLOCKS + m, g)),
            pl.BlockSpec((DEC_LEN, tn), lambda b, g, m: (b, nb + g)),
            pl.BlockSpec((DEC_LEN, tn), lambda b, g, m: (b, 2 * nb + g)),
            pl.BlockSpec((None, None, PAST_LEN, tn), lambda b, g, m: (b, layer_slot, 0, g)),
            pl.BlockSpec((None, None, PAST_LEN, tn), lambda b, g, m: (b, layer_slot, 0, g)),
            pl.BlockSpec((hb, N_BIAS_BLOCKS, GRID_W, 2 * GRID_W), lambda b, g, m: (g, 0, 0, 0)),
        ],
        out_specs=pl.BlockSpec((Q_BLOCK, tn), lambda b, g, m: (b * N_Q_BLOCKS + m, g)),
        out_shape=jax.ShapeDtypeStruct((DEC_ROWS, D), bf16),
        compiler_params=_params(("arbitrary", "arbitrary", "arbitrary"),
                                _nbytes((Q_BLOCK, tn), bf16) + 2 * _nbytes((DEC_LEN, tn), bf16)
                                + 2 * _nbytes((PAST_LEN, tn), f32)
                                + _nbytes((hb, N_BIAS_BLOCKS, GRID_W, 2 * GRID_W), f32)
                                + _nbytes((Q_BLOCK, tn), bf16),
                                10 * _nbytes((Q_BLOCK, BAND), f32)),
        name="latent_attention",
    )(qkv, qkv, qkv, cache_k, cache_v, bias_blocks)


HI_HALF_MASK = 0xFFFF0000


def _pack_bf16_halves(h):
    bits = lax.bitcast_convert_type(h.astype(f32), jnp.uint32)
    half = h.shape[1] // 2
    return (bits[:, :half] >> 16) | (bits[:, half:] & jnp.uint32(HI_HALF_MASK))


def _unpack_bf16_halves(words):
    lo = lax.bitcast_convert_type(words << 16, f32).astype(bf16)
    hi = lax.bitcast_convert_type(words & jnp.uint32(HI_HALF_MASK), f32).astype(bf16)
    return lo, hi


def _router_kernel(x_ref, gain_ref, sc_ref, sh_ref, wr_ref, hp_ref, idx_ref, wt_ref, rank_ref, cnt_ref, seen):
    tm = x_ref.shape[0]

    @pl.when(pl.program_id(0) == 0)
    def _():
        seen[...] = jnp.zeros(seen.shape, seen.dtype)

    h = _norm_mod_rows(x_ref[...], gain_ref[...], 1.0 + sc_ref[...], sh_ref[...]).astype(bf16)
    hp_ref[...] = _pack_bf16_halves(h)
    logits = _mm(h, wr_ref[...].astype(bf16))
    lane = lax.broadcasted_iota(jnp.int32, (tm, LANES), 1)
    l1 = jnp.where(lane < N_EXPERTS, logits, -jnp.inf)
    m1 = l1.max(axis=-1, keepdims=True)
    i1 = jnp.where(l1 == m1, lane, LANES).min(axis=-1, keepdims=True)
    l2 = jnp.where(lane == i1, -jnp.inf, l1)
    m2 = l2.max(axis=-1, keepdims=True)
    i2 = jnp.where(l2 == m2, lane, LANES).min(axis=-1, keepdims=True)
    e = jnp.exp(m2 - m1)
    w1 = 1.0 / (1.0 + e)
    w2 = e / (1.0 + e)
    idx_ref[...] = jnp.where(lane == 0, i1, jnp.where(lane == 1, i2, 0))
    wt_ref[...] = jnp.where(lane == 0, w1, jnp.where(lane == 1, w2, 0.0))

    first = lane == i1
    second = lane == i2
    chosen = (first | second).astype(bf16)
    earlier = (lax.broadcasted_iota(jnp.int32, (tm, tm), 1) < lax.broadcasted_iota(jnp.int32, (tm, tm), 0))
    before = _mm(earlier.astype(bf16), chosen) + seen[...]
    r1 = jnp.where(first, before, 0.0).sum(axis=-1, keepdims=True)
    r2 = jnp.where(second, before, 0.0).sum(axis=-1, keepdims=True)
    rank_ref[...] = jnp.where(lane == 0, r1, jnp.where(lane == 1, r2, 0.0)).astype(jnp.int32)
    seen[...] += chosen.astype(f32).sum(axis=0, keepdims=True)
    cnt_ref[...] = seen[...].astype(jnp.int32)


def _router(x, mod, norm, w_router_padded, layer):
    tm = 512
    jl = layer // 2
    lane_block = pl.BlockSpec((tm, LANES), lambda i: (i, 0))
    return pl.pallas_call(
        _router_kernel,
        grid=(T // tm,),
        in_specs=[
            pl.BlockSpec((tm, D), lambda i: (i, 0)),
            pl.BlockSpec((None, 1, D), lambda i: (layer, 0, 0)),
            _mod_spec(layer, 4, tm), _mod_spec(layer, 3, tm),
            pl.BlockSpec((None, D, LANES), lambda i: (jl, 0, 0)),
        ],
        out_specs=[pl.BlockSpec((tm, D // 2), lambda i: (i, 0)), lane_block, lane_block, lane_block,
                   pl.BlockSpec((1, LANES), lambda i: (0, 0))],
        out_shape=[jax.ShapeDtypeStruct((T, D // 2), jnp.uint32),
                   jax.ShapeDtypeStruct((T, LANES), jnp.int32),
                   jax.ShapeDtypeStruct((T, LANES), f32),
                   jax.ShapeDtypeStruct((T, LANES), jnp.int32),
                   jax.ShapeDtypeStruct((1, LANES), jnp.int32)],
        scratch_shapes=[pltpu.VMEM((1, LANES), f32)],
        compiler_params=_params(("arbitrary",),
                                _nbytes((tm, D), f32) + _nbytes((tm, D), bf16) + 3 * _nbytes((tm, LANES), f32),
                                _nbytes((D, LANES), f32) + 4 * _nbytes((tm, D), f32) + 2 * _nbytes((tm, tm), f32)),
        name="router",
    )(x, norm.reshape(DEPTH, 1, D), mod, mod, w_router_padded)


def _routing_plan(expert_ids, ranks, counts):
    e_flat = expert_ids.reshape(-1)
    padded = ((counts + MOE_TM - 1) // MOE_TM) * MOE_TM
    group_end = jnp.cumsum(padded)
    group_start = group_end - padded
    onehot = e_flat[:, None] == jnp.arange(N_EXPERTS, dtype=jnp.int32)[None, :]
    dest = ranks.reshape(-1) + jnp.sum(jnp.where(onehot, group_start[None, :], 0), axis=1)
    src_token = jnp.zeros((MOE_ROWS,), jnp.int32).at[dest].set(
        jnp.arange(N_ASSIGN, dtype=jnp.int32) // TOP_K, unique_indices=True)
    n_tiles = (group_end[-1] // MOE_TM).astype(jnp.int32)
    tile_row0 = jnp.arange(MOE_TILES, dtype=jnp.int32) * MOE_TM
    tile_expert = jnp.sum((tile_row0[:, None] >= group_end[None, :]).astype(jnp.int32), axis=1)
    tile_expert = jnp.minimum(tile_expert, N_EXPERTS - 1)
    last_used = tile_expert[jnp.maximum(n_tiles - 1, 0)]
    tile_expert = jnp.where(jnp.arange(MOE_TILES) < n_tiles, tile_expert, last_used).astype(jnp.int32)
    return src_token, dest.astype(jnp.int32), tile_expert, n_tiles.reshape(1)


def _dispatch_kernel(src_ref, nt_ref, hp_hbm, o_ref, buf, sem):
    i = pl.program_id(0)
    chunk = o_ref.shape[0]
    base = i * chunk

    def row_copy(r):
        return pltpu.make_async_copy(hp_hbm.at[pl.ds(src_ref[base + r], 1), :], buf.at[pl.ds(r, 1), :], sem)

    @pl.when(i < nt_ref[0])
    def _():
        def start(r, carry):
            row_copy(r).start()
            return carry

        def wait(r, carry):
            row_copy(r).wait()
            return carry

        lax.fori_loop(0, chunk, start, 0, unroll=DMA_UNROLL)
        lax.fori_loop(0, chunk, wait, 0, unroll=DMA_UNROLL)
        lo, hi = _unpack_bf16_halves(buf[...])
        o_ref[:, :D // 2] = lo
        o_ref[:, D // 2:] = hi

    @pl.when(i >= nt_ref[0])
    def _():
        o_ref[...] = jnp.zeros(o_ref.shape, o_ref.dtype)


def _dispatch(h_packed, src_token, n_tiles):
    chunk = MOE_TM
    return pl.pallas_call(
        _dispatch_kernel,
        grid_spec=pltpu.PrefetchScalarGridSpec(
            num_scalar_prefetch=2,
            grid=(MOE_ROWS // chunk,),
            in_specs=[pl.BlockSpec(memory_space=pl.ANY)],
            out_specs=pl.BlockSpec((chunk, D), lambda i, src, nt: (i, 0)),
            scratch_shapes=[pltpu.VMEM((chunk, D // 2), jnp.uint32), pltpu.SemaphoreType.DMA(())],
        ),
        out_shape=jax.ShapeDtypeStruct((MOE_ROWS, D), bf16),
        compiler_params=_params(("arbitrary",), _nbytes((chunk, D), bf16), 4 * _nbytes((chunk, D // 2), f32)),
        name="moe_dispatch",
    )(src_token, n_tiles, h_packed)


def _expert_changed(te_ref, i):
    return (i == 0) | (te_ref[i] != te_ref[jnp.maximum(i - 1, 0)])


def _moe_up_kernel(te_ref, nt_ref, hs_ref, wg_ref, wu_ref, a_ref, wg_s, wu_s):
    i = pl.program_id(1)

    @pl.when(_expert_changed(te_ref, i))
    def _():
        wg_s[...] = wg_ref[...].astype(bf16)
        wu_s[...] = wu_ref[...].astype(bf16)

    @pl.when(i < nt_ref[0])
    def _():
        h = hs_ref[...]
        g = _mm(h, wg_s[...])
        u = _mm(h, wu_s[...])
        a_ref[...] = ((g * jax.nn.sigmoid(g)) * u).astype(bf16)

    @pl.when(i >= nt_ref[0])
    def _():
        a_ref[...] = jnp.zeros(a_ref.shape, a_ref.dtype)


def _moe_up(hs, w_gate, w_up, tile_expert, n_tiles, layer):
    tm, tn = MOE_TM, 512
    jl = layer // 2
    return pl.pallas_call(
        _moe_up_kernel,
        grid_spec=pltpu.PrefetchScalarGridSpec(
            num_scalar_prefetch=2,
            grid=(D_FF // tn, MOE_TILES),
            in_specs=[
                pl.BlockSpec((tm, D), lambda j, i, te, nt: (i, 0)),
                pl.BlockSpec((None, None, D, tn), lambda j, i, te, nt: (jl, te[i], 0, j)),
                pl.BlockSpec((None, None, D, tn), lambda j, i, te, nt: (jl, te[i], 0, j)),
            ],
            out_specs=pl.BlockSpec((tm, tn), lambda j, i, te, nt: (i, j)),
            scratch_shapes=[pltpu.VMEM((D, tn), bf16), pltpu.VMEM((D, tn), bf16)],
        ),
        out_shape=jax.ShapeDtypeStruct((MOE_ROWS, D_FF), bf16),
        compiler_params=_params(("arbitrary", "arbitrary"),
                                _nbytes((tm, D), bf16) + 2 * _nbytes((D, tn), f32) + _nbytes((tm, tn), bf16),
                                2 * _nbytes((D, tn), bf16) + 3 * _nbytes((tm, tn), f32)),
        name="moe_up",
    )(tile_expert, n_tiles, hs, w_gate, w_up)


def _moe_down_kernel(te_ref, nt_ref, a_ref, wd_ref, y_ref, wd_s):
    i = pl.program_id(1)

    @pl.when(_expert_changed(te_ref, i))
    def _():
        wd_s[...] = wd_ref[...].astype(bf16)

    @pl.when(i < nt_ref[0])
    def _():
        y_ref[...] = _mm(a_ref[...], wd_s[...])

    @pl.when(i >= nt_ref[0])
    def _():
        y_ref[...] = jnp.zeros(y_ref.shape, y_ref.dtype)


def _moe_down(a, w_down, tile_expert, n_tiles, layer):
    tm, tn = MOE_TM, 512
    jl = layer // 2
    return pl.pallas_call(
        _moe_down_kernel,
        grid_spec=pltpu.PrefetchScalarGridSpec(
            num_scalar_prefetch=2,
            grid=(D // tn, MOE_TILES),
            in_specs=[
                pl.BlockSpec((tm, D_FF), lambda j, i, te, nt: (i, 0)),
                pl.BlockSpec((None, None, D_FF, tn), lambda j, i, te, nt: (jl, te[i], 0, j)),
            ],
            out_specs=pl.BlockSpec((tm, tn), lambda j, i, te, nt: (i, j)),
            scratch_shapes=[pltpu.VMEM((D_FF, tn), bf16)],
        ),
        out_shape=jax.ShapeDtypeStruct((MOE_ROWS, D), f32),
        compiler_params=_params(("arbitrary", "arbitrary"),
                                _nbytes((tm, D_FF), bf16) + _nbytes((D_FF, tn), f32) + _nbytes((tm, tn), f32),
                                _nbytes((D_FF, tn), bf16) + _nbytes((tm, tn), f32)),
        name="moe_down",
    )(tile_expert, n_tiles, a, w_down)


def _combine_kernel(pos_ref, x_ref, gate_ref, wt_ref, y_hbm, o_ref, ybuf, sem, *, tm):
    base = pl.program_id(0) * tm * TOP_K

    def copies(t):
        return [pltpu.make_async_copy(y_hbm.at[pl.ds(pos_ref[base + t * TOP_K + c], 1), :],
                                      ybuf.at[c, pl.ds(t, 1), :], sem.at[c])
                for c in range(TOP_K)]

    def start(t, carry):
        for cp in copies(t):
            cp.start()
        return carry

    def wait(t, carry):
        for cp in copies(t):
            cp.wait()
        return carry

    lax.fori_loop(0, tm, start, 0, unroll=DMA_UNROLL)
    lax.fori_loop(0, tm, wait, 0, unroll=DMA_UNROLL)
    w1 = jnp.broadcast_to(wt_ref[:, 0:1], (tm, LANES))
    w2 = jnp.broadcast_to(wt_ref[:, 1:2], (tm, LANES))
    for s in range(ROW_SLABS):
        sl = slice(s * LANES, (s + 1) * LANES)
        mix = w1 * ybuf[0, :, sl] + w2 * ybuf[1, :, sl]
        o_ref[:, sl] = x_ref[:, sl] + gate_ref[:, sl] * mix


def _combine(x, y_slabs, pos, wts, mod, layer):
    tm = 256
    return pl.pallas_call(
        functools.partial(_combine_kernel, tm=tm),
        grid_spec=pltpu.PrefetchScalarGridSpec(
            num_scalar_prefetch=1,
            grid=(T // tm,),
            in_specs=[
                pl.BlockSpec((tm, D), lambda i, pos: (i, 0)),
                _mod_spec(layer, 5, tm),
                pl.BlockSpec((tm, LANES), lambda i, pos: (i, 0)),
                pl.BlockSpec(memory_space=pl.ANY),
            ],
            out_specs=pl.BlockSpec((tm, D), lambda i, pos: (i, 0)),
            scratch_shapes=[pltpu.VMEM((TOP_K, tm, D), f32),
                            pltpu.SemaphoreType.DMA((TOP_K,))],
        ),
        out_shape=jax.ShapeDtypeStruct((T, D), f32),
        compiler_params=_params(("arbitrary",),
                                2 * _nbytes((tm, D), f32) + _nbytes((tm, LANES), f32),
                                _nbytes((TOP_K, tm, D), f32) + 4 * _nbytes((tm, LANES), f32)),
        name="moe_combine",
    )(pos, x, mod, wts, y_slabs)


def _moe_layer(x, mod, norm2, w_router, w_gate, w_up, w_down, layer):
    wr = jnp.pad(w_router, ((0, 0), (0, 0), (0, LANES - N_EXPERTS)))
    h_packed, idx, wts, ranks, counts = _router(x, mod, norm2, wr, layer)
    src_token, pos, tile_expert, n_tiles = _routing_plan(idx[:, :TOP_K], ranks[:, :TOP_K], counts[0, :N_EXPERTS])
    hs = _dispatch(h_packed, src_token, n_tiles)
    a = _moe_up(hs, w_gate, w_up, tile_expert, n_tiles, layer)
    y = _moe_down(a, w_down, tile_expert, n_tiles, layer)
    return _combine(x, y, pos, wts, mod, layer)


def kernel(x_prompt, x_sample, cache_k, cache_v, c, c_ctx, norm1, norm2, w_ada, b_ada, w_conv_in, conv_w,
           w_conv_out, w_qkv, q_norm, k_norm, rpb, w_attn_out, w_ffn_gate, w_ffn_up, w_ffn_down, w_router,
           w_moe_gate, w_moe_up, w_moe_down):
    x = jnp.concatenate([x_prompt.reshape(CTX_ROWS, D), x_sample.reshape(DEC_ROWS, D)], axis=0)
    cond = jnp.concatenate([c_ctx[None, :], c, jnp.zeros((N_GROUPS_PADDED - 1 - N_DEC_SEQ, D), f32)], axis=0)
    mod = _ada_params(cond, w_ada, b_ada)
    cache_k2 = cache_k.reshape(N_DEC_SEQ, -1, PAST_LEN, D)
    cache_v2 = cache_v.reshape(N_DEC_SEQ, -1, PAST_LEN, D)
    w_conv_in, w_conv_out, w_qkv, w_attn_out, w_ffn_gate, w_ffn_up, w_ffn_down = (
        w.astype(bf16) for w in (w_conv_in, w_conv_out, w_qkv, w_attn_out, w_ffn_gate, w_ffn_up, w_ffn_down))

    new_k, new_v = [], []
    for layer in range(DEPTH):
        jl = layer // 2
        if layer % 2 == 0:
            b, u = _conv_in(x, mod, norm1, w_conv_in, layer)
            gated = _conv_gate(b, u, conv_w, layer)
            x = _mm_residual([gated], w_conv_out, x, mod, layer, 2, 1024, 1024, "conv_out")
            a = _ffn_up(x, mod, norm2, w_ffn_gate, w_ffn_up, layer)
            x = _mm_residual([a], w_ffn_down, x, mod, layer, 5, 1024, 512, "ffn_down")
        else:
            qkv_ctx = _qkv(x, mod, norm1, w_qkv, q_norm, k_norm, layer, 0, CTX_ROWS, f32)
            qkv_dec = _qkv(x, mod, norm1, w_qkv, q_norm, k_norm, layer, CTX_ROWS, DEC_ROWS, bf16)
            o_ctx, k_ctx, v_ctx = _ctx_attention(qkv_ctx)
            new_k.append(k_ctx.reshape(N_CTX_SEQ, CTX_LEN, N_HEADS, HEAD_DIM))
            new_v.append(v_ctx.reshape(N_CTX_SEQ, CTX_LEN, N_HEADS, HEAD_DIM))
            o_dec = _latent_attention(qkv_dec, cache_k2, cache_v2, jl, _rel_bias_blocks(rpb[jl]))
            x = _mm_residual([o_ctx, o_dec], w_attn_out, x, mod, layer, 2, 1024, 1024, "attn_out")
            x = _moe_layer(x, mod, norm2, w_router, w_moe_gate, w_moe_up, w_moe_down, layer)

    y_prompt = x[:CTX_ROWS].reshape(N_CTX_SEQ, CTX_LEN, D)
    y_sample = x[CTX_ROWS:].reshape(N_DEC_SEQ, DEC_LEN, D)
    return y_prompt, y_sample, jnp.stack(new_k, axis=1), jnp.stack(new_v, axis=1)
```

```python
import functools

import numpy as np
import jax
import jax.numpy as jnp
from jax import lax
from jax.experimental import pallas as pl
from jax.experimental.pallas import tpu as pltpu

D = 2048
N_CTX_SEQ = 16
CTX_LEN = 256
N_DEC_SEQ = 4
DEC_LEN = 2048
CTX_ROWS = N_CTX_SEQ * CTX_LEN
DEC_ROWS = N_DEC_SEQ * DEC_LEN
T = CTX_ROWS + DEC_ROWS
DEPTH = 4
PAST_LEN = 512
N_HEADS = 16
HEAD_DIM = 128
GRID_W = 64
GRID_ROWS = DEC_LEN // GRID_W
WIN_ROWS = 8
WIN_COLS = 16
D_FF = 5632
N_EXPERTS = 8
TOP_K = 2
EPS = 1e-6
NEG_INF = -1e30
ATTN_SCALE = HEAD_DIM ** -0.5
LOG2_E = 1.4426950408889634
SCORE_SCALE_LOG2 = ATTN_SCALE * LOG2_E

N_GROUPS_PADDED = 8
LANES = 128
SUBLANES = 8
VMEM_CAP_BYTES = 56 * 2 ** 20
VMEM_HEADROOM = 4 * 2 ** 20

Q_ROWS_PER_BLOCK = 4
Q_BLOCK = Q_ROWS_PER_BLOCK * GRID_W
N_Q_BLOCKS = GRID_ROWS // Q_ROWS_PER_BLOCK
BAND_ROWS = Q_ROWS_PER_BLOCK + WIN_ROWS
BAND = BAND_ROWS * GRID_W
MASKED_BLOCK = 2 * WIN_ROWS - 1
N_BIAS_BLOCKS = MASKED_BLOCK + 1

MOE_TM = 512
DMA_UNROLL = 8
N_ASSIGN = T * TOP_K
MOE_ROWS = N_ASSIGN + N_EXPERTS * MOE_TM
MOE_TILES = MOE_ROWS // MOE_TM
ROW_SLABS = D // LANES

bf16 = jnp.bfloat16
f32 = jnp.float32


def _mm(a, b):
    return jnp.dot(a, b, preferred_element_type=f32)


def _mm_nt(a, b):
    return lax.dot_general(a, b, (((1,), (1,)), ((), ())), preferred_element_type=f32)


def _nbytes(shape, dtype):
    return int(np.prod(shape)) * jnp.dtype(dtype).itemsize


def _params(semantics, pipelined_bytes, resident_bytes):
    need = 2 * pipelined_bytes + resident_bytes + VMEM_HEADROOM
    return pltpu.CompilerParams(dimension_semantics=semantics,
                                vmem_limit_bytes=int(min(max(need, 16 * 2 ** 20), VMEM_CAP_BYTES)))


def _group_of_tile(i, tm):
    n_ctx = CTX_ROWS // tm
    per_seq = DEC_LEN // tm
    return jnp.where(i < n_ctx, 0, 1 + (i - n_ctx) // per_seq)


def _mod_spec(layer, which, tm, width=D, tile0=0):
    def index(*grid_and_prefetch):
        i = tile0 + grid_and_prefetch[0]
        j = grid_and_prefetch[1] if width != D else 0
        return ((layer * N_GROUPS_PADDED + _group_of_tile(i, tm)) * 6 + which, 0, j)
    return pl.BlockSpec((None, 1, width), index)


def _norm_mod_rows(x, gain, scale1p, shift):
    ms = jnp.mean(x * x, axis=-1, keepdims=True)
    y = x * lax.rsqrt(ms + EPS)
    return (y * gain) * scale1p + shift


def _norm_mod_to_scratch(x_ref, gain_ref, sc_ref, sh_ref, h_scr, rows=256):
    gain = gain_ref[...]
    scale1p = 1.0 + sc_ref[...]
    shift = sh_ref[...]

    def body(c, carry):
        r = pl.multiple_of(c * rows, rows)
        h = _norm_mod_rows(x_ref[pl.ds(r, rows), :], gain, scale1p, shift)
        h_scr[pl.ds(r, rows), :] = h.astype(bf16)
        return carry

    lax.fori_loop(0, x_ref.shape[0] // rows, body, 0)


def _ada_kernel(cond_ref, w_ref, b_ref, o_ref):
    c = cond_ref[...]
    s = (c * jax.nn.sigmoid(c)).astype(bf16)
    o_ref[...] = _mm(s, w_ref[...].astype(bf16)) + b_ref[...]


def _ada_params(cond, w_ada, b_ada):
    tn = 1024
    out = pl.pallas_call(
        _ada_kernel,
        grid=(DEPTH, 6 * D // tn),
        in_specs=[
            pl.BlockSpec((N_GROUPS_PADDED, D), lambda l, j: (0, 0)),
            pl.BlockSpec((None, D, tn), lambda l, j: (l, 0, j)),
            pl.BlockSpec((None, 1, tn), lambda l, j: (l, 0, j)),
        ],
        out_specs=pl.BlockSpec((None, N_GROUPS_PADDED, tn), lambda l, j: (l, 0, j)),
        out_shape=jax.ShapeDtypeStruct((DEPTH, N_GROUPS_PADDED, 6 * D), f32),
        compiler_params=_params(("arbitrary", "arbitrary"),
                                _nbytes((D, tn), f32) + _nbytes((8, tn), f32) * 2,
                                _nbytes((D, tn), bf16) + _nbytes((8, D), f32) * 2),
        name="ada_params",
    )(cond, w_ada, b_ada.reshape(DEPTH, 1, 6 * D))
    return out.reshape(DEPTH * N_GROUPS_PADDED * 6, 1, D)


def _conv_in_kernel(x_ref, gain_ref, sc_ref, sh_ref, wb_ref, wc_ref, wv_ref, b_ref, u_ref, h_scr):
    @pl.when(pl.program_id(1) == 0)
    def _():
        _norm_mod_to_scratch(x_ref, gain_ref, sc_ref, sh_ref, h_scr)

    h = h_scr[...]
    b_ref[...] = _mm(h, wb_ref[...])
    u_ref[...] = _mm(h, wc_ref[...]) * _mm(h, wv_ref[...])


def _conv_in(x, mod, norm, w_conv_in, layer):
    tm, tn = 1024, 256
    jl = layer // 2
    nb = D // tn
    return pl.pallas_call(
        _conv_in_kernel,
        grid=(T // tm, nb),
        in_specs=[
            pl.BlockSpec((tm, D), lambda i, j: (i, 0)),
            pl.BlockSpec((None, 1, D), lambda i, j: (layer, 0, 0)),
            _mod_spec(layer, 1, tm), _mod_spec(layer, 0, tm),
            pl.BlockSpec((None, D, tn), lambda i, j: (jl, 0, j)),
            pl.BlockSpec((None, D, tn), lambda i, j: (jl, 0, nb + j)),
            pl.BlockSpec((None, D, tn), lambda i, j: (jl, 0, 2 * nb + j)),
        ],
        out_specs=[pl.BlockSpec((tm, tn), lambda i, j: (i, j))] * 2,
        out_shape=[jax.ShapeDtypeStruct((T, D), f32)] * 2,
        scratch_shapes=[pltpu.VMEM((tm, D), bf16)],
        compiler_params=_params(("arbitrary", "arbitrary"),
                                _nbytes((tm, D), f32) + 3 * _nbytes((D, tn), bf16) + 2 * _nbytes((tm, tn), f32),
                                _nbytes((tm, D), bf16) + 3 * _nbytes((tm, tn), f32)),
        name="conv_in",
    )(x, norm.reshape(DEPTH, 1, D), mod, mod, w_conv_in, w_conv_in, w_conv_in)


def _ffn_up_kernel(x_ref, gain_ref, sc_ref, sh_ref, wg_ref, wu_ref, a_ref, h_scr):
    @pl.when(pl.program_id(1) == 0)
    def _():
        _norm_mod_to_scratch(x_ref, gain_ref, sc_ref, sh_ref, h_scr)

    h = h_scr[...]
    g = _mm(h, wg_ref[...])
    u = _mm(h, wu_ref[...])
    a_ref[...] = ((g * jax.nn.sigmoid(g)) * u).astype(bf16)


def _ffn_up(x, mod, norm, w_gate, w_up, layer):
    tm, tn = 1024, 512
    jl = layer // 2
    return pl.pallas_call(
        _ffn_up_kernel,
        grid=(T // tm, D_FF // tn),
        in_specs=[
            pl.BlockSpec((tm, D), lambda i, j: (i, 0)),
            pl.BlockSpec((None, 1, D), lambda i, j: (layer, 0, 0)),
            _mod_spec(layer, 4, tm), _mod_spec(layer, 3, tm),
            pl.BlockSpec((None, D, tn), lambda i, j: (jl, 0, j)),
            pl.BlockSpec((None, D, tn), lambda i, j: (jl, 0, j)),
        ],
        out_specs=pl.BlockSpec((tm, tn), lambda i, j: (i, j)),
        out_shape=jax.ShapeDtypeStruct((T, D_FF), bf16),
        scratch_shapes=[pltpu.VMEM((tm, D), bf16)],
        compiler_params=_params(("arbitrary", "arbitrary"),
                                _nbytes((tm, D), f32) + 2 * _nbytes((D, tn), bf16) + _nbytes((tm, tn), bf16),
                                _nbytes((tm, D), bf16) + 3 * _nbytes((tm, tn), f32)),
        name="ffn_up",
    )(x, norm.reshape(DEPTH, 1, D), mod, mod, w_gate, w_up)


def _qkv_kernel(x_ref, gain_ref, sc_ref, sh_ref, w_ref, qg_ref, kg_ref, o_ref, h_scr, *, n_q_blocks):
    j = pl.program_id(1)

    @pl.when(j == 0)
    def _():
        _norm_mod_to_scratch(x_ref, gain_ref, sc_ref, sh_ref, h_scr)

    acc = _mm(h_scr[...], w_ref[...])

    @pl.when(j >= 2 * n_q_blocks)
    def _():
        o_ref[...] = acc.astype(o_ref.dtype)

    @pl.when(j < 2 * n_q_blocks)
    def _():
        head_gain = jnp.where(j < n_q_blocks, qg_ref[...], kg_ref[...])
        for hh in range(acc.shape[1] // HEAD_DIM):
            a = acc[:, hh * HEAD_DIM:(hh + 1) * HEAD_DIM]
            ms = jnp.mean(a * a, axis=-1, keepdims=True)
            o_ref[:, hh * HEAD_DIM:(hh + 1) * HEAD_DIM] = ((a * lax.rsqrt(ms + EPS)) * head_gain).astype(o_ref.dtype)


def _qkv(x, mod, norm, w_qkv, q_norm, k_norm, layer, row0, n_rows, out_dtype):
    tm, tn = 1024, 1024
    jl = layer // 2
    tile0 = row0 // tm
    return pl.pallas_call(
        functools.partial(_qkv_kernel, n_q_blocks=D // tn),
        grid=(n_rows // tm, 3 * D // tn),
        in_specs=[
            pl.BlockSpec((tm, D), lambda i, j: (tile0 + i, 0)),
            pl.BlockSpec((None, 1, D), lambda i, j: (layer, 0, 0)),
            _mod_spec(layer, 1, tm, tile0=tile0), _mod_spec(layer, 0, tm, tile0=tile0),
            pl.BlockSpec((None, D, tn), lambda i, j: (jl, 0, j)),
            pl.BlockSpec((None, 1, HEAD_DIM), lambda i, j: (jl, 0, 0)),
            pl.BlockSpec((None, 1, HEAD_DIM), lambda i, j: (jl, 0, 0)),
        ],
        out_specs=pl.BlockSpec((tm, tn), lambda i, j: (i, j)),
        out_shape=jax.ShapeDtypeStruct((n_rows, 3 * D), out_dtype),
        scratch_shapes=[pltpu.VMEM((tm, D), bf16)],
        compiler_params=_params(("arbitrary", "arbitrary"),
                                _nbytes((tm, D), f32) + _nbytes((D, tn), bf16) + _nbytes((tm, tn), out_dtype),
                                _nbytes((tm, D), bf16) + 2 * _nbytes((tm, tn), f32)),
        name="qkv",
    )(x, norm.reshape(DEPTH, 1, D), mod, mod, w_qkv,
      q_norm.reshape(-1, 1, HEAD_DIM), k_norm.reshape(-1, 1, HEAD_DIM))


def _mm_res_kernel(*refs, n_ctx_tiles):
    *a_refs, w_ref, x_ref, gate_ref, o_ref = refs

    def update(a_ref):
        o_ref[...] = x_ref[...] + gate_ref[...] * _mm(a_ref[...], w_ref[...])

    if len(a_refs) == 1:
        update(a_refs[0])
    else:
        i = pl.program_id(0)

        @pl.when(i < n_ctx_tiles)
        def _():
            update(a_refs[0])

        @pl.when(i >= n_ctx_tiles)
        def _():
            update(a_refs[1])


def _mm_residual(a_parts, w, x, mod, layer, which, tm, tn, name):
    k = a_parts[0].shape[1]
    jl = layer // 2
    n_ctx_tiles = CTX_ROWS // tm
    if len(a_parts) == 1:
        a_specs = [pl.BlockSpec((tm, k), lambda i, j: (i, 0))]
    else:
        a_specs = [pl.BlockSpec((tm, k), lambda i, j: (jnp.minimum(i, n_ctx_tiles - 1), 0)),
                   pl.BlockSpec((tm, k), lambda i, j: (jnp.maximum(i - n_ctx_tiles, 0), 0))]
    return pl.pallas_call(
        functools.partial(_mm_res_kernel, n_ctx_tiles=n_ctx_tiles),
        grid=(T // tm, D // tn),
        in_specs=a_specs + [
            pl.BlockSpec((None, k, tn), lambda i, j: (jl, 0, j)),
            pl.BlockSpec((tm, tn), lambda i, j: (i, j)),
            _mod_spec(layer, which, tm, tn),
        ],
        out_specs=pl.BlockSpec((tm, tn), lambda i, j: (i, j)),
        out_shape=jax.ShapeDtypeStruct((T, D), f32),
        compiler_params=_params(("arbitrary", "arbitrary"),
                                len(a_parts) * _nbytes((tm, k), bf16) + _nbytes((k, tn), bf16)
                                + 2 * _nbytes((tm, tn), f32),
                                2 * _nbytes((tm, tn), f32)),
        name=name,
    )(*a_parts, w, x, mod)


def _conv_gate_kernel(b_ref, u_ref, up_ref, un_ref, cw_ref, o_ref, *, tm, cols):
    row0 = pl.program_id(0) * tm
    seq_len = jnp.where(row0 < CTX_ROWS, CTX_LEN, DEC_LEN)
    local = lax.broadcasted_iota(jnp.int32, (tm, cols), 0)
    pos = (row0 + local) & (seq_len - 1)
    is_first = pos == 0
    is_last = pos == seq_len - 1
    for c in range(D // cols):
        sl = slice(c * cols, (c + 1) * cols)
        u = u_ref[:, sl]
        prev_row = up_ref[SUBLANES - 1:SUBLANES, sl]
        next_row = un_ref[0:1, sl]
        u_prev = jnp.where(local == 0, prev_row, pltpu.roll(u, 1, axis=0))
        u_next = jnp.where(local == tm - 1, next_row, pltpu.roll(u, tm - 1, axis=0))
        u_prev = jnp.where(is_first, 0.0, u_prev)
        u_next = jnp.where(is_last, 0.0, u_next)
        conv = u_prev * cw_ref[0:1, sl] + u * cw_ref[1:2, sl] + u_next * cw_ref[2:3, sl]
        o_ref[:, sl] = (b_ref[:, sl] * conv).astype(bf16)


def _conv_gate(b, u, conv_w, layer):
    tm, cols = 512, 512
    jl = layer // 2
    blocks_per_tile = tm // SUBLANES
    return pl.pallas_call(
        functools.partial(_conv_gate_kernel, tm=tm, cols=cols),
        grid=(T // tm,),
        in_specs=[
            pl.BlockSpec((tm, D), lambda i: (i, 0)),
            pl.BlockSpec((tm, D), lambda i: (i, 0)),
            pl.BlockSpec((SUBLANES, D), lambda i: (jnp.maximum(i * blocks_per_tile - 1, 0), 0)),
            pl.BlockSpec((SUBLANES, D), lambda i: (jnp.minimum((i + 1) * blocks_per_tile, T // SUBLANES - 1), 0)),
            pl.BlockSpec((None, 3, D), lambda i: (jl, 0, 0)),
        ],
        out_specs=pl.BlockSpec((tm, D), lambda i: (i, 0)),
        out_shape=jax.ShapeDtypeStruct((T, D), bf16),
        compiler_params=_params(("arbitrary",),
                                2 * _nbytes((tm, D), f32) + _nbytes((tm, D), bf16) + 2 * _nbytes((SUBLANES, D), f32),
                                8 * _nbytes((tm, cols), f32)),
        name="conv_gate",
    )(b, u, u, u, conv_w)


def _softmax_pv(parts):
    m = parts[0][0].max(axis=-1, keepdims=True)
    for s, _ in parts[1:]:
        m = jnp.maximum(m, s.max(axis=-1, keepdims=True))
    den = None
    out = None
    for s, v in parts:
        p = jnp.exp2(s - m)
        d = p.sum(axis=-1, keepdims=True)
        o = _mm(p.astype(bf16), v)
        den = d if den is None else den + d
        out = o if out is None else out + o
    return out * (1.0 / den)


def _ctx_attn_kernel(q_ref, k_ref, v_ref, o_ref, k_out_ref, v_out_ref):
    k_out_ref[...] = k_ref[...]
    v_out_ref[...] = v_ref[...]
    for hh in range(q_ref.shape[1] // HEAD_DIM):
        sl = slice(hh * HEAD_DIM, (hh + 1) * HEAD_DIM)
        q = q_ref[:, sl].astype(bf16)
        k = k_ref[:, sl].astype(bf16)
        v = v_ref[:, sl].astype(bf16)
        s = _mm_nt(q, k) * SCORE_SCALE_LOG2
        o_ref[:, sl] = _softmax_pv([(s, v)]).astype(bf16)


def _ctx_attention(qkv):
    tn = 512
    nb = D // tn
    return pl.pallas_call(
        _ctx_attn_kernel,
        grid=(N_CTX_SEQ, nb),
        in_specs=[
            pl.BlockSpec((CTX_LEN, tn), lambda b, g: (b, g)),
            pl.BlockSpec((CTX_LEN, tn), lambda b, g: (b, nb + g)),
            pl.BlockSpec((CTX_LEN, tn), lambda b, g: (b, 2 * nb + g)),
        ],
        out_specs=[pl.BlockSpec((CTX_LEN, tn), lambda b, g: (b, g))] * 3,
        out_shape=[jax.ShapeDtypeStruct((CTX_ROWS, D), bf16),
                   jax.ShapeDtypeStruct((CTX_ROWS, D), f32),
                   jax.ShapeDtypeStruct((CTX_ROWS, D), f32)],
        compiler_params=_params(("arbitrary", "arbitrary"),
                                5 * _nbytes((CTX_LEN, tn), f32) + _nbytes((CTX_LEN, tn), bf16),
                                8 * _nbytes((CTX_LEN, CTX_LEN), f32)),
        name="ctx_attention",
    )(qkv, qkv, qkv)


def _band_start_row(m):
    return jnp.clip(Q_ROWS_PER_BLOCK * m - WIN_ROWS // 2, 0, GRID_ROWS - BAND_ROWS)


def _band_block_ids(m):
    band0 = _band_start_row(m)
    ids = []
    for rho in range(Q_ROWS_PER_BLOCK):
        r = Q_ROWS_PER_BLOCK * m + rho
        win0 = jnp.clip(r - WIN_ROWS // 2, 0, GRID_ROWS - WIN_ROWS)
        row_ids = []
        for kap in range(BAND_ROWS):
            kr = band0 + kap
            inside = (kr >= win0) & (kr < win0 + WIN_ROWS)
            row_ids.append(jnp.where(inside, kr - r + WIN_ROWS - 1, MASKED_BLOCK))
        ids.append(row_ids)
    return ids


def _band_bias(bias_ref, hh, ids):
    left_half = lax.broadcasted_iota(jnp.int32, (GRID_W, 2 * GRID_W), 1) < GRID_W
    rows = []
    for rho in range(Q_ROWS_PER_BLOCK):
        pairs = [jnp.where(left_half, bias_ref[hh, ids[rho][kap]], bias_ref[hh, ids[rho][kap + 1]])
                 for kap in range(0, BAND_ROWS, 2)]
        rows.append(jnp.concatenate(pairs, axis=1))
    return jnp.concatenate(rows, axis=0)


def _loc_attn_kernel(q_ref, k_ref, v_ref, ck_ref, cv_ref, bias_ref, o_ref):
    start = pl.multiple_of(_band_start_row(pl.program_id(2)) * GRID_W, GRID_W)
    ids = _band_block_ids(pl.program_id(2))
    for hh in range(q_ref.shape[1] // HEAD_DIM):
        sl = slice(hh * HEAD_DIM, (hh + 1) * HEAD_DIM)
        q = q_ref[:, sl]
        k_band = k_ref[pl.ds(start, BAND), sl]
        v_band = v_ref[pl.ds(start, BAND), sl]
        k_ctx = ck_ref[:, sl].astype(bf16)
        v_ctx = cv_ref[:, sl].astype(bf16)
        s_loc = _mm_nt(q, k_band) * SCORE_SCALE_LOG2 + _band_bias(bias_ref, hh, ids)
        s_ctx = _mm_nt(q, k_ctx) * SCORE_SCALE_LOG2
        o_ref[:, sl] = _softmax_pv([(s_loc, v_band), (s_ctx, v_ctx)]).astype(bf16)


def _rel_bias_blocks(rpb_l):
    cols = np.arange(GRID_W)
    col_start = np.clip(cols - WIN_COLS // 2, 0, GRID_W - WIN_COLS)
    col_ok = (cols[None, :] >= col_start[:, None]) & (cols[None, :] < col_start[:, None] + WIN_COLS)
    pad = GRID_W - WIN_COLS
    padded = jnp.pad(rpb_l.astype(f32), ((0, 0), (0, 0), (pad, pad)))
    skewed = jnp.tile(padded, (1, 1, GRID_W + 1))[:, :, :GRID_W * 2 * GRID_W]
    skewed = skewed.reshape(N_HEADS, MASKED_BLOCK, GRID_W, 2 * GRID_W)[..., :GRID_W]
    toeplitz = jnp.flip(skewed, axis=2)
    toeplitz = jnp.where(col_ok[None, None], toeplitz * LOG2_E, NEG_INF)
    masked = jnp.full((N_HEADS, 1, GRID_W, GRID_W), NEG_INF, f32)
    blocks = jnp.concatenate([toeplitz, masked], axis=1)
    return jnp.concatenate([blocks, blocks], axis=3)


def _latent_attention(qkv, cache_k, cache_v, layer_slot, bias_blocks):
    tn = 512
    nb = D // tn
    hb = tn // HEAD_DIM
    return pl.pallas_call(
        _loc_attn_kernel,
        grid=(N_DEC_SEQ, nb, N_Q_BLOCKS),
        in_specs=[
            pl.BlockSpec((Q_BLOCK, tn), lambda b, g, m: (b * N_Q_BLOCKS + m, g)),
            pl.BlockSpec((DEC_LEN, tn), lambda b, g, m: (b, nb + g)),
            pl.BlockSpec((DEC_LEN, tn), lambda b, g, m: (b, 2 * nb + g)),
            pl.BlockSpec((None, None, PAST_LEN, tn), lambda b, g, m: (b, layer_slot, 0, g)),
            pl.BlockSpec((None, None, PAST_LEN, tn), lambda b, g, m: (b, layer_slot, 0, g)),
            pl.BlockSpec((hb, N_BIAS_BLOCKS, GRID_W, 2 * GRID_W), lambda b, g, m: (g, 0, 0, 0)),
        ],
        out_specs=pl.BlockSpec((Q_BLOCK, tn), lambda b, g, m: (b * N_Q_BLOCKS + m, g)),
        out_shape=jax.ShapeDtypeStruct((DEC_ROWS, D), bf16),
        compiler_params=_params(("arbitrary", "arbitrary", "arbitrary"),
                                _nbytes((Q_BLOCK, tn), bf16) + 2 * _nbytes((DEC_LEN, tn), bf16)
                                + 2 * _nbytes((PAST_LEN, tn), f32)
                                + _nbytes((hb, N_BIAS_BLOCKS, GRID_W, 2 * GRID_W), f32)
                                + _nbytes((Q_BLOCK, tn), bf16),
                                10 * _nbytes((Q_BLOCK, BAND), f32)),
        name="latent_attention",
    )(qkv, qkv, qkv, cache_k, cache_v, bias_blocks)


HI_HALF_MASK = 0xFFFF0000


def _pack_bf16_halves(h):
    bits = lax.bitcast_convert_type(h.astype(f32), jnp.uint32)
    half = h.shape[1] // 2
    return (bits[:, :half] >> 16) | (bits[:, half:] & jnp.uint32(HI_HALF_MASK))


def _unpack_bf16_halves(words):
    lo = lax.bitcast_convert_type(words << 16, f32).astype(bf16)
    hi = lax.bitcast_convert_type(words & jnp.uint32(HI_HALF_MASK), f32).astype(bf16)
    return lo, hi


def _router_kernel(x_ref, gain_ref, sc_ref, sh_ref, wr_ref, hp_ref, idx_ref, wt_ref, rank_ref, cnt_ref, seen):
    tm = x_ref.shape[0]

    @pl.when(pl.program_id(0) == 0)
    def _():
        seen[...] = jnp.zeros(seen.shape, seen.dtype)

    h = _norm_mod_rows(x_ref[...], gain_ref[...], 1.0 + sc_ref[...], sh_ref[...]).astype(bf16)
    hp_ref[...] = _pack_bf16_halves(h)
    logits = _mm(h, wr_ref[...].astype(bf16))
    lane = lax.broadcasted_iota(jnp.int32, (tm, LANES), 1)
    l1 = jnp.where(lane < N_EXPERTS, logits, -jnp.inf)
    m1 = l1.max(axis=-1, keepdims=True)
    i1 = jnp.where(l1 == m1, lane, LANES).min(axis=-1, keepdims=True)
    l2 = jnp.where(lane == i1, -jnp.inf, l1)
    m2 = l2.max(axis=-1, keepdims=True)
    i2 = jnp.where(l2 == m2, lane, LANES).min(axis=-1, keepdims=True)
    e = jnp.exp(m2 - m1)
    w1 = 1.0 / (1.0 + e)
    w2 = e / (1.0 + e)
    idx_ref[...] = jnp.where(lane == 0, i1, jnp.where(lane == 1, i2, 0))
    wt_ref[...] = jnp.where(lane == 0, w1, jnp.where(lane == 1, w2, 0.0))

    first = lane == i1
    second = lane == i2
    chosen = (first | second).astype(bf16)
    earlier = (lax.broadcasted_iota(jnp.int32, (tm, tm), 1) < lax.broadcasted_iota(jnp.int32, (tm, tm), 0))
    before = _mm(earlier.astype(bf16), chosen) + seen[...]
    r1 = jnp.where(first, before, 0.0).sum(axis=-1, keepdims=True)
    r2 = jnp.where(second, before, 0.0).sum(axis=-1, keepdims=True)
    rank_ref[...] = jnp.where(lane == 0, r1, jnp.where(lane == 1, r2, 0.0)).astype(jnp.int32)
    seen[...] += chosen.astype(f32).sum(axis=0, keepdims=True)
    cnt_ref[...] = seen[...].astype(jnp.int32)


def _router(x, mod, norm, w_router_padded, layer):
    tm = 512
    jl = layer // 2
    lane_block = pl.BlockSpec((tm, LANES), lambda i: (i, 0))
    return pl.pallas_call(
        _router_kernel,
        grid=(T // tm,),
        in_specs=[
            pl.BlockSpec((tm, D), lambda i: (i, 0)),
            pl.BlockSpec((None, 1, D), lambda i: (layer, 0, 0)),
            _mod_spec(layer, 4, tm), _mod_spec(layer, 3, tm),
            pl.BlockSpec((None, D, LANES), lambda i: (jl, 0, 0)),
        ],
        out_specs=[pl.BlockSpec((tm, D // 2), lambda i: (i, 0)), lane_block, lane_block, lane_block,
                   pl.BlockSpec((1, LANES), lambda i: (0, 0))],
        out_shape=[jax.ShapeDtypeStruct((T, D // 2), jnp.uint32),
                   jax.ShapeDtypeStruct((T, LANES), jnp.int32),
                   jax.ShapeDtypeStruct((T, LANES), f32),
                   jax.ShapeDtypeStruct((T, LANES), jnp.int32),
                   jax.ShapeDtypeStruct((1, LANES), jnp.int32)],
        scratch_shapes=[pltpu.VMEM((1, LANES), f32)],
        compiler_params=_params(("arbitrary",),
                                _nbytes((tm, D), f32) + _nbytes((tm, D), bf16) + 3 * _nbytes((tm, LANES), f32),
                                _nbytes((D, LANES), f32) + 4 * _nbytes((tm, D), f32) + 2 * _nbytes((tm, tm), f32)),
        name="router",
    )(x, norm.reshape(DEPTH, 1, D), mod, mod, w_router_padded)


def _routing_plan(expert_ids, ranks, counts):
    e_flat = expert_ids.reshape(-1)
    padded = ((counts + MOE_TM - 1) // MOE_TM) * MOE_TM
    group_end = jnp.cumsum(padded)
    group_start = group_end - padded
    onehot = e_flat[:, None] == jnp.arange(N_EXPERTS, dtype=jnp.int32)[None, :]
    dest = ranks.reshape(-1) + jnp.sum(jnp.where(onehot, group_start[None, :], 0), axis=1)
    src_token = jnp.zeros((MOE_ROWS,), jnp.int32).at[dest].set(
        jnp.arange(N_ASSIGN, dtype=jnp.int32) // TOP_K, unique_indices=True)
    n_tiles = (group_end[-1] // MOE_TM).astype(jnp.int32)
    tile_row0 = jnp.arange(MOE_TILES, dtype=jnp.int32) * MOE_TM
    tile_expert = jnp.sum((tile_row0[:, None] >= group_end[None, :]).astype(jnp.int32), axis=1)
    tile_expert = jnp.minimum(tile_expert, N_EXPERTS - 1)
    last_used = tile_expert[jnp.maximum(n_tiles - 1, 0)]
    tile_expert = jnp.where(jnp.arange(MOE_TILES) < n_tiles, tile_expert, last_used).astype(jnp.int32)
    return src_token, dest.astype(jnp.int32), tile_expert, n_tiles.reshape(1)


def _dispatch_kernel(src_ref, nt_ref, hp_hbm, o_ref, buf, sem):
    i = pl.program_id(0)
    n_tiles = nt_ref[0]
    chunk = o_ref.shape[0]

    def tile_copies(tile, slot):
        return [pltpu.make_async_copy(hp_hbm.at[pl.ds(src_ref[tile * chunk + r], 1), :],
                                      buf.at[slot, pl.ds(r, 1), :], sem.at[slot]) for r in range(chunk)]

    def start_tile(tile, slot):
        for cp in tile_copies(tile, slot):
            cp.start()

    @pl.when((i == 0) & (n_tiles > 0))
    def _():
        start_tile(0, 0)

    @pl.when(i + 1 < n_tiles)
    def _():
        start_tile(i + 1, (i + 1) % 2)

    @pl.when(i < n_tiles)
    def _():
        slot = i % 2
        for cp in tile_copies(i, slot):
            cp.wait()
        lo, hi = _unpack_bf16_halves(buf[slot])
        o_ref[:, :D // 2] = lo
        o_ref[:, D // 2:] = hi

    @pl.when(i >= n_tiles)
    def _():
        o_ref[...] = jnp.zeros(o_ref.shape, o_ref.dtype)


def _dispatch(h_packed, src_token, n_tiles):
    chunk = MOE_TM
    return pl.pallas_call(
        _dispatch_kernel,
        grid_spec=pltpu.PrefetchScalarGridSpec(
            num_scalar_prefetch=2,
            grid=(MOE_ROWS // chunk,),
            in_specs=[pl.BlockSpec(memory_space=pl.ANY)],
            out_specs=pl.BlockSpec((chunk, D), lambda i, src, nt: (i, 0)),
            scratch_shapes=[pltpu.VMEM((2, chunk, D // 2), jnp.uint32), pltpu.SemaphoreType.DMA((2,))],
        ),
        out_shape=jax.ShapeDtypeStruct((MOE_ROWS, D), bf16),
        compiler_params=_params(("arbitrary",), _nbytes((chunk, D), bf16), 5 * _nbytes((chunk, D // 2), f32)),
        name="moe_dispatch",
    )(src_token, n_tiles, h_packed)


def _expert_changed(te_ref, i):
    return (i == 0) | (te_ref[i] != te_ref[jnp.maximum(i - 1, 0)])


def _moe_up_kernel(te_ref, nt_ref, hs_ref, wg_ref, wu_ref, a_ref, wg_s, wu_s):
    i = pl.program_id(1)

    @pl.when(_expert_changed(te_ref, i))
    def _():
        wg_s[...] = wg_ref[...].astype(bf16)
        wu_s[...] = wu_ref[...].astype(bf16)

    @pl.when(i < nt_ref[0])
    def _():
        h = hs_ref[...]
        g = _mm(h, wg_s[...])
        u = _mm(h, wu_s[...])
        a_ref[...] = ((g * jax.nn.sigmoid(g)) * u).astype(bf16)

    @pl.when(i >= nt_ref[0])
    def _():
        a_ref[...] = jnp.zeros(a_ref.shape, a_ref.dtype)


def _moe_up(hs, w_gate, w_up, tile_expert, n_tiles, layer):
    tm, tn = MOE_TM, 512
    jl = layer // 2
    return pl.pallas_call(
        _moe_up_kernel,
        grid_spec=pltpu.PrefetchScalarGridSpec(
            num_scalar_prefetch=2,
            grid=(D_FF // tn, MOE_TILES),
            in_specs=[
                pl.BlockSpec((tm, D), lambda j, i, te, nt: (i, 0)),
                pl.BlockSpec((None, None, D, tn), lambda j, i, te, nt: (jl, te[i], 0, j)),
                pl.BlockSpec((None, None, D, tn), lambda j, i, te, nt: (jl, te[i], 0, j)),
            ],
            out_specs=pl.BlockSpec((tm, tn), lambda j, i, te, nt: (i, j)),
            scratch_shapes=[pltpu.VMEM((D, tn), bf16), pltpu.VMEM((D, tn), bf16)],
        ),
        out_shape=jax.ShapeDtypeStruct((MOE_ROWS, D_FF), bf16),
        compiler_params=_params(("arbitrary", "arbitrary"),
                                _nbytes((tm, D), bf16) + 2 * _nbytes((D, tn), f32) + _nbytes((tm, tn), bf16),
                                2 * _nbytes((D, tn), bf16) + 3 * _nbytes((tm, tn), f32)),
        name="moe_up",
    )(tile_expert, n_tiles, hs, w_gate, w_up)


def _moe_down_kernel(te_ref, nt_ref, a_ref, wd_ref, y_ref, wd_s):
    i = pl.program_id(1)

    @pl.when(_expert_changed(te_ref, i))
    def _():
        wd_s[...] = wd_ref[...].astype(bf16)

    @pl.when(i < nt_ref[0])
    def _():
        y_ref[...] = _mm(a_ref[...], wd_s[...])

    @pl.when(i >= nt_ref[0])
    def _():
        y_ref[...] = jnp.zeros(y_ref.shape, y_ref.dtype)


def _moe_down(a, w_down, tile_expert, n_tiles, layer):
    tm, tn = MOE_TM, 512
    jl = layer // 2
    return pl.pallas_call(
        _moe_down_kernel,
        grid_spec=pltpu.PrefetchScalarGridSpec(
            num_scalar_prefetch=2,
            grid=(D // tn, MOE_TILES),
            in_specs=[
                pl.BlockSpec((tm, D_FF), lambda j, i, te, nt: (i, 0)),
                pl.BlockSpec((None, None, D_FF, tn), lambda j, i, te, nt: (jl, te[i], 0, j)),
            ],
            out_specs=pl.BlockSpec((tm, tn), lambda j, i, te, nt: (i, j)),
            scratch_shapes=[pltpu.VMEM((D_FF, tn), bf16)],
        ),
        out_shape=jax.ShapeDtypeStruct((MOE_ROWS, D), f32),
        compiler_params=_params(("arbitrary", "arbitrary"),
                                _nbytes((tm, D_FF), bf16) + _nbytes((D_FF, tn), f32) + _nbytes((tm, tn), f32),
                                _nbytes((D_FF, tn), bf16) + _nbytes((tm, tn), f32)),
        name="moe_down",
    )(tile_expert, n_tiles, a, w_down)


def _combine_kernel(pos_ref, x_ref, gate_ref, wt_ref, y_hbm, o_ref, ybuf, sem, *, tm):
    i = pl.program_id(0)
    n_steps = pl.num_programs(0)

    def tile_copies(tile, slot):
        return [pltpu.make_async_copy(y_hbm.at[pl.ds(pos_ref[(tile * tm + t) * TOP_K + c], 1), :],
                                      ybuf.at[slot, c, pl.ds(t, 1), :], sem.at[slot])
                for t in range(tm) for c in range(TOP_K)]

    def start_tile(tile, slot):
        for cp in tile_copies(tile, slot):
            cp.start()

    @pl.when(i == 0)
    def _():
        start_tile(0, 0)

    @pl.when(i + 1 < n_steps)
    def _():
        start_tile(i + 1, (i + 1) % 2)

    slot = i % 2
    for cp in tile_copies(i, slot):
        cp.wait()
    w1 = jnp.broadcast_to(wt_ref[:, 0:1], (tm, LANES))
    w2 = jnp.broadcast_to(wt_ref[:, 1:2], (tm, LANES))
    for s in range(ROW_SLABS):
        sl = slice(s * LANES, (s + 1) * LANES)
        mix = w1 * ybuf[slot, 0, :, sl] + w2 * ybuf[slot, 1, :, sl]
        o_ref[:, sl] = x_ref[:, sl] + gate_ref[:, sl] * mix


def _combine(x, y_slabs, pos, wts, mod, layer):
    tm = 256
    return pl.pallas_call(
        functools.partial(_combine_kernel, tm=tm),
        grid_spec=pltpu.PrefetchScalarGridSpec(
            num_scalar_prefetch=1,
            grid=(T // tm,),
            in_specs=[
                pl.BlockSpec((tm, D), lambda i, pos: (i, 0)),
                _mod_spec(layer, 5, tm),
                pl.BlockSpec((tm, LANES), lambda i, pos: (i, 0)),
                pl.BlockSpec(memory_space=pl.ANY),
            ],
            out_specs=pl.BlockSpec((tm, D), lambda i, pos: (i, 0)),
            scratch_shapes=[pltpu.VMEM((2, TOP_K, tm, D), f32),
                            pltpu.SemaphoreType.DMA((2,))],
        ),
        out_shape=jax.ShapeDtypeStruct((T, D), f32),
        compiler_params=_params(("arbitrary",),
                                2 * _nbytes((tm, D), f32) + _nbytes((tm, LANES), f32),
                                2 * _nbytes((TOP_K, tm, D), f32) + 4 * _nbytes((tm, LANES), f32)),
        name="moe_combine",
    )(pos, x, mod, wts, y_slabs)


def _moe_layer(x, mod, norm2, w_router, w_gate, w_up, w_down, layer):
    wr = jnp.pad(w_router, ((0, 0), (0, 0), (0, LANES - N_EXPERTS)))
    h_packed, idx, wts, ranks, counts = _router(x, mod, norm2, wr, layer)
    src_token, pos, tile_expert, n_tiles = _routing_plan(idx[:, :TOP_K], ranks[:, :TOP_K], counts[0, :N_EXPERTS])
    hs = _dispatch(h_packed, src_token, n_tiles)
    a = _moe_up(hs, w_gate, w_up, tile_expert, n_tiles, layer)
    y = _moe_down(a, w_down, tile_expert, n_tiles, layer)
    return _combine(x, y, pos, wts, mod, layer)


def kernel(x_prompt, x_sample, cache_k, cache_v, c, c_ctx, norm1, norm2, w_ada, b_ada, w_conv_in, conv_w,
           w_conv_out, w_qkv, q_norm, k_norm, rpb, w_attn_out, w_ffn_gate, w_ffn_up, w_ffn_down, w_router,
           w_moe_gate, w_moe_up, w_moe_down):
    x = jnp.concatenate([x_prompt.reshape(CTX_ROWS, D), x_sample.reshape(DEC_ROWS, D)], axis=0)
    cond = jnp.concatenate([c_ctx[None, :], c, jnp.zeros((N_GROUPS_PADDED - 1 - N_DEC_SEQ, D), f32)], axis=0)
    mod = _ada_params(cond, w_ada, b_ada)
    cache_k2 = cache_k.reshape(N_DEC_SEQ, -1, PAST_LEN, D)
    cache_v2 = cache_v.reshape(N_DEC_SEQ, -1, PAST_LEN, D)
    w_conv_in, w_conv_out, w_qkv, w_attn_out, w_ffn_gate, w_ffn_up, w_ffn_down = (
        w.astype(bf16) for w in (w_conv_in, w_conv_out, w_qkv, w_attn_out, w_ffn_gate, w_ffn_up, w_ffn_down))

    new_k, new_v = [], []
    for layer in range(DEPTH):
        jl = layer // 2
        if layer % 2 == 0:
            b, u = _conv_in(x, mod, norm1, w_conv_in, layer)
            gated = _conv_gate(b, u, conv_w, layer)
            x = _mm_residual([gated], w_conv_out, x, mod, layer, 2, 1024, 1024, "conv_out")
            a = _ffn_up(x, mod, norm2, w_ffn_gate, w_ffn_up, layer)
            x = _mm_residual([a], w_ffn_down, x, mod, layer, 5, 1024, 512, "ffn_down")
        else:
            qkv_ctx = _qkv(x, mod, norm1, w_qkv, q_norm, k_norm, layer, 0, CTX_ROWS, f32)
            qkv_dec = _qkv(x, mod, norm1, w_qkv, q_norm, k_norm, layer, CTX_ROWS, DEC_ROWS, bf16)
            o_ctx, k_ctx, v_ctx = _ctx_attention(qkv_ctx)
            new_k.append(k_ctx.reshape(N_CTX_SEQ, CTX_LEN, N_HEADS, HEAD_DIM))
            new_v.append(v_ctx.reshape(N_CTX_SEQ, CTX_LEN, N_HEADS, HEAD_DIM))
            o_dec = _latent_attention(qkv_dec, cache_k2, cache_v2, jl, _rel_bias_blocks(rpb[jl]))
            x = _mm_residual([o_ctx, o_dec], w_attn_out, x, mod, layer, 2, 1024, 1024, "attn_out")
            x = _moe_layer(x, mod, norm2, w_router, w_moe_gate, w_moe_up, w_moe_down, layer)

    y_prompt = x[:CTX_ROWS].reshape(N_CTX_SEQ, CTX_LEN, D)
    y_sample = x[CTX_ROWS:].reshape(N_DEC_SEQ, DEC_LEN, D)
    return y_prompt, y_sample, jnp.stack(new_k, axis=1), jnp.stack(new_v, axis=1)
```

```python
import functools

import numpy as np
import jax
import jax.numpy as jnp
from jax import lax
from jax.experimental import pallas as pl
from jax.experimental.pallas import tpu as pltpu

D = 2048
N_CTX_SEQ = 16
CTX_LEN = 256
N_DEC_SEQ = 4
DEC_LEN = 2048
CTX_ROWS = N_CTX_SEQ * CTX_LEN
DEC_ROWS = N_DEC_SEQ * DEC_LEN
T = CTX_ROWS + DEC_ROWS
DEPTH = 4
PAST_LEN = 512
N_HEADS = 16
HEAD_DIM = 128
GRID_W = 64
GRID_ROWS = DEC_LEN // GRID_W
WIN_ROWS = 8
WIN_COLS = 16
D_FF = 5632
N_EXPERTS = 8
TOP_K = 2
EPS = 1e-6
NEG_INF = -1e30
ATTN_SCALE = HEAD_DIM ** -0.5
LOG2_E = 1.4426950408889634
SCORE_SCALE_LOG2 = ATTN_SCALE * LOG2_E

N_GROUPS_PADDED = 8
LANES = 128
SUBLANES = 8
MXU_COLS = 256
VMEM_CAP_BYTES = 56 * 2 ** 20
VMEM_HEADROOM = 4 * 2 ** 20

Q_ROWS_PER_BLOCK = 4
Q_BLOCK = Q_ROWS_PER_BLOCK * GRID_W
N_Q_BLOCKS = GRID_ROWS // Q_ROWS_PER_BLOCK
BAND_ROWS = Q_ROWS_PER_BLOCK + WIN_ROWS
BAND = BAND_ROWS * GRID_W
MASKED_BLOCK = 2 * WIN_ROWS - 1
N_BIAS_BLOCKS = MASKED_BLOCK + 1

MOE_TM = 512
N_ASSIGN = T * TOP_K
MOE_ROWS = N_ASSIGN + N_EXPERTS * MOE_TM
MOE_TILES = MOE_ROWS // MOE_TM
ROW_SLABS = D // LANES

bf16 = jnp.bfloat16
f32 = jnp.float32


def _mm(a, b):
    return jnp.dot(a, b, preferred_element_type=f32)


def _mm_nt(a, b):
    return lax.dot_general(a, b, (((1,), (1,)), ((), ())), preferred_element_type=f32)


def _nbytes(shape, dtype):
    return int(np.prod(shape)) * jnp.dtype(dtype).itemsize


def _params(semantics, pipelined_bytes, resident_bytes):
    need = 2 * pipelined_bytes + resident_bytes + VMEM_HEADROOM
    return pltpu.CompilerParams(dimension_semantics=semantics,
                                vmem_limit_bytes=int(min(max(need, 16 * 2 ** 20), VMEM_CAP_BYTES)))


def _group_of_tile(i, tm):
    n_ctx = CTX_ROWS // tm
    per_seq = DEC_LEN // tm
    return jnp.where(i < n_ctx, 0, 1 + (i - n_ctx) // per_seq)


def _mod_spec(layer, which, tm, width=D, tile0=0):
    def index(*grid_and_prefetch):
        i = tile0 + grid_and_prefetch[0]
        j = grid_and_prefetch[1] if width != D else 0
        return ((layer * N_GROUPS_PADDED + _group_of_tile(i, tm)) * 6 + which, 0, j)
    return pl.BlockSpec((None, 1, width), index)


def _norm_mod_rows(x, gain, scale1p, shift):
    ms = jnp.mean(x * x, axis=-1, keepdims=True)
    y = x * lax.rsqrt(ms + EPS)
    return (y * gain) * scale1p + shift


def _norm_mod_to_scratch(x_ref, gain_ref, sc_ref, sh_ref, h_scr, rows=256):
    gain = gain_ref[...]
    scale1p = 1.0 + sc_ref[...]
    shift = sh_ref[...]

    def body(c, carry):
        r = pl.multiple_of(c * rows, rows)
        h = _norm_mod_rows(x_ref[pl.ds(r, rows), :], gain, scale1p, shift)
        h_scr[pl.ds(r, rows), :] = h.astype(bf16)
        return carry

    lax.fori_loop(0, x_ref.shape[0] // rows, body, 0)


def _ada_kernel(cond_ref, w_ref, b_ref, o_ref):
    c = cond_ref[...]
    s = (c * jax.nn.sigmoid(c)).astype(bf16)
    o_ref[...] = _mm(s, w_ref[...].astype(bf16)) + b_ref[...]


def _ada_params(cond, w_ada, b_ada):
    tn = 1024
    out = pl.pallas_call(
        _ada_kernel,
        grid=(DEPTH, 6 * D // tn),
        in_specs=[
            pl.BlockSpec((N_GROUPS_PADDED, D), lambda l, j: (0, 0)),
            pl.BlockSpec((None, D, tn), lambda l, j: (l, 0, j)),
            pl.BlockSpec((None, 1, tn), lambda l, j: (l, 0, j)),
        ],
        out_specs=pl.BlockSpec((None, N_GROUPS_PADDED, tn), lambda l, j: (l, 0, j)),
        out_shape=jax.ShapeDtypeStruct((DEPTH, N_GROUPS_PADDED, 6 * D), f32),
        compiler_params=_params(("arbitrary", "arbitrary"),
                                _nbytes((D, tn), f32) + _nbytes((8, tn), f32) * 2,
                                _nbytes((D, tn), bf16) + _nbytes((8, D), f32) * 2),
        name="ada_params",
    )(cond, w_ada, b_ada.reshape(DEPTH, 1, 6 * D))
    return out.reshape(DEPTH * N_GROUPS_PADDED * 6, 1, D)


def _conv_in_kernel(x_ref, gain_ref, sc_ref, sh_ref, wb_ref, wc_ref, wv_ref, b_ref, u_ref, h_scr):
    @pl.when(pl.program_id(1) == 0)
    def _():
        _norm_mod_to_scratch(x_ref, gain_ref, sc_ref, sh_ref, h_scr)

    h = h_scr[...]
    b_ref[...] = _mm(h, wb_ref[...])
    u_ref[...] = _mm(h, wc_ref[...]) * _mm(h, wv_ref[...])


def _conv_in(x, mod, norm, w_conv_in, layer):
    tm, tn = 1024, 256
    jl = layer // 2
    nb = D // tn
    return pl.pallas_call(
        _conv_in_kernel,
        grid=(T // tm, nb),
        in_specs=[
            pl.BlockSpec((tm, D), lambda i, j: (i, 0)),
            pl.BlockSpec((None, 1, D), lambda i, j: (layer, 0, 0)),
            _mod_spec(layer, 1, tm), _mod_spec(layer, 0, tm),
            pl.BlockSpec((None, D, tn), lambda i, j: (jl, 0, j)),
            pl.BlockSpec((None, D, tn), lambda i, j: (jl, 0, nb + j)),
            pl.BlockSpec((None, D, tn), lambda i, j: (jl, 0, 2 * nb + j)),
        ],
        out_specs=[pl.BlockSpec((tm, tn), lambda i, j: (i, j))] * 2,
        out_shape=[jax.ShapeDtypeStruct((T, D), f32)] * 2,
        scratch_shapes=[pltpu.VMEM((tm, D), bf16)],
        compiler_params=_params(("arbitrary", "arbitrary"),
                                _nbytes((tm, D), f32) + 3 * _nbytes((D, tn), bf16) + 2 * _nbytes((tm, tn), f32),
                                _nbytes((tm, D), bf16) + 3 * _nbytes((tm, tn), f32)),
        name="conv_in",
    )(x, norm.reshape(DEPTH, 1, D), mod, mod, w_conv_in, w_conv_in, w_conv_in)


def _ffn_up_kernel(x_ref, gain_ref, sc_ref, sh_ref, wg_ref, wu_ref, a_ref, h_scr):
    @pl.when(pl.program_id(1) == 0)
    def _():
        _norm_mod_to_scratch(x_ref, gain_ref, sc_ref, sh_ref, h_scr)

    h = h_scr[...]
    g = _mm(h, wg_ref[...])
    u = _mm(h, wu_ref[...])
    a_ref[...] = ((g * jax.nn.sigmoid(g)) * u).astype(bf16)


def _ffn_up(x, mod, norm, w_gate, w_up, layer):
    tm, tn = 1024, 512
    jl = layer // 2
    return pl.pallas_call(
        _ffn_up_kernel,
        grid=(T // tm, D_FF // tn),
        in_specs=[
            pl.BlockSpec((tm, D), lambda i, j: (i, 0)),
            pl.BlockSpec((None, 1, D), lambda i, j: (layer, 0, 0)),
            _mod_spec(layer, 4, tm), _mod_spec(layer, 3, tm),
            pl.BlockSpec((None, D, tn), lambda i, j: (jl, 0, j)),
            pl.BlockSpec((None, D, tn), lambda i, j: (jl, 0, j)),
        ],
        out_specs=pl.BlockSpec((tm, tn), lambda i, j: (i, j)),
        out_shape=jax.ShapeDtypeStruct((T, D_FF), bf16),
        scratch_shapes=[pltpu.VMEM((tm, D), bf16)],
        compiler_params=_params(("arbitrary", "arbitrary"),
                                _nbytes((tm, D), f32) + 2 * _nbytes((D, tn), bf16) + _nbytes((tm, tn), bf16),
                                _nbytes((tm, D), bf16) + 3 * _nbytes((tm, tn), f32)),
        name="ffn_up",
    )(x, norm.reshape(DEPTH, 1, D), mod, mod, w_gate, w_up)


def _qkv_kernel(x_ref, gain_ref, sc_ref, sh_ref, w_ref, qg_ref, kg_ref, o_ref, h_scr, *, n_q_blocks):
    j = pl.program_id(1)

    @pl.when(j == 0)
    def _():
        _norm_mod_to_scratch(x_ref, gain_ref, sc_ref, sh_ref, h_scr)

    is_qk = j < 2 * n_q_blocks
    head_gain = jnp.where(j < n_q_blocks, qg_ref[...], kg_ref[...])
    h = h_scr[...]
    for c in range(0, o_ref.shape[1], MXU_COLS):
        acc = _mm(h, w_ref[:, c:c + MXU_COLS])
        for hh in range(MXU_COLS // HEAD_DIM):
            a = acc[:, hh * HEAD_DIM:(hh + 1) * HEAD_DIM]
            ms = jnp.mean(a * a, axis=-1, keepdims=True)
            normed = (a * lax.rsqrt(ms + EPS)) * head_gain
            cols = slice(c + hh * HEAD_DIM, c + (hh + 1) * HEAD_DIM)
            o_ref[:, cols] = jnp.where(is_qk, normed, a).astype(o_ref.dtype)


def _qkv(x, mod, norm, w_qkv, q_norm, k_norm, layer, row0, n_rows, out_dtype):
    tm, tn = 1024, 1024
    jl = layer // 2
    tile0 = row0 // tm
    return pl.pallas_call(
        functools.partial(_qkv_kernel, n_q_blocks=D // tn),
        grid=(n_rows // tm, 3 * D // tn),
        in_specs=[
            pl.BlockSpec((tm, D), lambda i, j: (tile0 + i, 0)),
            pl.BlockSpec((None, 1, D), lambda i, j: (layer, 0, 0)),
            _mod_spec(layer, 1, tm, tile0=tile0), _mod_spec(layer, 0, tm, tile0=tile0),
            pl.BlockSpec((None, D, tn), lambda i, j: (jl, 0, j)),
            pl.BlockSpec((None, 1, HEAD_DIM), lambda i, j: (jl, 0, 0)),
            pl.BlockSpec((None, 1, HEAD_DIM), lambda i, j: (jl, 0, 0)),
        ],
        out_specs=pl.BlockSpec((tm, tn), lambda i, j: (i, j)),
        out_shape=jax.ShapeDtypeStruct((n_rows, 3 * D), out_dtype),
        scratch_shapes=[pltpu.VMEM((tm, D), bf16)],
        compiler_params=_params(("arbitrary", "arbitrary"),
                                _nbytes((tm, D), f32) + _nbytes((D, tn), bf16) + _nbytes((tm, tn), out_dtype),
                                _nbytes((tm, D), bf16) + 2 * _nbytes((tm, tn), f32)),
        name="qkv",
    )(x, norm.reshape(DEPTH, 1, D), mod, mod, w_qkv,
      q_norm.reshape(-1, 1, HEAD_DIM), k_norm.reshape(-1, 1, HEAD_DIM))


def _mm_res_kernel(*refs, n_ctx_tiles):
    *a_refs, w_ref, x_ref, gate_ref, o_ref = refs

    def update(a_ref):
        o_ref[...] = x_ref[...] + gate_ref[...] * _mm(a_ref[...], w_ref[...])

    if len(a_refs) == 1:
        update(a_refs[0])
    else:
        i = pl.program_id(0)

        @pl.when(i < n_ctx_tiles)
        def _():
            update(a_refs[0])

        @pl.when(i >= n_ctx_tiles)
        def _():
            update(a_refs[1])


def _mm_residual(a_parts, w, x, mod, layer, which, tm, tn, name):
    k = a_parts[0].shape[1]
    jl = layer // 2
    n_ctx_tiles = CTX_ROWS // tm
    if len(a_parts) == 1:
        a_specs = [pl.BlockSpec((tm, k), lambda i, j: (i, 0))]
    else:
        a_specs = [pl.BlockSpec((tm, k), lambda i, j: (jnp.minimum(i, n_ctx_tiles - 1), 0)),
                   pl.BlockSpec((tm, k), lambda i, j: (jnp.maximum(i - n_ctx_tiles, 0), 0))]
    return pl.pallas_call(
        functools.partial(_mm_res_kernel, n_ctx_tiles=n_ctx_tiles),
        grid=(T // tm, D // tn),
        in_specs=a_specs + [
            pl.BlockSpec((None, k, tn), lambda i, j: (jl, 0, j)),
            pl.BlockSpec((tm, tn), lambda i, j: (i, j)),
            _mod_spec(layer, which, tm, tn),
        ],
        out_specs=pl.BlockSpec((tm, tn), lambda i, j: (i, j)),
        out_shape=jax.ShapeDtypeStruct((T, D), f32),
        compiler_params=_params(("arbitrary", "arbitrary"),
                                len(a_parts) * _nbytes((tm, k), bf16) + _nbytes((k, tn), bf16)
                                + 2 * _nbytes((tm, tn), f32),
                                2 * _nbytes((tm, tn), f32)),
        name=name,
    )(*a_parts, w, x, mod)


def _conv_gate_kernel(b_ref, u_ref, up_ref, un_ref, cw_ref, o_ref, *, tm, cols):
    row0 = pl.program_id(0) * tm
    seq_len = jnp.where(row0 < CTX_ROWS, CTX_LEN, DEC_LEN)
    local = lax.broadcasted_iota(jnp.int32, (tm, cols), 0)
    pos = (row0 + local) & (seq_len - 1)
    is_first = pos == 0
    is_last = pos == seq_len - 1
    for c in range(D // cols):
        sl = slice(c * cols, (c + 1) * cols)
        u = u_ref[:, sl]
        prev_row = up_ref[SUBLANES - 1:SUBLANES, sl]
        next_row = un_ref[0:1, sl]
        u_prev = jnp.where(local == 0, prev_row, pltpu.roll(u, 1, axis=0))
        u_next = jnp.where(local == tm - 1, next_row, pltpu.roll(u, tm - 1, axis=0))
        u_prev = jnp.where(is_first, 0.0, u_prev)
        u_next = jnp.where(is_last, 0.0, u_next)
        conv = u_prev * cw_ref[0:1, sl] + u * cw_ref[1:2, sl] + u_next * cw_ref[2:3, sl]
        o_ref[:, sl] = (b_ref[:, sl] * conv).astype(bf16)


def _conv_gate(b, u, conv_w, layer):
    tm, cols = 512, 512
    jl = layer // 2
    blocks_per_tile = tm // SUBLANES
    return pl.pallas_call(
        functools.partial(_conv_gate_kernel, tm=tm, cols=cols),
        grid=(T // tm,),
        in_specs=[
            pl.BlockSpec((tm, D), lambda i: (i, 0)),
            pl.BlockSpec((tm, D), lambda i: (i, 0)),
            pl.BlockSpec((SUBLANES, D), lambda i: (jnp.maximum(i * blocks_per_tile - 1, 0), 0)),
            pl.BlockSpec((SUBLANES, D), lambda i: (jnp.minimum((i + 1) * blocks_per_tile, T // SUBLANES - 1), 0)),
            pl.BlockSpec((None, 3, D), lambda i: (jl, 0, 0)),
        ],
        out_specs=pl.BlockSpec((tm, D), lambda i: (i, 0)),
        out_shape=jax.ShapeDtypeStruct((T, D), bf16),
        compiler_params=_params(("arbitrary",),
                                2 * _nbytes((tm, D), f32) + _nbytes((tm, D), bf16) + 2 * _nbytes((SUBLANES, D), f32),
                                8 * _nbytes((tm, cols), f32)),
        name="conv_gate",
    )(b, u, u, u, conv_w)


def _softmax_pv(parts):
    m = parts[0][0].max(axis=-1, keepdims=True)
    for s, _ in parts[1:]:
        m = jnp.maximum(m, s.max(axis=-1, keepdims=True))
    acc = None
    for s, v in parts:
        p = jnp.exp2(s - m).astype(bf16)
        v_and_ones = jnp.concatenate([v, jnp.ones_like(v)], axis=1)
        o = _mm(p, v_and_ones)
        acc = o if acc is None else acc + o
    head_dim = acc.shape[1] // 2
    return acc[:, :head_dim] * (1.0 / acc[:, head_dim:head_dim + 1])


def _ctx_attn_kernel(q_ref, k_ref, v_ref, o_ref, k_out_ref, v_out_ref):
    k_out_ref[...] = k_ref[...]
    v_out_ref[...] = v_ref[...]
    for hh in range(q_ref.shape[1] // HEAD_DIM):
        sl = slice(hh * HEAD_DIM, (hh + 1) * HEAD_DIM)
        q = q_ref[:, sl].astype(bf16)
        k = k_ref[:, sl].astype(bf16)
        v = v_ref[:, sl].astype(bf16)
        s = _mm_nt(q, k) * SCORE_SCALE_LOG2
        o_ref[:, sl] = _softmax_pv([(s, v)]).astype(bf16)


def _ctx_attention(qkv):
    tn = 512
    nb = D // tn
    return pl.pallas_call(
        _ctx_attn_kernel,
        grid=(N_CTX_SEQ, nb),
        in_specs=[
            pl.BlockSpec((CTX_LEN, tn), lambda b, g: (b, g)),
            pl.BlockSpec((CTX_LEN, tn), lambda b, g: (b, nb + g)),
            pl.BlockSpec((CTX_LEN, tn), lambda b, g: (b, 2 * nb + g)),
        ],
        out_specs=[pl.BlockSpec((CTX_LEN, tn), lambda b, g: (b, g))] * 3,
        out_shape=[jax.ShapeDtypeStruct((CTX_ROWS, D), bf16),
                   jax.ShapeDtypeStruct((CTX_ROWS, D), f32),
                   jax.ShapeDtypeStruct((CTX_ROWS, D), f32)],
        compiler_params=_params(("arbitrary", "arbitrary"),
                                5 * _nbytes((CTX_LEN, tn), f32) + _nbytes((CTX_LEN, tn), bf16),
                                8 * _nbytes((CTX_LEN, CTX_LEN), f32)),
        name="ctx_attention",
    )(qkv, qkv, qkv)


def _band_start_row(m):
    return jnp.clip(Q_ROWS_PER_BLOCK * m - WIN_ROWS // 2, 0, GRID_ROWS - BAND_ROWS)


def _band_block_ids(m):
    band0 = _band_start_row(m)
    ids = []
    for rho in range(Q_ROWS_PER_BLOCK):
        r = Q_ROWS_PER_BLOCK * m + rho
        win0 = jnp.clip(r - WIN_ROWS // 2, 0, GRID_ROWS - WIN_ROWS)
        row_ids = []
        for kap in range(BAND_ROWS):
            kr = band0 + kap
            inside = (kr >= win0) & (kr < win0 + WIN_ROWS)
            row_ids.append(jnp.where(inside, kr - r + WIN_ROWS - 1, MASKED_BLOCK))
        ids.append(row_ids)
    return ids


def _band_bias(bias_ref, hh, ids):
    left_half = lax.broadcasted_iota(jnp.int32, (GRID_W, 2 * GRID_W), 1) < GRID_W
    rows = []
    for rho in range(Q_ROWS_PER_BLOCK):
        pairs = [jnp.where(left_half, bias_ref[hh, ids[rho][kap]], bias_ref[hh, ids[rho][kap + 1]])
                 for kap in range(0, BAND_ROWS, 2)]
        rows.append(jnp.concatenate(pairs, axis=1))
    return jnp.concatenate(rows, axis=0)


def _loc_attn_kernel(q_ref, k_ref, v_ref, ck_ref, cv_ref, bias_ref, o_ref):
    start = pl.multiple_of(_band_start_row(pl.program_id(2)) * GRID_W, GRID_W)
    ids = _band_block_ids(pl.program_id(2))
    for hh in range(q_ref.shape[1] // HEAD_DIM):
        sl = slice(hh * HEAD_DIM, (hh + 1) * HEAD_DIM)
        q = q_ref[:, sl]
        k_band = k_ref[pl.ds(start, BAND), sl]
        v_band = v_ref[pl.ds(start, BAND), sl]
        k_ctx = ck_ref[:, sl].astype(bf16)
        v_ctx = cv_ref[:, sl].astype(bf16)
        s_loc = _mm_nt(q, k_band) * SCORE_SCALE_LOG2 + _band_bias(bias_ref, hh, ids)
        s_ctx = _mm_nt(q, k_ctx) * SCORE_SCALE_LOG2
        o_ref[:, sl] = _softmax_pv([(s_loc, v_band), (s_ctx, v_ctx)]).astype(bf16)


def _rel_bias_blocks(rpb_l):
    cols = np.arange(GRID_W)
    col_start = np.clip(cols - WIN_COLS // 2, 0, GRID_W - WIN_COLS)
    col_ok = (cols[None, :] >= col_start[:, None]) & (cols[None, :] < col_start[:, None] + WIN_COLS)
    pad = GRID_W - WIN_COLS
    padded = jnp.pad(rpb_l.astype(f32), ((0, 0), (0, 0), (pad, pad)))
    skewed = jnp.tile(padded, (1, 1, GRID_W + 1))[:, :, :GRID_W * 2 * GRID_W]
    skewed = skewed.reshape(N_HEADS, MASKED_BLOCK, GRID_W, 2 * GRID_W)[..., :GRID_W]
    toeplitz = jnp.flip(skewed, axis=2)
    toeplitz = jnp.where(col_ok[None, None], toeplitz * LOG2_E, NEG_INF)
    masked = jnp.full((N_HEADS, 1, GRID_W, GRID_W), NEG_INF, f32)
    blocks = jnp.concatenate([toeplitz, masked], axis=1)
    return jnp.concatenate([blocks, blocks], axis=3)


def _latent_attention(qkv, cache_k, cache_v, layer_slot, bias_blocks):
    tn = 512
    nb = D // tn
    hb = tn // HEAD_DIM
    return pl.pallas_call(
        _loc_attn_kernel,
        grid=(N_DEC_SEQ, nb, N_Q_BLOCKS),
        in_specs=[
            pl.BlockSpec((Q_BLOCK, tn), lambda b, g, m: (b * N_Q_BLOCKS + m, g)),
            pl.BlockSpec((DEC_LEN, tn), lambda b, g, m: (b, nb + g)),
            pl.BlockSpec((DEC_LEN, tn), lambda b, g, m: (b, 2 * nb + g)),
            pl.BlockSpec((None, None, PAST_LEN, tn), lambda b, g, m: (b, layer_slot, 0, g)),
            pl.BlockSpec((None, None, PAST_LEN, tn), lambda b, g, m: (b, layer_slot, 0, g)),
            pl.BlockSpec((hb, N_BIAS_BLOCKS, GRID_W, 2 * GRID_W), lambda b, g, m: (g, 0, 0, 0)),
        ],
        out_specs=pl.BlockSpec((Q_BLOCK, tn), lambda b, g, m: (b * N_Q_BLOCKS + m, g)),
        out_shape=jax.ShapeDtypeStruct((DEC_ROWS, D), bf16),
        compiler_params=_params(("arbitrary", "arbitrary", "arbitrary"),
                                _nbytes((Q_BLOCK, tn), bf16) + 2 * _nbytes((DEC_LEN, tn), bf16)
                                + 2 * _nbytes((PAST_LEN, tn), f32)
                                + _nbytes((hb, N_BIAS_BLOCKS, GRID_W, 2 * GRID_W), f32)
                                + _nbytes((Q_BLOCK, tn), bf16),
                                10 * _nbytes((Q_BLOCK, BAND), f32)),
        name="latent_attention",
    )(qkv, qkv, qkv, cache_k, cache_v, bias_blocks)


HI_HALF_MASK = 0xFFFF0000


def _pack_bf16_halves(h):
    bits = lax.bitcast_convert_type(h.astype(f32), jnp.uint32)
    half = h.shape[1] // 2
    return (bits[:, :half] >> 16) | (bits[:, half:] & jnp.uint32(HI_HALF_MASK))


def _unpack_bf16_halves(words):
    lo = lax.bitcast_convert_type(words << 16, f32).astype(bf16)
    hi = lax.bitcast_convert_type(words & jnp.uint32(HI_HALF_MASK), f32).astype(bf16)
    return lo, hi


def _router_kernel(x_ref, gain_ref, sc_ref, sh_ref, wr_ref, hp_ref, idx_ref, wt_ref, rank_ref, cnt_ref, seen):
    tm = x_ref.shape[0]

    @pl.when(pl.program_id(0) == 0)
    def _():
        seen[...] = jnp.zeros(seen.shape, seen.dtype)

    h = _norm_mod_rows(x_ref[...], gain_ref[...], 1.0 + sc_ref[...], sh_ref[...]).astype(bf16)
    hp_ref[...] = _pack_bf16_halves(h)
    logits = _mm(h, wr_ref[...].astype(bf16))
    lane = lax.broadcasted_iota(jnp.int32, (tm, LANES), 1)
    l1 = jnp.where(lane < N_EXPERTS, logits, -jnp.inf)
    m1 = l1.max(axis=-1, keepdims=True)
    i1 = jnp.where(l1 == m1, lane, LANES).min(axis=-1, keepdims=True)
    l2 = jnp.where(lane == i1, -jnp.inf, l1)
    m2 = l2.max(axis=-1, keepdims=True)
    i2 = jnp.where(l2 == m2, lane, LANES).min(axis=-1, keepdims=True)
    e = jnp.exp(m2 - m1)
    w1 = 1.0 / (1.0 + e)
    w2 = e / (1.0 + e)
    idx_ref[...] = jnp.where(lane == 0, i1, jnp.where(lane == 1, i2, 0))
    wt_ref[...] = jnp.where(lane == 0, w1, jnp.where(lane == 1, w2, 0.0))

    first = lane == i1
    second = lane == i2
    chosen = (first | second).astype(bf16)
    earlier = (lax.broadcasted_iota(jnp.int32, (tm, tm), 1) < lax.broadcasted_iota(jnp.int32, (tm, tm), 0))
    before = _mm(earlier.astype(bf16), chosen) + seen[...]
    r1 = jnp.where(first, before, 0.0).sum(axis=-1, keepdims=True)
    r2 = jnp.where(second, before, 0.0).sum(axis=-1, keepdims=True)
    rank_ref[...] = jnp.where(lane == 0, r1, jnp.where(lane == 1, r2, 0.0)).astype(jnp.int32)
    seen[...] += chosen.astype(f32).sum(axis=0, keepdims=True)
    cnt_ref[...] = seen[...].astype(jnp.int32)


def _router(x, mod, norm, w_router_padded, layer):
    tm = 512
    jl = layer // 2
    lane_block = pl.BlockSpec((tm, LANES), lambda i: (i, 0))
    return pl.pallas_call(
        _router_kernel,
        grid=(T // tm,),
        in_specs=[
            pl.BlockSpec((tm, D), lambda i: (i, 0)),
            pl.BlockSpec((None, 1, D), lambda i: (layer, 0, 0)),
            _mod_spec(layer, 4, tm), _mod_spec(layer, 3, tm),
            pl.BlockSpec((None, D, LANES), lambda i: (jl, 0, 0)),
        ],
        out_specs=[pl.BlockSpec((tm, D // 2), lambda i: (i, 0)), lane_block, lane_block, lane_block,
                   pl.BlockSpec((1, LANES), lambda i: (0, 0))],
        out_shape=[jax.ShapeDtypeStruct((T, D // 2), jnp.uint32),
                   jax.ShapeDtypeStruct((T, LANES), jnp.int32),
                   jax.ShapeDtypeStruct((T, LANES), f32),
                   jax.ShapeDtypeStruct((T, LANES), jnp.int32),
                   jax.ShapeDtypeStruct((1, LANES), jnp.int32)],
        scratch_shapes=[pltpu.VMEM((1, LANES), f32)],
        compiler_params=_params(("arbitrary",),
                                _nbytes((tm, D), f32) + _nbytes((tm, D), bf16) + 3 * _nbytes((tm, LANES), f32),
                                _nbytes((D, LANES), f32) + 4 * _nbytes((tm, D), f32) + 2 * _nbytes((tm, tm), f32)),
        name="router",
    )(x, norm.reshape(DEPTH, 1, D), mod, mod, w_router_padded)


def _routing_plan(expert_ids, ranks, counts):
    e_flat = expert_ids.reshape(-1)
    span = ((counts + MOE_TM - 1) // MOE_TM) * MOE_TM
    group_end = jnp.cumsum(span)
    group_start = group_end - span
    experts = jnp.arange(N_EXPERTS, dtype=jnp.int32)
    onehot = e_flat[:, None] == experts[None, :]
    dest = ranks.reshape(-1) + jnp.sum(jnp.where(onehot, group_start[None, :], 0), axis=1)
    src_token = jnp.zeros((MOE_ROWS,), jnp.int32).at[dest].set(
        jnp.arange(N_ASSIGN, dtype=jnp.int32) // TOP_K, unique_indices=True)
    n_tiles = (group_end[-1] // MOE_TM).astype(jnp.int32)
    tile_row0 = jnp.arange(MOE_TILES, dtype=jnp.int32) * MOE_TM
    tile_expert = jnp.sum((tile_row0[:, None] >= group_end[None, :]).astype(jnp.int32), axis=1)
    tile_expert = jnp.minimum(tile_expert, N_EXPERTS - 1)
    rows_end = jnp.sum(jnp.where(tile_expert[:, None] == experts[None, :], (group_start + counts)[None, :], 0), axis=1)
    tile_half = (rows_end - tile_row0 <= MOE_TM // 2).astype(jnp.int32)
    last_used = tile_expert[jnp.maximum(n_tiles - 1, 0)]
    tile_expert = jnp.where(jnp.arange(MOE_TILES) < n_tiles, tile_expert, last_used).astype(jnp.int32)
    return src_token, dest.astype(jnp.int32), tile_expert, tile_half, n_tiles.reshape(1)


def _dispatch_kernel(src_ref, nt_ref, hp_hbm, o_ref, buf, sem):
    i = pl.program_id(0)
    n_tiles = nt_ref[0]
    chunk = o_ref.shape[0]

    def tile_copies(tile, slot):
        return [pltpu.make_async_copy(hp_hbm.at[pl.ds(src_ref[tile * chunk + r], 1), :],
                                      buf.at[slot, pl.ds(r, 1), :], sem.at[slot]) for r in range(chunk)]

    def start_tile(tile, slot):
        for cp in tile_copies(tile, slot):
            cp.start()

    @pl.when((i == 0) & (n_tiles > 0))
    def _():
        start_tile(0, 0)

    @pl.when(i + 1 < n_tiles)
    def _():
        start_tile(i + 1, (i + 1) % 2)

    @pl.when(i < n_tiles)
    def _():
        slot = i % 2
        for cp in tile_copies(i, slot):
            cp.wait()
        lo, hi = _unpack_bf16_halves(buf[slot])
        o_ref[:, :D // 2] = lo
        o_ref[:, D // 2:] = hi

    @pl.when(i >= n_tiles)
    def _():
        o_ref[...] = jnp.zeros(o_ref.shape, o_ref.dtype)


def _dispatch(h_packed, src_token, n_tiles):
    chunk = MOE_TM
    return pl.pallas_call(
        _dispatch_kernel,
        grid_spec=pltpu.PrefetchScalarGridSpec(
            num_scalar_prefetch=2,
            grid=(MOE_ROWS // chunk,),
            in_specs=[pl.BlockSpec(memory_space=pl.ANY)],
            out_specs=pl.BlockSpec((chunk, D), lambda i, src, nt: (i, 0)),
            scratch_shapes=[pltpu.VMEM((2, chunk, D // 2), jnp.uint32), pltpu.SemaphoreType.DMA((2,))],
        ),
        out_shape=jax.ShapeDtypeStruct((MOE_ROWS, D), bf16),
        compiler_params=_params(("arbitrary",), _nbytes((chunk, D), bf16), 5 * _nbytes((chunk, D // 2), f32)),
        name="moe_dispatch",
    )(src_token, n_tiles, h_packed)


def _expert_changed(te_ref, i):
    return (i == 0) | (te_ref[i] != te_ref[jnp.maximum(i - 1, 0)])


def _for_tile_rows(i, half_ref, nt_ref, o_ref, rows_fn):
    tm = o_ref.shape[0]
    used = i < nt_ref[0]
    half = half_ref[i] == 1

    @pl.when(used & jnp.logical_not(half))
    def _():
        o_ref[...] = rows_fn(slice(None))

    @pl.when(used & half)
    def _():
        o_ref[:tm // 2] = rows_fn(slice(0, tm // 2))
        o_ref[tm // 2:] = jnp.zeros((tm // 2, o_ref.shape[1]), o_ref.dtype)

    @pl.when(jnp.logical_not(used))
    def _():
        o_ref[...] = jnp.zeros(o_ref.shape, o_ref.dtype)


def _moe_up_kernel(te_ref, half_ref, nt_ref, hs_ref, wg_ref, wu_ref, a_ref, wg_s, wu_s):
    i = pl.program_id(1)

    @pl.when(_expert_changed(te_ref, i))
    def _():
        wg_s[...] = wg_ref[...].astype(bf16)
        wu_s[...] = wu_ref[...].astype(bf16)

    def gated(rows):
        h = hs_ref[rows]
        g = _mm(h, wg_s[...])
        u = _mm(h, wu_s[...])
        return ((g * jax.nn.sigmoid(g)) * u).astype(bf16)

    _for_tile_rows(i, half_ref, nt_ref, a_ref, gated)


def _moe_up(hs, w_gate, w_up, tile_expert, tile_half, n_tiles, layer):
    tm, tn = MOE_TM, 512
    jl = layer // 2
    return pl.pallas_call(
        _moe_up_kernel,
        grid_spec=pltpu.PrefetchScalarGridSpec(
            num_scalar_prefetch=3,
            grid=(D_FF // tn, MOE_TILES),
            in_specs=[
                pl.BlockSpec((tm, D), lambda j, i, te, half, nt: (i, 0)),
                pl.BlockSpec((None, None, D, tn), lambda j, i, te, half, nt: (jl, te[i], 0, j)),
                pl.BlockSpec((None, None, D, tn), lambda j, i, te, half, nt: (jl, te[i], 0, j)),
            ],
            out_specs=pl.BlockSpec((tm, tn), lambda j, i, te, half, nt: (i, j)),
            scratch_shapes=[pltpu.VMEM((D, tn), bf16), pltpu.VMEM((D, tn), bf16)],
        ),
        out_shape=jax.ShapeDtypeStruct((MOE_ROWS, D_FF), bf16),
        compiler_params=_params(("arbitrary", "arbitrary"),
                                _nbytes((tm, D), bf16) + 2 * _nbytes((D, tn), f32) + _nbytes((tm, tn), bf16),
                                2 * _nbytes((D, tn), bf16) + 3 * _nbytes((tm, tn), f32)),
        name="moe_up",
    )(tile_expert, tile_half, n_tiles, hs, w_gate, w_up)


def _moe_down_kernel(te_ref, half_ref, nt_ref, a_ref, wd_ref, y_ref, wd_s):
    i = pl.program_id(1)

    @pl.when(_expert_changed(te_ref, i))
    def _():
        wd_s[...] = wd_ref[...].astype(bf16)

    _for_tile_rows(i, half_ref, nt_ref, y_ref, lambda rows: _mm(a_ref[rows], wd_s[...]))


def _moe_down(a, w_down, tile_expert, tile_half, n_tiles, layer):
    tm, tn = MOE_TM, 512
    jl = layer // 2
    return pl.pallas_call(
        _moe_down_kernel,
        grid_spec=pltpu.PrefetchScalarGridSpec(
            num_scalar_prefetch=3,
            grid=(D // tn, MOE_TILES),
            in_specs=[
                pl.BlockSpec((tm, D_FF), lambda j, i, te, half, nt: (i, 0)),
                pl.BlockSpec((None, None, D_FF, tn), lambda j, i, te, half, nt: (jl, te[i], 0, j)),
            ],
            out_specs=pl.BlockSpec((tm, tn), lambda j, i, te, half, nt: (i, j)),
            scratch_shapes=[pltpu.VMEM((D_FF, tn), bf16)],
        ),
        out_shape=jax.ShapeDtypeStruct((MOE_ROWS, D), f32),
        compiler_params=_params(("arbitrary", "arbitrary"),
                                _nbytes((tm, D_FF), bf16) + _nbytes((D_FF, tn), f32) + _nbytes((tm, tn), f32),
                                _nbytes((D_FF, tn), bf16) + _nbytes((tm, tn), f32)),
        name="moe_down",
    )(tile_expert, tile_half, n_tiles, a, w_down)


def _combine_kernel(pos_ref, x_ref, gate_ref, wt_ref, y_hbm, o_ref, ybuf, sem, *, tm):
    i = pl.program_id(0)
    n_steps = pl.num_programs(0)

    def tile_copies(tile, slot):
        return [pltpu.make_async_copy(y_hbm.at[pl.ds(pos_ref[(tile * tm + t) * TOP_K + c], 1), :],
                                      ybuf.at[slot, c, pl.ds(t, 1), :], sem.at[slot])
                for t in range(tm) for c in range(TOP_K)]

    def start_tile(tile, slot):
        for cp in tile_copies(tile, slot):
            cp.start()

    @pl.when(i == 0)
    def _():
        start_tile(0, 0)

    @pl.when(i + 1 < n_steps)
    def _():
        start_tile(i + 1, (i + 1) % 2)

    slot = i % 2
    for cp in tile_copies(i, slot):
        cp.wait()
    w1 = jnp.broadcast_to(wt_ref[:, 0:1], (tm, LANES))
    w2 = jnp.broadcast_to(wt_ref[:, 1:2], (tm, LANES))
    for s in range(ROW_SLABS):
        sl = slice(s * LANES, (s + 1) * LANES)
        mix = w1 * ybuf[slot, 0, :, sl] + w2 * ybuf[slot, 1, :, sl]
        o_ref[:, sl] = x_ref[:, sl] + gate_ref[:, sl] * mix


def _combine(x, y_slabs, pos, wts, mod, layer):
    tm = 256
    return pl.pallas_call(
        functools.partial(_combine_kernel, tm=tm),
        grid_spec=pltpu.PrefetchScalarGridSpec(
            num_scalar_prefetch=1,
            grid=(T // tm,),
            in_specs=[
                pl.BlockSpec((tm, D), lambda i, pos: (i, 0)),
                _mod_spec(layer, 5, tm),
                pl.BlockSpec((tm, LANES), lambda i, pos: (i, 0)),
                pl.BlockSpec(memory_space=pl.ANY),
            ],
            out_specs=pl.BlockSpec((tm, D), lambda i, pos: (i, 0)),
            scratch_shapes=[pltpu.VMEM((2, TOP_K, tm, D), f32),
                            pltpu.SemaphoreType.DMA((2,))],
        ),
        out_shape=jax.ShapeDtypeStruct((T, D), f32),
        compiler_params=_params(("arbitrary",),
                                2 * _nbytes((tm, D), f32) + _nbytes((tm, LANES), f32),
                                2 * _nbytes((TOP_K, tm, D), f32) + 4 * _nbytes((tm, LANES), f32)),
        name="moe_combine",
    )(pos, x, mod, wts, y_slabs)


def _moe_layer(x, mod, norm2, w_router, w_gate, w_up, w_down, layer):
    wr = jnp.pad(w_router, ((0, 0), (0, 0), (0, LANES - N_EXPERTS)))
    h_packed, idx, wts, ranks, counts = _router(x, mod, norm2, wr, layer)
    src_token, pos, tile_expert, tile_half, n_tiles = _routing_plan(
        idx[:, :TOP_K], ranks[:, :TOP_K], counts[0, :N_EXPERTS])
    hs = _dispatch(h_packed, src_token, n_tiles)
    a = _moe_up(hs, w_gate, w_up, tile_expert, tile_half, n_tiles, layer)
    y = _moe_down(a, w_down, tile_expert, tile_half, n_tiles, layer)
    return _combine(x, y, pos, wts, mod, layer)


def kernel(x_prompt, x_sample, cache_k, cache_v, c, c_ctx, norm1, norm2, w_ada, b_ada, w_conv_in, conv_w,
           w_conv_out, w_qkv, q_norm, k_norm, rpb, w_attn_out, w_ffn_gate, w_ffn_up, w_ffn_down, w_router,
           w_moe_gate, w_moe_up, w_moe_down):
    x = jnp.concatenate([x_prompt.reshape(CTX_ROWS, D), x_sample.reshape(DEC_ROWS, D)], axis=0)
    cond = jnp.concatenate([c_ctx[None, :], c, jnp.zeros((N_GROUPS_PADDED - 1 - N_DEC_SEQ, D), f32)], axis=0)
    mod = _ada_params(cond, w_ada, b_ada)
    cache_k2 = cache_k.reshape(N_DEC_SEQ, -1, PAST_LEN, D)
    cache_v2 = cache_v.reshape(N_DEC_SEQ, -1, PAST_LEN, D)
    w_conv_in, w_conv_out, w_qkv, w_attn_out, w_ffn_gate, w_ffn_up, w_ffn_down = (
        w.astype(bf16) for w in (w_conv_in, w_conv_out, w_qkv, w_attn_out, w_ffn_gate, w_ffn_up, w_ffn_down))

    new_k, new_v = [], []
    for layer in range(DEPTH):
        jl = layer // 2
        if layer % 2 == 0:
            b, u = _conv_in(x, mod, norm1, w_conv_in, layer)
            gated = _conv_gate(b, u, conv_w, layer)
            x = _mm_residual([gated], w_conv_out, x, mod, layer, 2, 1024, 1024, "conv_out")
            a = _ffn_up(x, mod, norm2, w_ffn_gate, w_ffn_up, layer)
            x = _mm_residual([a], w_ffn_down, x, mod, layer, 5, 1024, 512, "ffn_down")
        else:
            qkv_ctx = _qkv(x, mod, norm1, w_qkv, q_norm, k_norm, layer, 0, CTX_ROWS, f32)
            qkv_dec = _qkv(x, mod, norm1, w_qkv, q_norm, k_norm, layer, CTX_ROWS, DEC_ROWS, bf16)
            o_ctx, k_ctx, v_ctx = _ctx_attention(qkv_ctx)
            new_k.append(k_ctx.reshape(N_CTX_SEQ, CTX_LEN, N_HEADS, HEAD_DIM))
            new_v.append(v_ctx.reshape(N_CTX_SEQ, CTX_LEN, N_HEADS, HEAD_DIM))
            o_dec = _latent_attention(qkv_dec, cache_k2, cache_v2, jl, _rel_bias_blocks(rpb[jl]))
            x = _mm_residual([o_ctx, o_dec], w_attn_out, x, mod, layer, 2, 1024, 1024, "attn_out")
            x = _moe_layer(x, mod, norm2, w_router, w_moe_gate, w_moe_up, w_moe_down, layer)

    y_prompt = x[:CTX_ROWS].reshape(N_CTX_SEQ, CTX_LEN, D)
    y_sample = x[CTX_ROWS:].reshape(N_DEC_SEQ, DEC_LEN, D)
    return y_prompt, y_sample, jnp.stack(new_k, axis=1), jnp.stack(new_v, axis=1)
```

```python
import functools

import numpy as np
import jax
import jax.numpy as jnp
from jax import lax
from jax.experimental import pallas as pl
from jax.experimental.pallas import tpu as pltpu

D = 2048
N_CTX_SEQ = 16
CTX_LEN = 256
N_DEC_SEQ = 4
DEC_LEN = 2048
CTX_ROWS = N_CTX_SEQ * CTX_LEN
DEC_ROWS = N_DEC_SEQ * DEC_LEN
T = CTX_ROWS + DEC_ROWS
DEPTH = 4
PAST_LEN = 512
N_HEADS = 16
HEAD_DIM = 128
GRID_W = 64
GRID_ROWS = DEC_LEN // GRID_W
WIN_ROWS = 8
WIN_COLS = 16
D_FF = 5632
N_EXPERTS = 8
TOP_K = 2
EPS = 1e-6
NEG_INF = -1e30
ATTN_SCALE = HEAD_DIM ** -0.5
LOG2_E = 1.4426950408889634
SCORE_SCALE_LOG2 = ATTN_SCALE * LOG2_E

N_GROUPS_PADDED = 8
LANES = 128
SUBLANES = 8
MXU_COLS = 256
VMEM_CAP_BYTES = 56 * 2 ** 20
VMEM_HEADROOM = 4 * 2 ** 20

Q_ROWS_PER_BLOCK = 4
Q_BLOCK = Q_ROWS_PER_BLOCK * GRID_W
N_Q_BLOCKS = GRID_ROWS // Q_ROWS_PER_BLOCK
BAND_ROWS = Q_ROWS_PER_BLOCK + WIN_ROWS
BAND = BAND_ROWS * GRID_W
MASKED_BLOCK = 2 * WIN_ROWS - 1
N_BIAS_BLOCKS = MASKED_BLOCK + 1

MOE_UP_TM = 1024
MOE_DOWN_TM = 512
N_ASSIGN = T * TOP_K
MOE_ROWS = N_ASSIGN + N_EXPERTS * MOE_UP_TM
TILE_EMPTY, TILE_FIRST_HALF, TILE_FULL = 0, 1, 2
ROW_SLABS = D // LANES

bf16 = jnp.bfloat16
f32 = jnp.float32


def _mm(a, b):
    return jnp.dot(a, b, preferred_element_type=f32)


def _mm_nt(a, b):
    return lax.dot_general(a, b, (((1,), (1,)), ((), ())), preferred_element_type=f32)


def _nbytes(shape, dtype):
    return int(np.prod(shape)) * jnp.dtype(dtype).itemsize


def _params(semantics, pipelined_bytes, resident_bytes):
    need = 2 * pipelined_bytes + resident_bytes + VMEM_HEADROOM
    return pltpu.CompilerParams(dimension_semantics=semantics,
                                vmem_limit_bytes=int(min(max(need, 16 * 2 ** 20), VMEM_CAP_BYTES)))


def _group_of_tile(i, tm):
    n_ctx = CTX_ROWS // tm
    per_seq = DEC_LEN // tm
    return jnp.where(i < n_ctx, 0, 1 + (i - n_ctx) // per_seq)


def _mod_spec(layer, which, tm, width=D, tile0=0):
    def index(*grid_and_prefetch):
        i = tile0 + grid_and_prefetch[0]
        j = grid_and_prefetch[1] if width != D else 0
        return ((layer * N_GROUPS_PADDED + _group_of_tile(i, tm)) * 6 + which, 0, j)
    return pl.BlockSpec((None, 1, width), index)


def _norm_mod_rows(x, gain, scale1p, shift):
    ms = jnp.mean(x * x, axis=-1, keepdims=True)
    y = x * lax.rsqrt(ms + EPS)
    return (y * gain) * scale1p + shift


def _norm_mod_to_scratch(x_ref, gain_ref, sc_ref, sh_ref, h_scr, rows=256):
    gain = gain_ref[...]
    scale1p = 1.0 + sc_ref[...]
    shift = sh_ref[...]

    def body(c, carry):
        r = pl.multiple_of(c * rows, rows)
        h = _norm_mod_rows(x_ref[pl.ds(r, rows), :], gain, scale1p, shift)
        h_scr[pl.ds(r, rows), :] = h.astype(bf16)
        return carry

    lax.fori_loop(0, x_ref.shape[0] // rows, body, 0)


def _ada_kernel(cond_ref, w_ref, b_ref, o_ref):
    c = cond_ref[...]
    s = (c * jax.nn.sigmoid(c)).astype(bf16)
    o_ref[...] = _mm(s, w_ref[...].astype(bf16)) + b_ref[...]


def _ada_params(cond, w_ada, b_ada):
    tn = 1024
    out = pl.pallas_call(
        _ada_kernel,
        grid=(DEPTH, 6 * D // tn),
        in_specs=[
            pl.BlockSpec((N_GROUPS_PADDED, D), lambda l, j: (0, 0)),
            pl.BlockSpec((None, D, tn), lambda l, j: (l, 0, j)),
            pl.BlockSpec((None, 1, tn), lambda l, j: (l, 0, j)),
        ],
        out_specs=pl.BlockSpec((None, N_GROUPS_PADDED, tn), lambda l, j: (l, 0, j)),
        out_shape=jax.ShapeDtypeStruct((DEPTH, N_GROUPS_PADDED, 6 * D), f32),
        compiler_params=_params(("arbitrary", "arbitrary"),
                                _nbytes((D, tn), f32) + _nbytes((8, tn), f32) * 2,
                                _nbytes((D, tn), bf16) + _nbytes((8, D), f32) * 2),
        name="ada_params",
    )(cond, w_ada, b_ada.reshape(DEPTH, 1, 6 * D))
    return out.reshape(DEPTH * N_GROUPS_PADDED * 6, 1, D)


def _conv_in_kernel(x_ref, gain_ref, sc_ref, sh_ref, wb_ref, wc_ref, wv_ref, b_ref, u_ref, h_scr):
    @pl.when(pl.program_id(1) == 0)
    def _():
        _norm_mod_to_scratch(x_ref, gain_ref, sc_ref, sh_ref, h_scr)

    h = h_scr[...]
    b_ref[...] = _mm(h, wb_ref[...])
    u_ref[...] = _mm(h, wc_ref[...]) * _mm(h, wv_ref[...])


def _conv_in(x, mod, norm, w_conv_in, layer):
    tm, tn = 1024, 256
    jl = layer // 2
    nb = D // tn
    return pl.pallas_call(
        _conv_in_kernel,
        grid=(T // tm, nb),
        in_specs=[
            pl.BlockSpec((tm, D), lambda i, j: (i, 0)),
            pl.BlockSpec((None, 1, D), lambda i, j: (layer, 0, 0)),
            _mod_spec(layer, 1, tm), _mod_spec(layer, 0, tm),
            pl.BlockSpec((None, D, tn), lambda i, j: (jl, 0, j)),
            pl.BlockSpec((None, D, tn), lambda i, j: (jl, 0, nb + j)),
            pl.BlockSpec((None, D, tn), lambda i, j: (jl, 0, 2 * nb + j)),
        ],
        out_specs=[pl.BlockSpec((tm, tn), lambda i, j: (i, j))] * 2,
        out_shape=[jax.ShapeDtypeStruct((T, D), f32)] * 2,
        scratch_shapes=[pltpu.VMEM((tm, D), bf16)],
        compiler_params=_params(("arbitrary", "arbitrary"),
                                _nbytes((tm, D), f32) + 3 * _nbytes((D, tn), bf16) + 2 * _nbytes((tm, tn), f32),
                                _nbytes((tm, D), bf16) + 3 * _nbytes((tm, tn), f32)),
        name="conv_in",
    )(x, norm.reshape(DEPTH, 1, D), mod, mod, w_conv_in, w_conv_in, w_conv_in)


def _ffn_up_kernel(x_ref, gain_ref, sc_ref, sh_ref, wg_ref, wu_ref, a_ref, h_scr):
    @pl.when(pl.program_id(1) == 0)
    def _():
        _norm_mod_to_scratch(x_ref, gain_ref, sc_ref, sh_ref, h_scr)

    h = h_scr[...]
    g = _mm(h, wg_ref[...])
    u = _mm(h, wu_ref[...])
    a_ref[...] = ((g * jax.nn.sigmoid(g)) * u).astype(bf16)


def _ffn_up(x, mod, norm, w_gate, w_up, layer):
    tm, tn = 1024, 512
    jl = layer // 2
    return pl.pallas_call(
        _ffn_up_kernel,
        grid=(T // tm, D_FF // tn),
        in_specs=[
            pl.BlockSpec((tm, D), lambda i, j: (i, 0)),
            pl.BlockSpec((None, 1, D), lambda i, j: (layer, 0, 0)),
            _mod_spec(layer, 4, tm), _mod_spec(layer, 3, tm),
            pl.BlockSpec((None, D, tn), lambda i, j: (jl, 0, j)),
            pl.BlockSpec((None, D, tn), lambda i, j: (jl, 0, j)),
        ],
        out_specs=pl.BlockSpec((tm, tn), lambda i, j: (i, j)),
        out_shape=jax.ShapeDtypeStruct((T, D_FF), bf16),
        scratch_shapes=[pltpu.VMEM((tm, D), bf16)],
        compiler_params=_params(("arbitrary", "arbitrary"),
                                _nbytes((tm, D), f32) + 2 * _nbytes((D, tn), bf16) + _nbytes((tm, tn), bf16),
                                _nbytes((tm, D), bf16) + 3 * _nbytes((tm, tn), f32)),
        name="ffn_up",
    )(x, norm.reshape(DEPTH, 1, D), mod, mod, w_gate, w_up)


def _qkv_kernel(x_ref, gain_ref, sc_ref, sh_ref, w_ref, qg_ref, kg_ref, o_ref, h_scr, *, n_q_blocks):
    j = pl.program_id(1)

    @pl.when(j == 0)
    def _():
        _norm_mod_to_scratch(x_ref, gain_ref, sc_ref, sh_ref, h_scr)

    is_qk = j < 2 * n_q_blocks
    head_gain = jnp.where(j < n_q_blocks, qg_ref[...], kg_ref[...])
    h = h_scr[...]
    for c in range(0, o_ref.shape[1], MXU_COLS):
        acc = _mm(h, w_ref[:, c:c + MXU_COLS])
        for hh in range(MXU_COLS // HEAD_DIM):
            a = acc[:, hh * HEAD_DIM:(hh + 1) * HEAD_DIM]
            ms = jnp.mean(a * a, axis=-1, keepdims=True)
            normed = (a * lax.rsqrt(ms + EPS)) * head_gain
            cols = slice(c + hh * HEAD_DIM, c + (hh + 1) * HEAD_DIM)
            o_ref[:, cols] = jnp.where(is_qk, normed, a).astype(o_ref.dtype)


def _qkv(x, mod, norm, w_qkv, q_norm, k_norm, layer, row0, n_rows, out_dtype):
    tm, tn = 1024, 1024
    jl = layer // 2
    tile0 = row0 // tm
    return pl.pallas_call(
        functools.partial(_qkv_kernel, n_q_blocks=D // tn),
        grid=(n_rows // tm, 3 * D // tn),
        in_specs=[
            pl.BlockSpec((tm, D), lambda i, j: (tile0 + i, 0)),
            pl.BlockSpec((None, 1, D), lambda i, j: (layer, 0, 0)),
            _mod_spec(layer, 1, tm, tile0=tile0), _mod_spec(layer, 0, tm, tile0=tile0),
            pl.BlockSpec((None, D, tn), lambda i, j: (jl, 0, j)),
            pl.BlockSpec((None, 1, HEAD_DIM), lambda i, j: (jl, 0, 0)),
            pl.BlockSpec((None, 1, HEAD_DIM), lambda i, j: (jl, 0, 0)),
        ],
        out_specs=pl.BlockSpec((tm, tn), lambda i, j: (i, j)),
        out_shape=jax.ShapeDtypeStruct((n_rows, 3 * D), out_dtype),
        scratch_shapes=[pltpu.VMEM((tm, D), bf16)],
        compiler_params=_params(("arbitrary", "arbitrary"),
                                _nbytes((tm, D), f32) + _nbytes((D, tn), bf16) + _nbytes((tm, tn), out_dtype),
                                _nbytes((tm, D), bf16) + 2 * _nbytes((tm, tn), f32)),
        name="qkv",
    )(x, norm.reshape(DEPTH, 1, D), mod, mod, w_qkv,
      q_norm.reshape(-1, 1, HEAD_DIM), k_norm.reshape(-1, 1, HEAD_DIM))


def _mm_res_kernel(*refs, n_ctx_tiles):
    *a_refs, w_ref, x_ref, gate_ref, o_ref = refs

    def update(a_ref):
        o_ref[...] = x_ref[...] + gate_ref[...] * _mm(a_ref[...], w_ref[...])

    if len(a_refs) == 1:
        update(a_refs[0])
    else:
        i = pl.program_id(0)

        @pl.when(i < n_ctx_tiles)
        def _():
            update(a_refs[0])

        @pl.when(i >= n_ctx_tiles)
        def _():
            update(a_refs[1])


def _mm_residual(a_parts, w, x, mod, layer, which, tm, tn, name):
    k = a_parts[0].shape[1]
    jl = layer // 2
    n_ctx_tiles = CTX_ROWS // tm
    if len(a_parts) == 1:
        a_specs = [pl.BlockSpec((tm, k), lambda i, j: (i, 0))]
    else:
        a_specs = [pl.BlockSpec((tm, k), lambda i, j: (jnp.minimum(i, n_ctx_tiles - 1), 0)),
                   pl.BlockSpec((tm, k), lambda i, j: (jnp.maximum(i - n_ctx_tiles, 0), 0))]
    return pl.pallas_call(
        functools.partial(_mm_res_kernel, n_ctx_tiles=n_ctx_tiles),
        grid=(T // tm, D // tn),
        in_specs=a_specs + [
            pl.BlockSpec((None, k, tn), lambda i, j: (jl, 0, j)),
            pl.BlockSpec((tm, tn), lambda i, j: (i, j)),
            _mod_spec(layer, which, tm, tn),
        ],
        out_specs=pl.BlockSpec((tm, tn), lambda i, j: (i, j)),
        out_shape=jax.ShapeDtypeStruct((T, D), f32),
        compiler_params=_params(("arbitrary", "arbitrary"),
                                len(a_parts) * _nbytes((tm, k), bf16) + _nbytes((k, tn), bf16)
                                + 2 * _nbytes((tm, tn), f32),
                                2 * _nbytes((tm, tn), f32)),
        name=name,
    )(*a_parts, w, x, mod)


def _conv_gate_kernel(b_ref, u_ref, up_ref, un_ref, cw_ref, o_ref, *, tm, cols):
    row0 = pl.program_id(0) * tm
    seq_len = jnp.where(row0 < CTX_ROWS, CTX_LEN, DEC_LEN)
    local = lax.broadcasted_iota(jnp.int32, (tm, cols), 0)
    pos = (row0 + local) & (seq_len - 1)
    is_first = pos == 0
    is_last = pos == seq_len - 1
    for c in range(D // cols):
        sl = slice(c * cols, (c + 1) * cols)
        u = u_ref[:, sl]
        prev_row = up_ref[SUBLANES - 1:SUBLANES, sl]
        next_row = un_ref[0:1, sl]
        u_prev = jnp.where(local == 0, prev_row, pltpu.roll(u, 1, axis=0))
        u_next = jnp.where(local == tm - 1, next_row, pltpu.roll(u, tm - 1, axis=0))
        u_prev = jnp.where(is_first, 0.0, u_prev)
        u_next = jnp.where(is_last, 0.0, u_next)
        conv = u_prev * cw_ref[0:1, sl] + u * cw_ref[1:2, sl] + u_next * cw_ref[2:3, sl]
        o_ref[:, sl] = (b_ref[:, sl] * conv).astype(bf16)


def _conv_gate(b, u, conv_w, layer):
    tm, cols = 512, 512
    jl = layer // 2
    blocks_per_tile = tm // SUBLANES
    return pl.pallas_call(
        functools.partial(_conv_gate_kernel, tm=tm, cols=cols),
        grid=(T // tm,),
        in_specs=[
            pl.BlockSpec((tm, D), lambda i: (i, 0)),
            pl.BlockSpec((tm, D), lambda i: (i, 0)),
            pl.BlockSpec((SUBLANES, D), lambda i: (jnp.maximum(i * blocks_per_tile - 1, 0), 0)),
            pl.BlockSpec((SUBLANES, D), lambda i: (jnp.minimum((i + 1) * blocks_per_tile, T // SUBLANES - 1), 0)),
            pl.BlockSpec((None, 3, D), lambda i: (jl, 0, 0)),
        ],
        out_specs=pl.BlockSpec((tm, D), lambda i: (i, 0)),
        out_shape=jax.ShapeDtypeStruct((T, D), bf16),
        compiler_params=_params(("arbitrary",),
                                2 * _nbytes((tm, D), f32) + _nbytes((tm, D), bf16) + 2 * _nbytes((SUBLANES, D), f32),
                                8 * _nbytes((tm, cols), f32)),
        name="conv_gate",
    )(b, u, u, u, conv_w)


def _softmax_pv(parts):
    m = parts[0][0].max(axis=-1, keepdims=True)
    for s, _ in parts[1:]:
        m = jnp.maximum(m, s.max(axis=-1, keepdims=True))
    acc = None
    for s, v in parts:
        p = jnp.exp2(s - m).astype(bf16)
        v_and_ones = jnp.concatenate([v, jnp.ones_like(v)], axis=1)
        o = _mm(p, v_and_ones)
        acc = o if acc is None else acc + o
    head_dim = acc.shape[1] // 2
    return acc[:, :head_dim] * (1.0 / acc[:, head_dim:head_dim + 1])


def _ctx_attn_kernel(q_ref, k_ref, v_ref, o_ref, k_out_ref, v_out_ref):
    k_out_ref[...] = k_ref[...]
    v_out_ref[...] = v_ref[...]
    for hh in range(q_ref.shape[1] // HEAD_DIM):
        sl = slice(hh * HEAD_DIM, (hh + 1) * HEAD_DIM)
        q = q_ref[:, sl].astype(bf16)
        k = k_ref[:, sl].astype(bf16)
        v = v_ref[:, sl].astype(bf16)
        s = _mm_nt(q, k) * SCORE_SCALE_LOG2
        o_ref[:, sl] = _softmax_pv([(s, v)]).astype(bf16)


def _ctx_attention(qkv):
    tn = 512
    nb = D // tn
    return pl.pallas_call(
        _ctx_attn_kernel,
        grid=(N_CTX_SEQ, nb),
        in_specs=[
            pl.BlockSpec((CTX_LEN, tn), lambda b, g: (b, g)),
            pl.BlockSpec((CTX_LEN, tn), lambda b, g: (b, nb + g)),
            pl.BlockSpec((CTX_LEN, tn), lambda b, g: (b, 2 * nb + g)),
        ],
        out_specs=[pl.BlockSpec((CTX_LEN, tn), lambda b, g: (b, g))] * 3,
        out_shape=[jax.ShapeDtypeStruct((CTX_ROWS, D), bf16),
                   jax.ShapeDtypeStruct((CTX_ROWS, D), f32),
                   jax.ShapeDtypeStruct((CTX_ROWS, D), f32)],
        compiler_params=_params(("arbitrary", "arbitrary"),
                                5 * _nbytes((CTX_LEN, tn), f32) + _nbytes((CTX_LEN, tn), bf16),
                                8 * _nbytes((CTX_LEN, CTX_LEN), f32)),
        name="ctx_attention",
    )(qkv, qkv, qkv)


def _band_start_row(m):
    return jnp.clip(Q_ROWS_PER_BLOCK * m - WIN_ROWS // 2, 0, GRID_ROWS - BAND_ROWS)


def _band_block_ids(m):
    band0 = _band_start_row(m)
    ids = []
    for rho in range(Q_ROWS_PER_BLOCK):
        r = Q_ROWS_PER_BLOCK * m + rho
        win0 = jnp.clip(r - WIN_ROWS // 2, 0, GRID_ROWS - WIN_ROWS)
        row_ids = []
        for kap in range(BAND_ROWS):
            kr = band0 + kap
            inside = (kr >= win0) & (kr < win0 + WIN_ROWS)
            row_ids.append(jnp.where(inside, kr - r + WIN_ROWS - 1, MASKED_BLOCK))
        ids.append(row_ids)
    return ids


def _band_bias(bias_ref, hh, ids):
    left_half = lax.broadcasted_iota(jnp.int32, (GRID_W, 2 * GRID_W), 1) < GRID_W
    rows = []
    for rho in range(Q_ROWS_PER_BLOCK):
        pairs = [jnp.where(left_half, bias_ref[hh, ids[rho][kap]], bias_ref[hh, ids[rho][kap + 1]])
                 for kap in range(0, BAND_ROWS, 2)]
        rows.append(jnp.concatenate(pairs, axis=1))
    return jnp.concatenate(rows, axis=0)


def _loc_attn_kernel(q_ref, k_ref, v_ref, ck_ref, cv_ref, bias_ref, o_ref):
    start = pl.multiple_of(_band_start_row(pl.program_id(2)) * GRID_W, GRID_W)
    ids = _band_block_ids(pl.program_id(2))
    for hh in range(q_ref.shape[1] // HEAD_DIM):
        sl = slice(hh * HEAD_DIM, (hh + 1) * HEAD_DIM)
        q = q_ref[:, sl]
        k_band = k_ref[pl.ds(start, BAND), sl]
        v_band = v_ref[pl.ds(start, BAND), sl]
        k_ctx = ck_ref[:, sl].astype(bf16)
        v_ctx = cv_ref[:, sl].astype(bf16)
        s_loc = _mm_nt(q, k_band) * SCORE_SCALE_LOG2 + _band_bias(bias_ref, hh, ids)
        s_ctx = _mm_nt(q, k_ctx) * SCORE_SCALE_LOG2
        o_ref[:, sl] = _softmax_pv([(s_loc, v_band), (s_ctx, v_ctx)]).astype(bf16)


def _rel_bias_blocks(rpb_l):
    cols = np.arange(GRID_W)
    col_start = np.clip(cols - WIN_COLS // 2, 0, GRID_W - WIN_COLS)
    col_ok = (cols[None, :] >= col_start[:, None]) & (cols[None, :] < col_start[:, None] + WIN_COLS)
    pad = GRID_W - WIN_COLS
    padded = jnp.pad(rpb_l.astype(f32), ((0, 0), (0, 0), (pad, pad)))
    skewed = jnp.tile(padded, (1, 1, GRID_W + 1))[:, :, :GRID_W * 2 * GRID_W]
    skewed = skewed.reshape(N_HEADS, MASKED_BLOCK, GRID_W, 2 * GRID_W)[..., :GRID_W]
    toeplitz = jnp.flip(skewed, axis=2)
    toeplitz = jnp.where(col_ok[None, None], toeplitz * LOG2_E, NEG_INF)
    masked = jnp.full((N_HEADS, 1, GRID_W, GRID_W), NEG_INF, f32)
    blocks = jnp.concatenate([toeplitz, masked], axis=1)
    return jnp.concatenate([blocks, blocks], axis=3)


def _latent_attention(qkv, cache_k, cache_v, layer_slot, bias_blocks):
    tn = 512
    nb = D // tn
    hb = tn // HEAD_DIM
    return pl.pallas_call(
        _loc_attn_kernel,
        grid=(N_DEC_SEQ, nb, N_Q_BLOCKS),
        in_specs=[
            pl.BlockSpec((Q_BLOCK, tn), lambda b, g, m: (b * N_Q_BLOCKS + m, g)),
            pl.BlockSpec((DEC_LEN, tn), lambda b, g, m: (b, nb + g)),
            pl.BlockSpec((DEC_LEN, tn), lambda b, g, m: (b, 2 * nb + g)),
            pl.BlockSpec((None, None, PAST_LEN, tn), lambda b, g, m: (b, layer_slot, 0, g)),
            pl.BlockSpec((None, None, PAST_LEN, tn), lambda b, g, m: (b, layer_slot, 0, g)),
            pl.BlockSpec((hb, N_BIAS_BLOCKS, GRID_W, 2 * GRID_W), lambda b, g, m: (g, 0, 0, 0)),
        ],
        out_specs=pl.BlockSpec((Q_BLOCK, tn), lambda b, g, m: (b * N_Q_BLOCKS + m, g)),
        out_shape=jax.ShapeDtypeStruct((DEC_ROWS, D), bf16),
        compiler_params=_params(("arbitrary", "arbitrary", "arbitrary"),
                                _nbytes((Q_BLOCK, tn), bf16) + 2 * _nbytes((DEC_LEN, tn), bf16)
                                + 2 * _nbytes((PAST_LEN, tn), f32)
                                + _nbytes((hb, N_BIAS_BLOCKS, GRID_W, 2 * GRID_W), f32)
                                + _nbytes((Q_BLOCK, tn), bf16),
                                10 * _nbytes((Q_BLOCK, BAND), f32)),
        name="latent_attention",
    )(qkv, qkv, qkv, cache_k, cache_v, bias_blocks)


HI_HALF_MASK = 0xFFFF0000


def _pack_bf16_halves(h):
    bits = lax.bitcast_convert_type(h.astype(f32), jnp.uint32)
    half = h.shape[1] // 2
    return (bits[:, :half] >> 16) | (bits[:, half:] & jnp.uint32(HI_HALF_MASK))


def _unpack_bf16_halves(words):
    lo = lax.bitcast_convert_type(words << 16, f32).astype(bf16)
    hi = lax.bitcast_convert_type(words & jnp.uint32(HI_HALF_MASK), f32).astype(bf16)
    return lo, hi


def _router_kernel(x_ref, gain_ref, sc_ref, sh_ref, wr_ref, hp_ref, idx_ref, wt_ref, rank_ref, cnt_ref, seen):
    tm = x_ref.shape[0]

    @pl.when(pl.program_id(0) == 0)
    def _():
        seen[...] = jnp.zeros(seen.shape, seen.dtype)

    h = _norm_mod_rows(x_ref[...], gain_ref[...], 1.0 + sc_ref[...], sh_ref[...]).astype(bf16)
    hp_ref[...] = _pack_bf16_halves(h)
    logits = _mm(h, wr_ref[...].astype(bf16))
    lane = lax.broadcasted_iota(jnp.int32, (tm, LANES), 1)
    l1 = jnp.where(lane < N_EXPERTS, logits, -jnp.inf)
    m1 = l1.max(axis=-1, keepdims=True)
    i1 = jnp.where(l1 == m1, lane, LANES).min(axis=-1, keepdims=True)
    l2 = jnp.where(lane == i1, -jnp.inf, l1)
    m2 = l2.max(axis=-1, keepdims=True)
    i2 = jnp.where(l2 == m2, lane, LANES).min(axis=-1, keepdims=True)
    e = jnp.exp(m2 - m1)
    w1 = 1.0 / (1.0 + e)
    w2 = e / (1.0 + e)
    idx_ref[...] = jnp.where(lane == 0, i1, jnp.where(lane == 1, i2, 0))
    wt_ref[...] = jnp.where(lane == 0, w1, jnp.where(lane == 1, w2, 0.0))

    first = lane == i1
    second = lane == i2
    chosen = (first | second).astype(bf16)
    earlier = (lax.broadcasted_iota(jnp.int32, (tm, tm), 1) < lax.broadcasted_iota(jnp.int32, (tm, tm), 0))
    before = _mm(earlier.astype(bf16), chosen) + seen[...]
    r1 = jnp.where(first, before, 0.0).sum(axis=-1, keepdims=True)
    r2 = jnp.where(second, before, 0.0).sum(axis=-1, keepdims=True)
    rank_ref[...] = jnp.where(lane == 0, r1, jnp.where(lane == 1, r2, 0.0)).astype(jnp.int32)
    seen[...] += chosen.astype(f32).sum(axis=0, keepdims=True)
    cnt_ref[...] = seen[...].astype(jnp.int32)


def _router(x, mod, norm, w_router_padded, layer):
    tm = 512
    jl = layer // 2
    lane_block = pl.BlockSpec((tm, LANES), lambda i: (i, 0))
    return pl.pallas_call(
        _router_kernel,
        grid=(T // tm,),
        in_specs=[
            pl.BlockSpec((tm, D), lambda i: (i, 0)),
            pl.BlockSpec((None, 1, D), lambda i: (layer, 0, 0)),
            _mod_spec(layer, 4, tm), _mod_spec(layer, 3, tm),
            pl.BlockSpec((None, D, LANES), lambda i: (jl, 0, 0)),
        ],
        out_specs=[pl.BlockSpec((tm, D // 2), lambda i: (i, 0)), lane_block, lane_block, lane_block,
                   pl.BlockSpec((1, LANES), lambda i: (0, 0))],
        out_shape=[jax.ShapeDtypeStruct((T, D // 2), jnp.uint32),
                   jax.ShapeDtypeStruct((T, LANES), jnp.int32),
                   jax.ShapeDtypeStruct((T, LANES), f32),
                   jax.ShapeDtypeStruct((T, LANES), jnp.int32),
                   jax.ShapeDtypeStruct((1, LANES), jnp.int32)],
        scratch_shapes=[pltpu.VMEM((1, LANES), f32)],
        compiler_params=_params(("arbitrary",),
                                _nbytes((tm, D), f32) + _nbytes((tm, D), bf16) + 3 * _nbytes((tm, LANES), f32),
                                _nbytes((D, LANES), f32) + 4 * _nbytes((tm, D), f32) + 2 * _nbytes((tm, tm), f32)),
        name="router",
    )(x, norm.reshape(DEPTH, 1, D), mod, mod, w_router_padded)


def _tile_table(tm, group_start, group_end, counts):
    experts = jnp.arange(N_EXPERTS, dtype=jnp.int32)
    tile_row0 = jnp.arange(MOE_ROWS // tm, dtype=jnp.int32) * tm
    tile_expert = jnp.sum((tile_row0[:, None] >= group_end[None, :]).astype(jnp.int32), axis=1)
    tile_expert = jnp.minimum(tile_expert, N_EXPERTS - 1)
    rows_end = jnp.sum(jnp.where(tile_expert[:, None] == experts[None, :], (group_start + counts)[None, :], 0), axis=1)
    rows_here = rows_end - tile_row0
    fill = jnp.where(rows_here <= 0, TILE_EMPTY, jnp.where(rows_here <= tm // 2, TILE_FIRST_HALF, TILE_FULL))
    return tile_expert.astype(jnp.int32), fill.astype(jnp.int32)


def _routing_plan(expert_ids, ranks, counts):
    e_flat = expert_ids.reshape(-1)
    span = ((counts + MOE_UP_TM - 1) // MOE_UP_TM) * MOE_UP_TM
    group_end = jnp.cumsum(span)
    group_start = group_end - span
    onehot = e_flat[:, None] == jnp.arange(N_EXPERTS, dtype=jnp.int32)[None, :]
    dest = ranks.reshape(-1) + jnp.sum(jnp.where(onehot, group_start[None, :], 0), axis=1)
    src_token = jnp.zeros((MOE_ROWS,), jnp.int32).at[dest].set(
        jnp.arange(N_ASSIGN, dtype=jnp.int32) // TOP_K, unique_indices=True)
    up_tiles = _tile_table(MOE_UP_TM, group_start, group_end, counts)
    down_tiles = _tile_table(MOE_DOWN_TM, group_start, group_end, counts)
    return src_token, dest.astype(jnp.int32), up_tiles, down_tiles


def _dispatch_kernel(src_ref, fill_ref, hp_hbm, o_ref, buf, sem):
    i = pl.program_id(0)
    n_steps = pl.num_programs(0)
    chunk = o_ref.shape[0]

    def tile_copies(tile, slot):
        return [pltpu.make_async_copy(hp_hbm.at[pl.ds(src_ref[tile * chunk + r], 1), :],
                                      buf.at[slot, pl.ds(r, 1), :], sem.at[slot]) for r in range(chunk)]

    def start_tile(tile, slot):
        for cp in tile_copies(tile, slot):
            cp.start()

    @pl.when((i == 0) & (fill_ref[0] != TILE_EMPTY))
    def _():
        start_tile(0, 0)

    @pl.when((i + 1 < n_steps) & (fill_ref[jnp.minimum(i + 1, n_steps - 1)] != TILE_EMPTY))
    def _():
        start_tile(i + 1, (i + 1) % 2)

    @pl.when(fill_ref[i] != TILE_EMPTY)
    def _():
        slot = i % 2
        for cp in tile_copies(i, slot):
            cp.wait()
        lo, hi = _unpack_bf16_halves(buf[slot])
        o_ref[:, :D // 2] = lo
        o_ref[:, D // 2:] = hi

    @pl.when(fill_ref[i] == TILE_EMPTY)
    def _():
        o_ref[...] = jnp.zeros(o_ref.shape, o_ref.dtype)


def _dispatch(h_packed, src_token, tile_fill):
    chunk = MOE_DOWN_TM
    return pl.pallas_call(
        _dispatch_kernel,
        grid_spec=pltpu.PrefetchScalarGridSpec(
            num_scalar_prefetch=2,
            grid=(MOE_ROWS // chunk,),
            in_specs=[pl.BlockSpec(memory_space=pl.ANY)],
            out_specs=pl.BlockSpec((chunk, D), lambda i, src, fill: (i, 0)),
            scratch_shapes=[pltpu.VMEM((2, chunk, D // 2), jnp.uint32), pltpu.SemaphoreType.DMA((2,))],
        ),
        out_shape=jax.ShapeDtypeStruct((MOE_ROWS, D), bf16),
        compiler_params=_params(("arbitrary",), _nbytes((chunk, D), bf16), 5 * _nbytes((chunk, D // 2), f32)),
        name="moe_dispatch",
    )(src_token, tile_fill, h_packed)


def _expert_changed(te_ref, i):
    return (i == 0) | (te_ref[i] != te_ref[jnp.maximum(i - 1, 0)])


def _for_tile_rows(i, fill_ref, o_ref, rows_fn):
    tm = o_ref.shape[0]
    fill = fill_ref[i]

    @pl.when(fill == TILE_FULL)
    def _():
        o_ref[...] = rows_fn(slice(None))

    @pl.when(fill == TILE_FIRST_HALF)
    def _():
        o_ref[:tm // 2] = rows_fn(slice(0, tm // 2))
        o_ref[tm // 2:] = jnp.zeros((tm // 2, o_ref.shape[1]), o_ref.dtype)

    @pl.when(fill == TILE_EMPTY)
    def _():
        o_ref[...] = jnp.zeros(o_ref.shape, o_ref.dtype)


def _moe_up_kernel(te_ref, fill_ref, hs_ref, wg_ref, wu_ref, a_ref, wg_s, wu_s):
    i = pl.program_id(1)

    @pl.when(_expert_changed(te_ref, i))
    def _():
        wg_s[...] = wg_ref[...].astype(bf16)
        wu_s[...] = wu_ref[...].astype(bf16)

    def gated(rows):
        h = hs_ref[rows]
        g = _mm(h, wg_s[...])
        u = _mm(h, wu_s[...])
        return ((g * jax.nn.sigmoid(g)) * u).astype(bf16)

    _for_tile_rows(i, fill_ref, a_ref, gated)


def _moe_up(hs, w_gate, w_up, tile_expert, tile_fill, layer):
    tm, tn = MOE_UP_TM, 512
    jl = layer // 2
    return pl.pallas_call(
        _moe_up_kernel,
        grid_spec=pltpu.PrefetchScalarGridSpec(
            num_scalar_prefetch=2,
            grid=(D_FF // tn, MOE_ROWS // tm),
            in_specs=[
                pl.BlockSpec((tm, D), lambda j, i, te, fill: (i, 0)),
                pl.BlockSpec((None, None, D, tn), lambda j, i, te, fill: (jl, te[i], 0, j)),
                pl.BlockSpec((None, None, D, tn), lambda j, i, te, fill: (jl, te[i], 0, j)),
            ],
            out_specs=pl.BlockSpec((tm, tn), lambda j, i, te, fill: (i, j)),
            scratch_shapes=[pltpu.VMEM((D, tn), bf16), pltpu.VMEM((D, tn), bf16)],
        ),
        out_shape=jax.ShapeDtypeStruct((MOE_ROWS, D_FF), bf16),
        compiler_params=_params(("arbitrary", "arbitrary"),
                                _nbytes((tm, D), bf16) + 2 * _nbytes((D, tn), f32) + _nbytes((tm, tn), bf16),
                                2 * _nbytes((D, tn), bf16) + 3 * _nbytes((tm, tn), f32)),
        name="moe_up",
    )(tile_expert, tile_fill, hs, w_gate, w_up)


def _moe_down_kernel(te_ref, fill_ref, a_ref, wd_ref, y_ref, wd_s):
    i = pl.program_id(1)

    @pl.when(_expert_changed(te_ref, i))
    def _():
        wd_s[...] = wd_ref[...].astype(bf16)

    _for_tile_rows(i, fill_ref, y_ref, lambda rows: _mm(a_ref[rows], wd_s[...]))


def _moe_down(a, w_down, tile_expert, tile_fill, layer):
    tm, tn = MOE_DOWN_TM, 512
    jl = layer // 2
    return pl.pallas_call(
        _moe_down_kernel,
        grid_spec=pltpu.PrefetchScalarGridSpec(
            num_scalar_prefetch=2,
            grid=(D // tn, MOE_ROWS // tm),
            in_specs=[
                pl.BlockSpec((tm, D_FF), lambda j, i, te, fill: (i, 0)),
                pl.BlockSpec((None, None, D_FF, tn), lambda j, i, te, fill: (jl, te[i], 0, j)),
            ],
            out_specs=pl.BlockSpec((tm, tn), lambda j, i, te, fill: (i, j)),
            scratch_shapes=[pltpu.VMEM((D_FF, tn), bf16)],
        ),
        out_shape=jax.ShapeDtypeStruct((MOE_ROWS, D), f32),
        compiler_params=_params(("arbitrary", "arbitrary"),
                                _nbytes((tm, D_FF), bf16) + _nbytes((D_FF, tn), f32) + _nbytes((tm, tn), f32),
                                _nbytes((D_FF, tn), bf16) + _nbytes((tm, tn), f32)),
        name="moe_down",
    )(tile_expert, tile_fill, a, w_down)


def _combine_kernel(pos_ref, x_ref, gate_ref, wt_ref, y_hbm, o_ref, ybuf, sem, *, tm):
    i = pl.program_id(0)
    n_steps = pl.num_programs(0)

    def tile_copies(tile, slot):
        return [pltpu.make_async_copy(y_hbm.at[pl.ds(pos_ref[(tile * tm + t) * TOP_K + c], 1), :],
                                      ybuf.at[slot, c, pl.ds(t, 1), :], sem.at[slot])
                for t in range(tm) for c in range(TOP_K)]

    def start_tile(tile, slot):
        for cp in tile_copies(tile, slot):
            cp.start()

    @pl.when(i == 0)
    def _():
        start_tile(0, 0)

    @pl.when(i + 1 < n_steps)
    def _():
        start_tile(i + 1, (i + 1) % 2)

    slot = i % 2
    for cp in tile_copies(i, slot):
        cp.wait()
    w1 = jnp.broadcast_to(wt_ref[:, 0:1], (tm, LANES))
    w2 = jnp.broadcast_to(wt_ref[:, 1:2], (tm, LANES))
    for s in range(ROW_SLABS):
        sl = slice(s * LANES, (s + 1) * LANES)
        mix = w1 * ybuf[slot, 0, :, sl] + w2 * ybuf[slot, 1, :, sl]
        o_ref[:, sl] = x_ref[:, sl] + gate_ref[:, sl] * mix


def _combine(x, y_slabs, pos, wts, mod, layer):
    tm = 256
    return pl.pallas_call(
        functools.partial(_combine_kernel, tm=tm),
        grid_spec=pltpu.PrefetchScalarGridSpec(
            num_scalar_prefetch=1,
            grid=(T // tm,),
            in_specs=[
                pl.BlockSpec((tm, D), lambda i, pos: (i, 0)),
                _mod_spec(layer, 5, tm),
                pl.BlockSpec((tm, LANES), lambda i, pos: (i, 0)),
                pl.BlockSpec(memory_space=pl.ANY),
            ],
            out_specs=pl.BlockSpec((tm, D), lambda i, pos: (i, 0)),
            scratch_shapes=[pltpu.VMEM((2, TOP_K, tm, D), f32),
                            pltpu.SemaphoreType.DMA((2,))],
        ),
        out_shape=jax.ShapeDtypeStruct((T, D), f32),
        compiler_params=_params(("arbitrary",),
                                2 * _nbytes((tm, D), f32) + _nbytes((tm, LANES), f32),
                                2 * _nbytes((TOP_K, tm, D), f32) + 4 * _nbytes((tm, LANES), f32)),
        name="moe_combine",
    )(pos, x, mod, wts, y_slabs)


def _moe_layer(x, mod, norm2, w_router, w_gate, w_up, w_down, layer):
    wr = jnp.pad(w_router, ((0, 0), (0, 0), (0, LANES - N_EXPERTS)))
    h_packed, idx, wts, ranks, counts = _router(x, mod, norm2, wr, layer)
    src_token, pos, up_tiles, down_tiles = _routing_plan(idx[:, :TOP_K], ranks[:, :TOP_K], counts[0, :N_EXPERTS])
    hs = _dispatch(h_packed, src_token, down_tiles[1])
    a = _moe_up(hs, w_gate, w_up, *up_tiles, layer)
    y = _moe_down(a, w_down, *down_tiles, layer)
    return _combine(x, y, pos, wts, mod, layer)


def kernel(x_prompt, x_sample, cache_k, cache_v, c, c_ctx, norm1, norm2, w_ada, b_ada, w_conv_in, conv_w,
           w_conv_out, w_qkv, q_norm, k_norm, rpb, w_attn_out, w_ffn_gate, w_ffn_up, w_ffn_down, w_router,
           w_moe_gate, w_moe_up, w_moe_down):
    x = jnp.concatenate([x_prompt.reshape(CTX_ROWS, D), x_sample.reshape(DEC_ROWS, D)], axis=0)
    cond = jnp.concatenate([c_ctx[None, :], c, jnp.zeros((N_GROUPS_PADDED - 1 - N_DEC_SEQ, D), f32)], axis=0)
    mod = _ada_params(cond, w_ada, b_ada)
    cache_k2 = cache_k.reshape(N_DEC_SEQ, -1, PAST_LEN, D)
    cache_v2 = cache_v.reshape(N_DEC_SEQ, -1, PAST_LEN, D)
    w_conv_in, w_conv_out, w_qkv, w_attn_out, w_ffn_gate, w_ffn_up, w_ffn_down = (
        w.astype(bf16) for w in (w_conv_in, w_conv_out, w_qkv, w_attn_out, w_ffn_gate, w_ffn_up, w_ffn_down))

    new_k, new_v = [], []
    for layer in range(DEPTH):
        jl = layer // 2
        if layer % 2 == 0:
            b, u = _conv_in(x, mod, norm1, w_conv_in, layer)
            gated = _conv_gate(b, u, conv_w, layer)
            x = _mm_residual([gated], w_conv_out, x, mod, layer, 2, 1024, 1024, "conv_out")
            a = _ffn_up(x, mod, norm2, w_ffn_gate, w_ffn_up, layer)
            x = _mm_residual([a], w_ffn_down, x, mod, layer, 5, 1024, 512, "ffn_down")
        else:
            qkv_ctx = _qkv(x, mod, norm1, w_qkv, q_norm, k_norm, layer, 0, CTX_ROWS, f32)
            qkv_dec = _qkv(x, mod, norm1, w_qkv, q_norm, k_norm, layer, CTX_ROWS, DEC_ROWS, bf16)
            o_ctx, k_ctx, v_ctx = _ctx_attention(qkv_ctx)
            new_k.append(k_ctx.reshape(N_CTX_SEQ, CTX_LEN, N_HEADS, HEAD_DIM))
            new_v.append(v_ctx.reshape(N_CTX_SEQ, CTX_LEN, N_HEADS, HEAD_DIM))
            o_dec = _latent_attention(qkv_dec, cache_k2, cache_v2, jl, _rel_bias_blocks(rpb[jl]))
            x = _mm_residual([o_ctx, o_dec], w_attn_out, x, mod, layer, 2, 1024, 1024, "attn_out")
            x = _moe_layer(x, mod, norm2, w_router, w_moe_gate, w_moe_up, w_moe_down, layer)

    y_prompt = x[:CTX_ROWS].reshape(N_CTX_SEQ, CTX_LEN, D)
    y_sample = x[CTX_ROWS:].reshape(N_DEC_SEQ, DEC_LEN, D)
    return y_prompt, y_sample, jnp.stack(new_k, axis=1), jnp.stack(new_v, axis=1)
```

```python
import functools

import numpy as np
import jax
import jax.numpy as jnp
from jax import lax
from jax.experimental import pallas as pl
from jax.experimental.pallas import tpu as pltpu

D = 2048
N_CTX_SEQ = 16
CTX_LEN = 256
N_DEC_SEQ = 4
DEC_LEN = 2048
CTX_ROWS = N_CTX_SEQ * CTX_LEN
DEC_ROWS = N_DEC_SEQ * DEC_LEN
T = CTX_ROWS + DEC_ROWS
DEPTH = 4
PAST_LEN = 512
N_HEADS = 16
HEAD_DIM = 128
GRID_W = 64
GRID_ROWS = DEC_LEN // GRID_W
WIN_ROWS = 8
WIN_COLS = 16
D_FF = 5632
N_EXPERTS = 8
TOP_K = 2
EPS = 1e-6
NEG_INF = -1e30
ATTN_SCALE = HEAD_DIM ** -0.5
LOG2_E = 1.4426950408889634
SCORE_SCALE_LOG2 = ATTN_SCALE * LOG2_E

N_GROUPS_PADDED = 8
LANES = 128
SUBLANES = 8
MXU_COLS = 256
VMEM_CAP_BYTES = 56 * 2 ** 20
VMEM_HEADROOM = 4 * 2 ** 20

Q_ROWS_PER_BLOCK = 4
Q_BLOCK = Q_ROWS_PER_BLOCK * GRID_W
N_Q_BLOCKS = GRID_ROWS // Q_ROWS_PER_BLOCK
BAND_ROWS = Q_ROWS_PER_BLOCK + WIN_ROWS
BAND = BAND_ROWS * GRID_W
MASKED_BLOCK = 2 * WIN_ROWS - 1
N_BIAS_BLOCKS = MASKED_BLOCK + 1

MOE_UP_TM = 1024
MOE_DOWN_TM = 512
N_ASSIGN = T * TOP_K
MOE_ROWS = N_ASSIGN + N_EXPERTS * MOE_UP_TM
TILE_EMPTY, TILE_FIRST_HALF, TILE_FULL = 0, 1, 2
ROW_SLABS = D // LANES

bf16 = jnp.bfloat16
f32 = jnp.float32


def _mm(a, b):
    return jnp.dot(a, b, preferred_element_type=f32)


def _mm_nt(a, b):
    return lax.dot_general(a, b, (((1,), (1,)), ((), ())), preferred_element_type=f32)


def _nbytes(shape, dtype):
    return int(np.prod(shape)) * jnp.dtype(dtype).itemsize


def _params(semantics, pipelined_bytes, resident_bytes):
    need = 2 * pipelined_bytes + resident_bytes + VMEM_HEADROOM
    return pltpu.CompilerParams(dimension_semantics=semantics,
                                vmem_limit_bytes=int(min(max(need, 16 * 2 ** 20), VMEM_CAP_BYTES)))


def _group_of_tile(i, tm):
    n_ctx = CTX_ROWS // tm
    per_seq = DEC_LEN // tm
    return jnp.where(i < n_ctx, 0, 1 + (i - n_ctx) // per_seq)


def _mod_spec(layer, which, tm, width=D, tile0=0):
    def index(*grid_and_prefetch):
        i = tile0 + grid_and_prefetch[0]
        j = grid_and_prefetch[1] if width != D else 0
        return ((layer * N_GROUPS_PADDED + _group_of_tile(i, tm)) * 6 + which, 0, j)
    return pl.BlockSpec((None, 1, width), index)


def _norm_mod_rows(x, gain, scale1p, shift):
    ms = jnp.mean(x * x, axis=-1, keepdims=True)
    y = x * lax.rsqrt(ms + EPS)
    return (y * gain) * scale1p + shift


def _norm_mod_to_scratch(x_ref, gain_ref, sc_ref, sh_ref, h_scr, rows=256):
    gain = gain_ref[...]
    scale1p = 1.0 + sc_ref[...]
    shift = sh_ref[...]

    def body(c, carry):
        r = pl.multiple_of(c * rows, rows)
        h = _norm_mod_rows(x_ref[pl.ds(r, rows), :], gain, scale1p, shift)
        h_scr[pl.ds(r, rows), :] = h.astype(bf16)
        return carry

    lax.fori_loop(0, x_ref.shape[0] // rows, body, 0)


def _ada_kernel(cond_ref, w_ref, b_ref, o_ref):
    c = cond_ref[...]
    s = (c * jax.nn.sigmoid(c)).astype(bf16)
    o_ref[...] = _mm(s, w_ref[...].astype(bf16)) + b_ref[...]


def _ada_params(cond, w_ada, b_ada):
    tn = 1024
    out = pl.pallas_call(
        _ada_kernel,
        grid=(DEPTH, 6 * D // tn),
        in_specs=[
            pl.BlockSpec((N_GROUPS_PADDED, D), lambda l, j: (0, 0)),
            pl.BlockSpec((None, D, tn), lambda l, j: (l, 0, j)),
            pl.BlockSpec((None, 1, tn), lambda l, j: (l, 0, j)),
        ],
        out_specs=pl.BlockSpec((None, N_GROUPS_PADDED, tn), lambda l, j: (l, 0, j)),
        out_shape=jax.ShapeDtypeStruct((DEPTH, N_GROUPS_PADDED, 6 * D), f32),
        compiler_params=_params(("arbitrary", "arbitrary"),
                                _nbytes((D, tn), f32) + _nbytes((8, tn), f32) * 2,
                                _nbytes((D, tn), bf16) + _nbytes((8, D), f32) * 2),
        name="ada_params",
    )(cond, w_ada, b_ada.reshape(DEPTH, 1, 6 * D))
    return out.reshape(DEPTH * N_GROUPS_PADDED * 6, 1, D)


def _conv_in_kernel(x_ref, gain_ref, sc_ref, sh_ref, wb_ref, wc_ref, wv_ref, b_ref, u_ref, h_scr):
    @pl.when(pl.program_id(1) == 0)
    def _():
        _norm_mod_to_scratch(x_ref, gain_ref, sc_ref, sh_ref, h_scr)

    h = h_scr[...]
    b_ref[...] = _mm(h, wb_ref[...])
    u_ref[...] = _mm(h, wc_ref[...]) * _mm(h, wv_ref[...])


def _conv_in(x, mod, norm, w_conv_in, layer):
    tm, tn = 1024, 256
    jl = layer // 2
    nb = D // tn
    return pl.pallas_call(
        _conv_in_kernel,
        grid=(T // tm, nb),
        in_specs=[
            pl.BlockSpec((tm, D), lambda i, j: (i, 0)),
            pl.BlockSpec((None, 1, D), lambda i, j: (layer, 0, 0)),
            _mod_spec(layer, 1, tm), _mod_spec(layer, 0, tm),
            pl.BlockSpec((None, D, tn), lambda i, j: (jl, 0, j)),
            pl.BlockSpec((None, D, tn), lambda i, j: (jl, 0, nb + j)),
            pl.BlockSpec((None, D, tn), lambda i, j: (jl, 0, 2 * nb + j)),
        ],
        out_specs=[pl.BlockSpec((tm, tn), lambda i, j: (i, j))] * 2,
        out_shape=[jax.ShapeDtypeStruct((T, D), f32)] * 2,
        scratch_shapes=[pltpu.VMEM((tm, D), bf16)],
        compiler_params=_params(("arbitrary", "arbitrary"),
                                _nbytes((tm, D), f32) + 3 * _nbytes((D, tn), bf16) + 2 * _nbytes((tm, tn), f32),
                                _nbytes((tm, D), bf16) + 3 * _nbytes((tm, tn), f32)),
        name="conv_in",
    )(x, norm.reshape(DEPTH, 1, D), mod, mod, w_conv_in, w_conv_in, w_conv_in)


def _ffn_up_kernel(x_ref, gain_ref, sc_ref, sh_ref, wg_ref, wu_ref, a_ref, h_scr):
    @pl.when(pl.program_id(1) == 0)
    def _():
        _norm_mod_to_scratch(x_ref, gain_ref, sc_ref, sh_ref, h_scr)

    h = h_scr[...]
    g = _mm(h, wg_ref[...])
    u = _mm(h, wu_ref[...])
    a_ref[...] = ((g * jax.nn.sigmoid(g)) * u).astype(bf16)


def _ffn_up(x, mod, norm, w_gate, w_up, layer):
    tm, tn = 1024, 512
    jl = layer // 2
    return pl.pallas_call(
        _ffn_up_kernel,
        grid=(T // tm, D_FF // tn),
        in_specs=[
            pl.BlockSpec((tm, D), lambda i, j: (i, 0)),
            pl.BlockSpec((None, 1, D), lambda i, j: (layer, 0, 0)),
            _mod_spec(layer, 4, tm), _mod_spec(layer, 3, tm),
            pl.BlockSpec((None, D, tn), lambda i, j: (jl, 0, j)),
            pl.BlockSpec((None, D, tn), lambda i, j: (jl, 0, j)),
        ],
        out_specs=pl.BlockSpec((tm, tn), lambda i, j: (i, j)),
        out_shape=jax.ShapeDtypeStruct((T, D_FF), bf16),
        scratch_shapes=[pltpu.VMEM((tm, D), bf16)],
        compiler_params=_params(("arbitrary", "arbitrary"),
                                _nbytes((tm, D), f32) + 2 * _nbytes((D, tn), bf16) + _nbytes((tm, tn), bf16),
                                _nbytes((tm, D), bf16) + 3 * _nbytes((tm, tn), f32)),
        name="ffn_up",
    )(x, norm.reshape(DEPTH, 1, D), mod, mod, w_gate, w_up)


def _qkv_kernel(x_ref, gain_ref, sc_ref, sh_ref, w_ref, qg_ref, kg_ref, o_ref, h_scr, *, n_q_blocks):
    j = pl.program_id(1)

    @pl.when(j == 0)
    def _():
        _norm_mod_to_scratch(x_ref, gain_ref, sc_ref, sh_ref, h_scr)

    is_qk = j < 2 * n_q_blocks
    head_gain = jnp.where(j < n_q_blocks, qg_ref[...], kg_ref[...])
    h = h_scr[...]
    for c in range(0, o_ref.shape[1], MXU_COLS):
        acc = _mm(h, w_ref[:, c:c + MXU_COLS])
        for hh in range(MXU_COLS // HEAD_DIM):
            a = acc[:, hh * HEAD_DIM:(hh + 1) * HEAD_DIM]
            ms = jnp.mean(a * a, axis=-1, keepdims=True)
            normed = (a * lax.rsqrt(ms + EPS)) * head_gain
            cols = slice(c + hh * HEAD_DIM, c + (hh + 1) * HEAD_DIM)
            o_ref[:, cols] = jnp.where(is_qk, normed, a).astype(o_ref.dtype)


def _qkv(x, mod, norm, w_qkv, q_norm, k_norm, layer, row0, n_rows, out_dtype):
    tm, tn = 1024, 1024
    jl = layer // 2
    tile0 = row0 // tm
    return pl.pallas_call(
        functools.partial(_qkv_kernel, n_q_blocks=D // tn),
        grid=(n_rows // tm, 3 * D // tn),
        in_specs=[
            pl.BlockSpec((tm, D), lambda i, j: (tile0 + i, 0)),
            pl.BlockSpec((None, 1, D), lambda i, j: (layer, 0, 0)),
            _mod_spec(layer, 1, tm, tile0=tile0), _mod_spec(layer, 0, tm, tile0=tile0),
            pl.BlockSpec((None, D, tn), lambda i, j: (jl, 0, j)),
            pl.BlockSpec((None, 1, HEAD_DIM), lambda i, j: (jl, 0, 0)),
            pl.BlockSpec((None, 1, HEAD_DIM), lambda i, j: (jl, 0, 0)),
        ],
        out_specs=pl.BlockSpec((tm, tn), lambda i, j: (i, j)),
        out_shape=jax.ShapeDtypeStruct((n_rows, 3 * D), out_dtype),
        scratch_shapes=[pltpu.VMEM((tm, D), bf16)],
        compiler_params=_params(("arbitrary", "arbitrary"),
                                _nbytes((tm, D), f32) + _nbytes((D, tn), bf16) + _nbytes((tm, tn), out_dtype),
                                _nbytes((tm, D), bf16) + 2 * _nbytes((tm, tn), f32)),
        name="qkv",
    )(x, norm.reshape(DEPTH, 1, D), mod, mod, w_qkv,
      q_norm.reshape(-1, 1, HEAD_DIM), k_norm.reshape(-1, 1, HEAD_DIM))


def _mm_res_kernel(*refs, n_ctx_tiles):
    *a_refs, w_ref, x_ref, gate_ref, o_ref = refs

    def update(a_ref):
        o_ref[...] = x_ref[...] + gate_ref[...] * _mm(a_ref[...], w_ref[...])

    if len(a_refs) == 1:
        update(a_refs[0])
    else:
        i = pl.program_id(0)

        @pl.when(i < n_ctx_tiles)
        def _():
            update(a_refs[0])

        @pl.when(i >= n_ctx_tiles)
        def _():
            update(a_refs[1])


def _mm_residual(a_parts, w, x, mod, layer, which, tm, tn, name):
    k = a_parts[0].shape[1]
    jl = layer // 2
    n_ctx_tiles = CTX_ROWS // tm
    if len(a_parts) == 1:
        a_specs = [pl.BlockSpec((tm, k), lambda i, j: (i, 0))]
    else:
        a_specs = [pl.BlockSpec((tm, k), lambda i, j: (jnp.minimum(i, n_ctx_tiles - 1), 0)),
                   pl.BlockSpec((tm, k), lambda i, j: (jnp.maximum(i - n_ctx_tiles, 0), 0))]
    return pl.pallas_call(
        functools.partial(_mm_res_kernel, n_ctx_tiles=n_ctx_tiles),
        grid=(T // tm, D // tn),
        in_specs=a_specs + [
            pl.BlockSpec((None, k, tn), lambda i, j: (jl, 0, j)),
            pl.BlockSpec((tm, tn), lambda i, j: (i, j)),
            _mod_spec(layer, which, tm, tn),
        ],
        out_specs=pl.BlockSpec((tm, tn), lambda i, j: (i, j)),
        out_shape=jax.ShapeDtypeStruct((T, D), f32),
        compiler_params=_params(("arbitrary", "arbitrary"),
                                len(a_parts) * _nbytes((tm, k), bf16) + _nbytes((k, tn), bf16)
                                + 2 * _nbytes((tm, tn), f32),
                                2 * _nbytes((tm, tn), f32)),
        name=name,
    )(*a_parts, w, x, mod)


def _conv_gate_kernel(b_ref, u_ref, up_ref, un_ref, cw_ref, o_ref, *, tm, cols):
    row0 = pl.program_id(0) * tm
    seq_len = jnp.where(row0 < CTX_ROWS, CTX_LEN, DEC_LEN)
    local = lax.broadcasted_iota(jnp.int32, (tm, cols), 0)
    pos = (row0 + local) & (seq_len - 1)
    is_first = pos == 0
    is_last = pos == seq_len - 1
    for c in range(D // cols):
        sl = slice(c * cols, (c + 1) * cols)
        u = u_ref[:, sl]
        prev_row = up_ref[SUBLANES - 1:SUBLANES, sl]
        next_row = un_ref[0:1, sl]
        u_prev = jnp.where(local == 0, prev_row, pltpu.roll(u, 1, axis=0))
        u_next = jnp.where(local == tm - 1, next_row, pltpu.roll(u, tm - 1, axis=0))
        u_prev = jnp.where(is_first, 0.0, u_prev)
        u_next = jnp.where(is_last, 0.0, u_next)
        conv = u_prev * cw_ref[0:1, sl] + u * cw_ref[1:2, sl] + u_next * cw_ref[2:3, sl]
        o_ref[:, sl] = (b_ref[:, sl] * conv).astype(bf16)


def _conv_gate(b, u, conv_w, layer):
    tm, cols = 512, 512
    jl = layer // 2
    blocks_per_tile = tm // SUBLANES
    return pl.pallas_call(
        functools.partial(_conv_gate_kernel, tm=tm, cols=cols),
        grid=(T // tm,),
        in_specs=[
            pl.BlockSpec((tm, D), lambda i: (i, 0)),
            pl.BlockSpec((tm, D), lambda i: (i, 0)),
            pl.BlockSpec((SUBLANES, D), lambda i: (jnp.maximum(i * blocks_per_tile - 1, 0), 0)),
            pl.BlockSpec((SUBLANES, D), lambda i: (jnp.minimum((i + 1) * blocks_per_tile, T // SUBLANES - 1), 0)),
            pl.BlockSpec((None, 3, D), lambda i: (jl, 0, 0)),
        ],
        out_specs=pl.BlockSpec((tm, D), lambda i: (i, 0)),
        out_shape=jax.ShapeDtypeStruct((T, D), bf16),
        compiler_params=_params(("arbitrary",),
                                2 * _nbytes((tm, D), f32) + _nbytes((tm, D), bf16) + 2 * _nbytes((SUBLANES, D), f32),
                                8 * _nbytes((tm, cols), f32)),
        name="conv_gate",
    )(b, u, u, u, conv_w)


def _softmax_pv(parts):
    m = parts[0][0].max(axis=-1, keepdims=True)
    for s, _ in parts[1:]:
        m = jnp.maximum(m, s.max(axis=-1, keepdims=True))
    acc = None
    for s, v in parts:
        p = jnp.exp2(s - m).astype(bf16)
        v_and_ones = jnp.concatenate([v, jnp.ones_like(v)], axis=1)
        o = _mm(p, v_and_ones)
        acc = o if acc is None else acc + o
    head_dim = acc.shape[1] // 2
    return acc[:, :head_dim] * (1.0 / acc[:, head_dim:head_dim + 1])


def _ctx_attn_kernel(q_ref, k_ref, v_ref, o_ref, k_out_ref, v_out_ref):
    k_out_ref[...] = k_ref[...]
    v_out_ref[...] = v_ref[...]
    for hh in range(q_ref.shape[1] // HEAD_DIM):
        sl = slice(hh * HEAD_DIM, (hh + 1) * HEAD_DIM)
        q = q_ref[:, sl].astype(bf16)
        k = k_ref[:, sl].astype(bf16)
        v = v_ref[:, sl].astype(bf16)
        s = _mm_nt(q, k) * SCORE_SCALE_LOG2
        o_ref[:, sl] = _softmax_pv([(s, v)]).astype(bf16)


def _ctx_attention(qkv):
    tn = 512
    nb = D // tn
    return pl.pallas_call(
        _ctx_attn_kernel,
        grid=(N_CTX_SEQ, nb),
        in_specs=[
            pl.BlockSpec((CTX_LEN, tn), lambda b, g: (b, g)),
            pl.BlockSpec((CTX_LEN, tn), lambda b, g: (b, nb + g)),
            pl.BlockSpec((CTX_LEN, tn), lambda b, g: (b, 2 * nb + g)),
        ],
        out_specs=[pl.BlockSpec((CTX_LEN, tn), lambda b, g: (b, g))] * 3,
        out_shape=[jax.ShapeDtypeStruct((CTX_ROWS, D), bf16),
                   jax.ShapeDtypeStruct((CTX_ROWS, D), f32),
                   jax.ShapeDtypeStruct((CTX_ROWS, D), f32)],
        compiler_params=_params(("arbitrary", "arbitrary"),
                                5 * _nbytes((CTX_LEN, tn), f32) + _nbytes((CTX_LEN, tn), bf16),
                                8 * _nbytes((CTX_LEN, CTX_LEN), f32)),
        name="ctx_attention",
    )(qkv, qkv, qkv)


def _band_start_row(m):
    return jnp.clip(Q_ROWS_PER_BLOCK * m - WIN_ROWS // 2, 0, GRID_ROWS - BAND_ROWS)


def _band_block_ids(m):
    band0 = _band_start_row(m)
    ids = []
    for rho in range(Q_ROWS_PER_BLOCK):
        r = Q_ROWS_PER_BLOCK * m + rho
        win0 = jnp.clip(r - WIN_ROWS // 2, 0, GRID_ROWS - WIN_ROWS)
        row_ids = []
        for kap in range(BAND_ROWS):
            kr = band0 + kap
            inside = (kr >= win0) & (kr < win0 + WIN_ROWS)
            row_ids.append(jnp.where(inside, kr - r + WIN_ROWS - 1, MASKED_BLOCK))
        ids.append(row_ids)
    return ids


def _band_bias(bias_ref, hh, ids):
    left_half = lax.broadcasted_iota(jnp.int32, (GRID_W, 2 * GRID_W), 1) < GRID_W
    rows = []
    for rho in range(Q_ROWS_PER_BLOCK):
        pairs = [jnp.where(left_half, bias_ref[hh, ids[rho][kap]], bias_ref[hh, ids[rho][kap + 1]])
                 for kap in range(0, BAND_ROWS, 2)]
        rows.append(jnp.concatenate(pairs, axis=1))
    return jnp.concatenate(rows, axis=0)


def _loc_attn_kernel(q_ref, k_ref, v_ref, ck_ref, cv_ref, bias_ref, o_ref):
    start = pl.multiple_of(_band_start_row(pl.program_id(2)) * GRID_W, GRID_W)
    ids = _band_block_ids(pl.program_id(2))
    for hh in range(q_ref.shape[1] // HEAD_DIM):
        sl = slice(hh * HEAD_DIM, (hh + 1) * HEAD_DIM)
        q = q_ref[:, sl]
        k_band = k_ref[pl.ds(start, BAND), sl]
        v_band = v_ref[pl.ds(start, BAND), sl]
        k_ctx = ck_ref[:, sl].astype(bf16)
        v_ctx = cv_ref[:, sl].astype(bf16)
        s_loc = _mm_nt(q, k_band) * SCORE_SCALE_LOG2 + _band_bias(bias_ref, hh, ids)
        s_ctx = _mm_nt(q, k_ctx) * SCORE_SCALE_LOG2
        o_ref[:, sl] = _softmax_pv([(s_loc, v_band), (s_ctx, v_ctx)]).astype(bf16)


def _rel_bias_blocks(rpb_all):
    cols = np.arange(GRID_W)
    col_start = np.clip(cols - WIN_COLS // 2, 0, GRID_W - WIN_COLS)
    col_ok = (cols[None, :] >= col_start[:, None]) & (cols[None, :] < col_start[:, None] + WIN_COLS)
    pad = GRID_W - WIN_COLS
    n = rpb_all.shape[0] * rpb_all.shape[1]
    padded = jnp.pad(rpb_all.astype(f32).reshape(n, MASKED_BLOCK, -1), ((0, 0), (0, 0), (pad, pad)))
    skewed = jnp.tile(padded, (1, 1, GRID_W + 1))[:, :, :GRID_W * 2 * GRID_W]
    skewed = skewed.reshape(n, MASKED_BLOCK, GRID_W, 2 * GRID_W)[..., :GRID_W]
    toeplitz = jnp.flip(skewed, axis=2)
    toeplitz = jnp.where(col_ok[None, None], toeplitz * LOG2_E, NEG_INF)
    masked = jnp.full((n, 1, GRID_W, GRID_W), NEG_INF, f32)
    blocks = jnp.concatenate([toeplitz, masked], axis=1)
    return jnp.concatenate([blocks, blocks], axis=3)


def _latent_attention(qkv, cache_k, cache_v, layer_slot, bias_blocks):
    tn = 512
    nb = D // tn
    hb = tn // HEAD_DIM
    return pl.pallas_call(
        _loc_attn_kernel,
        grid=(N_DEC_SEQ, nb, N_Q_BLOCKS),
        in_specs=[
            pl.BlockSpec((Q_BLOCK, tn), lambda b, g, m: (b * N_Q_BLOCKS + m, g)),
            pl.BlockSpec((DEC_LEN, tn), lambda b, g, m: (b, nb + g)),
            pl.BlockSpec((DEC_LEN, tn), lambda b, g, m: (b, 2 * nb + g)),
            pl.BlockSpec((None, None, PAST_LEN, tn), lambda b, g, m: (b, layer_slot, 0, g)),
            pl.BlockSpec((None, None, PAST_LEN, tn), lambda b, g, m: (b, layer_slot, 0, g)),
            pl.BlockSpec((hb, N_BIAS_BLOCKS, GRID_W, 2 * GRID_W), lambda b, g, m: (layer_slot * nb + g, 0, 0, 0)),
        ],
        out_specs=pl.BlockSpec((Q_BLOCK, tn), lambda b, g, m: (b * N_Q_BLOCKS + m, g)),
        out_shape=jax.ShapeDtypeStruct((DEC_ROWS, D), bf16),
        compiler_params=_params(("arbitrary", "arbitrary", "arbitrary"),
                                _nbytes((Q_BLOCK, tn), bf16) + 2 * _nbytes((DEC_LEN, tn), bf16)
                                + 2 * _nbytes((PAST_LEN, tn), f32)
                                + _nbytes((hb, N_BIAS_BLOCKS, GRID_W, 2 * GRID_W), f32)
                                + _nbytes((Q_BLOCK, tn), bf16),
                                10 * _nbytes((Q_BLOCK, BAND), f32)),
        name="latent_attention",
    )(qkv, qkv, qkv, cache_k, cache_v, bias_blocks)


HI_HALF_MASK = 0xFFFF0000


def _pack_bf16_halves(h):
    bits = lax.bitcast_convert_type(h.astype(f32), jnp.uint32)
    half = h.shape[1] // 2
    return (bits[:, :half] >> 16) | (bits[:, half:] & jnp.uint32(HI_HALF_MASK))


def _unpack_bf16_halves(words):
    lo = lax.bitcast_convert_type(words << 16, f32).astype(bf16)
    hi = lax.bitcast_convert_type(words & jnp.uint32(HI_HALF_MASK), f32).astype(bf16)
    return lo, hi


def _router_kernel(x_ref, gain_ref, sc_ref, sh_ref, wr_ref, hp_ref, idx_ref, wt_ref, rank_ref, cnt_ref, seen):
    tm = x_ref.shape[0]

    @pl.when(pl.program_id(0) == 0)
    def _():
        seen[...] = jnp.zeros(seen.shape, seen.dtype)

    h = _norm_mod_rows(x_ref[...], gain_ref[...], 1.0 + sc_ref[...], sh_ref[...]).astype(bf16)
    hp_ref[...] = _pack_bf16_halves(h)
    logits = _mm(h, wr_ref[...].astype(bf16))
    lane = lax.broadcasted_iota(jnp.int32, (tm, LANES), 1)
    l1 = jnp.where(lane < N_EXPERTS, logits, -jnp.inf)
    m1 = l1.max(axis=-1, keepdims=True)
    i1 = jnp.where(l1 == m1, lane, LANES).min(axis=-1, keepdims=True)
    l2 = jnp.where(lane == i1, -jnp.inf, l1)
    m2 = l2.max(axis=-1, keepdims=True)
    i2 = jnp.where(l2 == m2, lane, LANES).min(axis=-1, keepdims=True)
    e = jnp.exp(m2 - m1)
    w1 = 1.0 / (1.0 + e)
    w2 = e / (1.0 + e)
    idx_ref[...] = jnp.where(lane == 0, i1, jnp.where(lane == 1, i2, 0))
    wt_ref[...] = jnp.where(lane == 0, w1, jnp.where(lane == 1, w2, 0.0))

    first = lane == i1
    second = lane == i2
    chosen = (first | second).astype(bf16)
    earlier = (lax.broadcasted_iota(jnp.int32, (tm, tm), 1) < lax.broadcasted_iota(jnp.int32, (tm, tm), 0))
    before = _mm(earlier.astype(bf16), chosen) + seen[...]
    r1 = jnp.where(first, before, 0.0).sum(axis=-1, keepdims=True)
    r2 = jnp.where(second, before, 0.0).sum(axis=-1, keepdims=True)
    rank_ref[...] = jnp.where(lane == 0, r1, jnp.where(lane == 1, r2, 0.0)).astype(jnp.int32)
    seen[...] += chosen.astype(f32).sum(axis=0, keepdims=True)
    cnt_ref[...] = seen[...].astype(jnp.int32)


def _router(x, mod, norm, w_router_padded, layer):
    tm = 512
    jl = layer // 2
    lane_block = pl.BlockSpec((tm, LANES), lambda i: (i, 0))
    return pl.pallas_call(
        _router_kernel,
        grid=(T // tm,),
        in_specs=[
            pl.BlockSpec((tm, D), lambda i: (i, 0)),
            pl.BlockSpec((None, 1, D), lambda i: (layer, 0, 0)),
            _mod_spec(layer, 4, tm), _mod_spec(layer, 3, tm),
            pl.BlockSpec((None, D, LANES), lambda i: (jl, 0, 0)),
        ],
        out_specs=[pl.BlockSpec((tm, D // 2), lambda i: (i, 0)), lane_block, lane_block, lane_block,
                   pl.BlockSpec((1, LANES), lambda i: (0, 0))],
        out_shape=[jax.ShapeDtypeStruct((T, D // 2), jnp.uint32),
                   jax.ShapeDtypeStruct((T, LANES), jnp.int32),
                   jax.ShapeDtypeStruct((T, LANES), f32),
                   jax.ShapeDtypeStruct((T, LANES), jnp.int32),
                   jax.ShapeDtypeStruct((1, LANES), jnp.int32)],
        scratch_shapes=[pltpu.VMEM((1, LANES), f32)],
        compiler_params=_params(("arbitrary",),
                                _nbytes((tm, D), f32) + _nbytes((tm, D), bf16) + 3 * _nbytes((tm, LANES), f32),
                                _nbytes((D, LANES), f32) + 4 * _nbytes((tm, D), f32) + 2 * _nbytes((tm, tm), f32)),
        name="router",
    )(x, norm.reshape(DEPTH, 1, D), mod, mod, w_router_padded)


def _tile_table(tm, group_start, group_end, counts):
    experts = jnp.arange(N_EXPERTS, dtype=jnp.int32)
    tile_row0 = jnp.arange(MOE_ROWS // tm, dtype=jnp.int32) * tm
    tile_expert = jnp.sum((tile_row0[:, None] >= group_end[None, :]).astype(jnp.int32), axis=1)
    tile_expert = jnp.minimum(tile_expert, N_EXPERTS - 1)
    rows_end = jnp.sum(jnp.where(tile_expert[:, None] == experts[None, :], (group_start + counts)[None, :], 0), axis=1)
    rows_here = rows_end - tile_row0
    fill = jnp.where(rows_here <= 0, TILE_EMPTY, jnp.where(rows_here <= tm // 2, TILE_FIRST_HALF, TILE_FULL))
    tile_ids = jnp.arange(MOE_ROWS // tm, dtype=jnp.int32)
    input_block = lax.cummax(jnp.where(fill != TILE_EMPTY, tile_ids, 0), axis=0)
    return tile_expert.astype(jnp.int32), fill.astype(jnp.int32), input_block.astype(jnp.int32)


def _routing_plan(expert_ids, ranks, counts):
    e_flat = expert_ids.reshape(-1)
    span = ((counts + MOE_UP_TM - 1) // MOE_UP_TM) * MOE_UP_TM
    group_end = jnp.cumsum(span)
    group_start = group_end - span
    onehot = e_flat[:, None] == jnp.arange(N_EXPERTS, dtype=jnp.int32)[None, :]
    dest = ranks.reshape(-1) + jnp.sum(jnp.where(onehot, group_start[None, :], 0), axis=1)
    src_token = jnp.zeros((MOE_ROWS,), jnp.int32).at[dest].set(
        jnp.arange(N_ASSIGN, dtype=jnp.int32) // TOP_K, unique_indices=True)
    up_tiles = _tile_table(MOE_UP_TM, group_start, group_end, counts)
    down_tiles = _tile_table(MOE_DOWN_TM, group_start, group_end, counts)
    return src_token, dest.astype(jnp.int32), up_tiles, down_tiles


def _dispatch_kernel(src_ref, fill_ref, hp_hbm, o_ref, buf, sem):
    i = pl.program_id(0)
    n_steps = pl.num_programs(0)
    chunk = o_ref.shape[0]

    def tile_copies(tile, slot):
        return [pltpu.make_async_copy(hp_hbm.at[pl.ds(src_ref[tile * chunk + r], 1), :],
                                      buf.at[slot, pl.ds(r, 1), :], sem.at[slot]) for r in range(chunk)]

    def start_tile(tile, slot):
        for cp in tile_copies(tile, slot):
            cp.start()

    @pl.when((i == 0) & (fill_ref[0] != TILE_EMPTY))
    def _():
        start_tile(0, 0)

    @pl.when((i + 1 < n_steps) & (fill_ref[jnp.minimum(i + 1, n_steps - 1)] != TILE_EMPTY))
    def _():
        start_tile(i + 1, (i + 1) % 2)

    @pl.when(fill_ref[i] != TILE_EMPTY)
    def _():
        slot = i % 2
        for cp in tile_copies(i, slot):
            cp.wait()
        lo, hi = _unpack_bf16_halves(buf[slot])
        o_ref[:, :D // 2] = lo
        o_ref[:, D // 2:] = hi

    @pl.when(fill_ref[i] == TILE_EMPTY)
    def _():
        o_ref[...] = jnp.zeros(o_ref.shape, o_ref.dtype)


def _dispatch(h_packed, src_token, tile_fill):
    chunk = MOE_DOWN_TM
    return pl.pallas_call(
        _dispatch_kernel,
        grid_spec=pltpu.PrefetchScalarGridSpec(
            num_scalar_prefetch=2,
            grid=(MOE_ROWS // chunk,),
            in_specs=[pl.BlockSpec(memory_space=pl.ANY)],
            out_specs=pl.BlockSpec((chunk, D), lambda i, src, fill: (i, 0)),
            scratch_shapes=[pltpu.VMEM((2, chunk, D // 2), jnp.uint32), pltpu.SemaphoreType.DMA((2,))],
        ),
        out_shape=jax.ShapeDtypeStruct((MOE_ROWS, D), bf16),
        compiler_params=_params(("arbitrary",), _nbytes((chunk, D), bf16), 5 * _nbytes((chunk, D // 2), f32)),
        name="moe_dispatch",
    )(src_token, tile_fill, h_packed)


def _expert_changed(te_ref, i):
    return (i == 0) | (te_ref[i] != te_ref[jnp.maximum(i - 1, 0)])


def _for_tile_rows(i, fill_ref, o_ref, rows_fn):
    tm = o_ref.shape[0]
    fill = fill_ref[i]

    @pl.when(fill == TILE_FULL)
    def _():
        o_ref[...] = rows_fn(slice(None))

    @pl.when(fill == TILE_FIRST_HALF)
    def _():
        o_ref[:tm // 2] = rows_fn(slice(0, tm // 2))
        o_ref[tm // 2:] = jnp.zeros((tm // 2, o_ref.shape[1]), o_ref.dtype)

    @pl.when(fill == TILE_EMPTY)
    def _():
        o_ref[...] = jnp.zeros(o_ref.shape, o_ref.dtype)


def _moe_up_kernel(te_ref, fill_ref, src_ref, hs_ref, wg_ref, wu_ref, a_ref, wg_s, wu_s):
    i = pl.program_id(1)

    @pl.when(_expert_changed(te_ref, i))
    def _():
        wg_s[...] = wg_ref[...].astype(bf16)
        wu_s[...] = wu_ref[...].astype(bf16)

    def gated(rows):
        h = hs_ref[rows]
        g = _mm(h, wg_s[...])
        u = _mm(h, wu_s[...])
        return ((g * jax.nn.sigmoid(g)) * u).astype(bf16)

    _for_tile_rows(i, fill_ref, a_ref, gated)


def _moe_up(hs, w_gate, w_up, tile_expert, tile_fill, input_block, layer):
    tm, tn = MOE_UP_TM, 512
    jl = layer // 2
    return pl.pallas_call(
        _moe_up_kernel,
        grid_spec=pltpu.PrefetchScalarGridSpec(
            num_scalar_prefetch=3,
            grid=(D_FF // tn, MOE_ROWS // tm),
            in_specs=[
                pl.BlockSpec((tm, D), lambda j, i, te, fill, src: (src[i], 0)),
                pl.BlockSpec((None, None, D, tn), lambda j, i, te, fill, src: (jl, te[i], 0, j)),
                pl.BlockSpec((None, None, D, tn), lambda j, i, te, fill, src: (jl, te[i], 0, j)),
            ],
            out_specs=pl.BlockSpec((tm, tn), lambda j, i, te, fill, src: (i, j)),
            scratch_shapes=[pltpu.VMEM((D, tn), bf16), pltpu.VMEM((D, tn), bf16)],
        ),
        out_shape=jax.ShapeDtypeStruct((MOE_ROWS, D_FF), bf16),
        compiler_params=_params(("arbitrary", "arbitrary"),
                                _nbytes((tm, D), bf16) + 2 * _nbytes((D, tn), f32) + _nbytes((tm, tn), bf16),
                                2 * _nbytes((D, tn), bf16) + 3 * _nbytes((tm, tn), f32)),
        name="moe_up",
    )(tile_expert, tile_fill, input_block, hs, w_gate, w_up)


def _moe_down_kernel(te_ref, fill_ref, src_ref, a_ref, wd_ref, y_ref, wd_s):
    i = pl.program_id(1)

    @pl.when(_expert_changed(te_ref, i))
    def _():
        wd_s[...] = wd_ref[...].astype(bf16)

    _for_tile_rows(i, fill_ref, y_ref, lambda rows: _mm(a_ref[rows], wd_s[...]))


def _moe_down(a, w_down, tile_expert, tile_fill, input_block, layer):
    tm, tn = MOE_DOWN_TM, 512
    jl = layer // 2
    return pl.pallas_call(
        _moe_down_kernel,
        grid_spec=pltpu.PrefetchScalarGridSpec(
            num_scalar_prefetch=3,
            grid=(D // tn, MOE_ROWS // tm),
            in_specs=[
                pl.BlockSpec((tm, D_FF), lambda j, i, te, fill, src: (src[i], 0)),
                pl.BlockSpec((None, None, D_FF, tn), lambda j, i, te, fill, src: (jl, te[i], 0, j)),
            ],
            out_specs=pl.BlockSpec((tm, tn), lambda j, i, te, fill, src: (i, j)),
            scratch_shapes=[pltpu.VMEM((D_FF, tn), bf16)],
        ),
        out_shape=jax.ShapeDtypeStruct((MOE_ROWS, D), f32),
        compiler_params=_params(("arbitrary", "arbitrary"),
                                _nbytes((tm, D_FF), bf16) + _nbytes((D_FF, tn), f32) + _nbytes((tm, tn), f32),
                                _nbytes((D_FF, tn), bf16) + _nbytes((tm, tn), f32)),
        name="moe_down",
    )(tile_expert, tile_fill, input_block, a, w_down)


def _combine_kernel(pos_ref, x_ref, gate_ref, wt_ref, y_hbm, o_ref, ybuf, sem, *, tm):
    i = pl.program_id(0)
    n_steps = pl.num_programs(0)

    def tile_copies(tile, slot):
        return [pltpu.make_async_copy(y_hbm.at[pl.ds(pos_ref[(tile * tm + t) * TOP_K + c], 1), :],
                                      ybuf.at[slot, c, pl.ds(t, 1), :], sem.at[slot])
                for t in range(tm) for c in range(TOP_K)]

    def start_tile(tile, slot):
        for cp in tile_copies(tile, slot):
            cp.start()

    @pl.when(i == 0)
    def _():
        start_tile(0, 0)

    @pl.when(i + 1 < n_steps)
    def _():
        start_tile(i + 1, (i + 1) % 2)

    slot = i % 2
    for cp in tile_copies(i, slot):
        cp.wait()
    w1 = jnp.broadcast_to(wt_ref[:, 0:1], (tm, LANES))
    w2 = jnp.broadcast_to(wt_ref[:, 1:2], (tm, LANES))
    for s in range(ROW_SLABS):
        sl = slice(s * LANES, (s + 1) * LANES)
        mix = w1 * ybuf[slot, 0, :, sl] + w2 * ybuf[slot, 1, :, sl]
        o_ref[:, sl] = x_ref[:, sl] + gate_ref[:, sl] * mix


def _combine(x, y_slabs, pos, wts, mod, layer):
    tm = 256
    return pl.pallas_call(
        functools.partial(_combine_kernel, tm=tm),
        grid_spec=pltpu.PrefetchScalarGridSpec(
            num_scalar_prefetch=1,
            grid=(T // tm,),
            in_specs=[
                pl.BlockSpec((tm, D), lambda i, pos: (i, 0)),
                _mod_spec(layer, 5, tm),
                pl.BlockSpec((tm, LANES), lambda i, pos: (i, 0)),
                pl.BlockSpec(memory_space=pl.ANY),
            ],
            out_specs=pl.BlockSpec((tm, D), lambda i, pos: (i, 0)),
            scratch_shapes=[pltpu.VMEM((2, TOP_K, tm, D), f32),
                            pltpu.SemaphoreType.DMA((2,))],
        ),
        out_shape=jax.ShapeDtypeStruct((T, D), f32),
        compiler_params=_params(("arbitrary",),
                                2 * _nbytes((tm, D), f32) + _nbytes((tm, LANES), f32),
                                2 * _nbytes((TOP_K, tm, D), f32) + 4 * _nbytes((tm, LANES), f32)),
        name="moe_combine",
    )(pos, x, mod, wts, y_slabs)


def _moe_layer(x, mod, norm2, w_router, w_gate, w_up, w_down, layer):
    wr = jnp.pad(w_router, ((0, 0), (0, 0), (0, LANES - N_EXPERTS)))
    h_packed, idx, wts, ranks, counts = _router(x, mod, norm2, wr, layer)
    src_token, pos, up_tiles, down_tiles = _routing_plan(idx[:, :TOP_K], ranks[:, :TOP_K], counts[0, :N_EXPERTS])
    hs = _dispatch(h_packed, src_token, down_tiles[1])
    a = _moe_up(hs, w_gate, w_up, *up_tiles, layer)
    y = _moe_down(a, w_down, *down_tiles, layer)
    return _combine(x, y, pos, wts, mod, layer)


def kernel(x_prompt, x_sample, cache_k, cache_v, c, c_ctx, norm1, norm2, w_ada, b_ada, w_conv_in, conv_w,
           w_conv_out, w_qkv, q_norm, k_norm, rpb, w_attn_out, w_ffn_gate, w_ffn_up, w_ffn_down, w_router,
           w_moe_gate, w_moe_up, w_moe_down):
    x = jnp.concatenate([x_prompt.reshape(CTX_ROWS, D), x_sample.reshape(DEC_ROWS, D)], axis=0)
    cond = jnp.concatenate([c_ctx[None, :], c, jnp.zeros((N_GROUPS_PADDED - 1 - N_DEC_SEQ, D), f32)], axis=0)
    mod = _ada_params(cond, w_ada, b_ada)
    cache_k2 = cache_k.reshape(N_DEC_SEQ, -1, PAST_LEN, D)
    cache_v2 = cache_v.reshape(N_DEC_SEQ, -1, PAST_LEN, D)
    bias_blocks = _rel_bias_blocks(rpb)
    w_conv_in, w_conv_out, w_qkv, w_attn_out, w_ffn_gate, w_ffn_up, w_ffn_down = (
        w.astype(bf16) for w in (w_conv_in, w_conv_out, w_qkv, w_attn_out, w_ffn_gate, w_ffn_up, w_ffn_down))

    new_k, new_v = [], []
    for layer in range(DEPTH):
        jl = layer // 2
        if layer % 2 == 0:
            b, u = _conv_in(x, mod, norm1, w_conv_in, layer)
            gated = _conv_gate(b, u, conv_w, layer)
            x = _mm_residual([gated], w_conv_out, x, mod, layer, 2, 1024, 1024, "conv_out")
            a = _ffn_up(x, mod, norm2, w_ffn_gate, w_ffn_up, layer)
            x = _mm_residual([a], w_ffn_down, x, mod, layer, 5, 1024, 512, "ffn_down")
        else:
            qkv_ctx = _qkv(x, mod, norm1, w_qkv, q_norm, k_norm, layer, 0, CTX_ROWS, f32)
            qkv_dec = _qkv(x, mod, norm1, w_qkv, q_norm, k_norm, layer, CTX_ROWS, DEC_ROWS, bf16)
            o_ctx, k_ctx, v_ctx = _ctx_attention(qkv_ctx)
            new_k.append(k_ctx.reshape(N_CTX_SEQ, CTX_LEN, N_HEADS, HEAD_DIM))
            new_v.append(v_ctx.reshape(N_CTX_SEQ, CTX_LEN, N_HEADS, HEAD_DIM))
            o_dec = _latent_attention(qkv_dec, cache_k2, cache_v2, jl, bias_blocks)
            x = _mm_residual([o_ctx, o_dec], w_attn_out, x, mod, layer, 2, 1024, 1024, "attn_out")
            x = _moe_layer(x, mod, norm2, w_router, w_moe_gate, w_moe_up, w_moe_down, layer)

    y_prompt = x[:CTX_ROWS].reshape(N_CTX_SEQ, CTX_LEN, D)
    y_sample = x[CTX_ROWS:].reshape(N_DEC_SEQ, DEC_LEN, D)
    return y_prompt, y_sample, jnp.stack(new_k, axis=1), jnp.stack(new_v, axis=1)
```

```python
import functools

import numpy as np
import jax
import jax.numpy as jnp
from jax import lax
from jax.experimental import pallas as pl
from jax.experimental.pallas import tpu as pltpu

D = 2048
N_CTX_SEQ = 16
CTX_LEN = 256
N_DEC_SEQ = 4
DEC_LEN = 2048
CTX_ROWS = N_CTX_SEQ * CTX_LEN
DEC_ROWS = N_DEC_SEQ * DEC_LEN
T = CTX_ROWS + DEC_ROWS
DEPTH = 4
PAST_LEN = 512
N_HEADS = 16
HEAD_DIM = 128
GRID_W = 64
GRID_ROWS = DEC_LEN // GRID_W
WIN_ROWS = 8
WIN_COLS = 16
D_FF = 5632
N_EXPERTS = 8
TOP_K = 2
EPS = 1e-6
NEG_INF = -1e30
ATTN_SCALE = HEAD_DIM ** -0.5
LOG2_E = 1.4426950408889634
SCORE_SCALE_LOG2 = ATTN_SCALE * LOG2_E

N_GROUPS_PADDED = 8
LANES = 128
SUBLANES = 8
MXU_COLS = 256
VMEM_CAP_BYTES = 56 * 2 ** 20
VMEM_HEADROOM = 4 * 2 ** 20

Q_ROWS_PER_BLOCK = 4
Q_BLOCK = Q_ROWS_PER_BLOCK * GRID_W
N_Q_BLOCKS = GRID_ROWS // Q_ROWS_PER_BLOCK
BAND_ROWS = Q_ROWS_PER_BLOCK + WIN_ROWS
BAND = BAND_ROWS * GRID_W
MASKED_BLOCK = 2 * WIN_ROWS - 1
N_BIAS_BLOCKS = MASKED_BLOCK + 1

MOE_UP_TM = 1024
MOE_DOWN_TM = 512
N_ASSIGN = T * TOP_K
MOE_ROWS = N_ASSIGN + N_EXPERTS * MOE_UP_TM
TILE_EMPTY, TILE_FIRST_HALF, TILE_FULL = 0, 1, 2
ROW_SLABS = D // LANES

bf16 = jnp.bfloat16
f32 = jnp.float32


def _mm(a, b):
    return jnp.dot(a, b, preferred_element_type=f32)


def _mm_nt(a, b):
    return lax.dot_general(a, b, (((1,), (1,)), ((), ())), preferred_element_type=f32)


def _nbytes(shape, dtype):
    return int(np.prod(shape)) * jnp.dtype(dtype).itemsize


def _params(semantics, pipelined_bytes, resident_bytes):
    need = 2 * pipelined_bytes + resident_bytes + VMEM_HEADROOM
    return pltpu.CompilerParams(dimension_semantics=semantics,
                                vmem_limit_bytes=int(min(max(need, 16 * 2 ** 20), VMEM_CAP_BYTES)))


def _group_of_tile(i, tm):
    n_ctx = CTX_ROWS // tm
    per_seq = DEC_LEN // tm
    return jnp.where(i < n_ctx, 0, 1 + (i - n_ctx) // per_seq)


def _mod_spec(layer, which, tm, width=D, tile0=0):
    def index(*grid_and_prefetch):
        i = tile0 + grid_and_prefetch[0]
        j = grid_and_prefetch[1] if width != D else 0
        return ((layer * N_GROUPS_PADDED + _group_of_tile(i, tm)) * 6 + which, 0, j)
    return pl.BlockSpec((None, 1, width), index)


def _norm_mod_rows(x, gain, scale1p, shift):
    ms = jnp.mean(x * x, axis=-1, keepdims=True)
    y = x * lax.rsqrt(ms + EPS)
    return (y * gain) * scale1p + shift


def _norm_mod_to_scratch(x_ref, gain_ref, sc_ref, sh_ref, h_scr, rows=256):
    gain = gain_ref[...]
    scale1p = 1.0 + sc_ref[...]
    shift = sh_ref[...]

    def body(c, carry):
        r = pl.multiple_of(c * rows, rows)
        h = _norm_mod_rows(x_ref[pl.ds(r, rows), :], gain, scale1p, shift)
        h_scr[pl.ds(r, rows), :] = h.astype(bf16)
        return carry

    lax.fori_loop(0, x_ref.shape[0] // rows, body, 0)


def _ada_kernel(cond_ref, w_ref, b_ref, o_ref):
    c = cond_ref[...]
    s = (c * jax.nn.sigmoid(c)).astype(bf16)
    o_ref[...] = _mm(s, w_ref[...].astype(bf16)) + b_ref[...]


def _ada_params(cond, w_ada, b_ada):
    tn = 1024
    out = pl.pallas_call(
        _ada_kernel,
        grid=(DEPTH, 6 * D // tn),
        in_specs=[
            pl.BlockSpec((N_GROUPS_PADDED, D), lambda l, j: (0, 0)),
            pl.BlockSpec((None, D, tn), lambda l, j: (l, 0, j)),
            pl.BlockSpec((None, 1, tn), lambda l, j: (l, 0, j)),
        ],
        out_specs=pl.BlockSpec((None, N_GROUPS_PADDED, tn), lambda l, j: (l, 0, j)),
        out_shape=jax.ShapeDtypeStruct((DEPTH, N_GROUPS_PADDED, 6 * D), f32),
        compiler_params=_params(("arbitrary", "arbitrary"),
                                _nbytes((D, tn), f32) + _nbytes((8, tn), f32) * 2,
                                _nbytes((D, tn), bf16) + _nbytes((8, D), f32) * 2),
        name="ada_params",
    )(cond, w_ada, b_ada.reshape(DEPTH, 1, 6 * D))
    return out.reshape(DEPTH * N_GROUPS_PADDED * 6, 1, D)


def _conv_in_kernel(x_ref, gain_ref, sc_ref, sh_ref, wb_ref, wc_ref, wv_ref, b_ref, u_ref, h_scr):
    @pl.when(pl.program_id(1) == 0)
    def _():
        _norm_mod_to_scratch(x_ref, gain_ref, sc_ref, sh_ref, h_scr)

    h = h_scr[...]
    b_ref[...] = _mm(h, wb_ref[...])
    u_ref[...] = _mm(h, wc_ref[...]) * _mm(h, wv_ref[...])


def _conv_in(x, mod, norm, w_conv_in, layer, row0=0):
    tm, tn = 1024, 256
    jl = layer // 2
    nb = D // tn
    n_rows = x.shape[0]
    tile0 = row0 // tm
    return pl.pallas_call(
        _conv_in_kernel,
        grid=(n_rows // tm, nb),
        in_specs=[
            pl.BlockSpec((tm, D), lambda i, j: (i, 0)),
            pl.BlockSpec((None, 1, D), lambda i, j: (layer, 0, 0)),
            _mod_spec(layer, 1, tm, tile0=tile0), _mod_spec(layer, 0, tm, tile0=tile0),
            pl.BlockSpec((None, D, tn), lambda i, j: (jl, 0, j)),
            pl.BlockSpec((None, D, tn), lambda i, j: (jl, 0, nb + j)),
            pl.BlockSpec((None, D, tn), lambda i, j: (jl, 0, 2 * nb + j)),
        ],
        out_specs=[pl.BlockSpec((tm, tn), lambda i, j: (i, j))] * 2,
        out_shape=[jax.ShapeDtypeStruct((n_rows, D), f32)] * 2,
        scratch_shapes=[pltpu.VMEM((tm, D), bf16)],
        compiler_params=_params(("arbitrary", "arbitrary"),
                                _nbytes((tm, D), f32) + 3 * _nbytes((D, tn), bf16) + 2 * _nbytes((tm, tn), f32),
                                _nbytes((tm, D), bf16) + 3 * _nbytes((tm, tn), f32)),
        name="conv_in",
    )(x, norm.reshape(DEPTH, 1, D), mod, mod, w_conv_in, w_conv_in, w_conv_in)


def _ffn_up_kernel(x_ref, gain_ref, sc_ref, sh_ref, wg_ref, wu_ref, a_ref, h_scr):
    @pl.when(pl.program_id(1) == 0)
    def _():
        _norm_mod_to_scratch(x_ref, gain_ref, sc_ref, sh_ref, h_scr)

    h = h_scr[...]
    g = _mm(h, wg_ref[...])
    u = _mm(h, wu_ref[...])
    a_ref[...] = ((g * jax.nn.sigmoid(g)) * u).astype(bf16)


def _ffn_up(x, mod, norm, w_gate, w_up, layer):
    tm, tn = 1024, 512
    jl = layer // 2
    return pl.pallas_call(
        _ffn_up_kernel,
        grid=(T // tm, D_FF // tn),
        in_specs=[
            pl.BlockSpec((tm, D), lambda i, j: (i, 0)),
            pl.BlockSpec((None, 1, D), lambda i, j: (layer, 0, 0)),
            _mod_spec(layer, 4, tm), _mod_spec(layer, 3, tm),
            pl.BlockSpec((None, D, tn), lambda i, j: (jl, 0, j)),
            pl.BlockSpec((None, D, tn), lambda i, j: (jl, 0, j)),
        ],
        out_specs=pl.BlockSpec((tm, tn), lambda i, j: (i, j)),
        out_shape=jax.ShapeDtypeStruct((T, D_FF), bf16),
        scratch_shapes=[pltpu.VMEM((tm, D), bf16)],
        compiler_params=_params(("arbitrary", "arbitrary"),
                                _nbytes((tm, D), f32) + 2 * _nbytes((D, tn), bf16) + _nbytes((tm, tn), bf16),
                                _nbytes((tm, D), bf16) + 3 * _nbytes((tm, tn), f32)),
        name="ffn_up",
    )(x, norm.reshape(DEPTH, 1, D), mod, mod, w_gate, w_up)


def _qkv_kernel(x_ref, gain_ref, sc_ref, sh_ref, w_ref, qg_ref, kg_ref, o_ref, h_scr, *, n_q_blocks):
    j = pl.program_id(1)

    @pl.when(j == 0)
    def _():
        _norm_mod_to_scratch(x_ref, gain_ref, sc_ref, sh_ref, h_scr)

    is_qk = j < 2 * n_q_blocks
    head_gain = jnp.where(j < n_q_blocks, qg_ref[...], kg_ref[...])
    h = h_scr[...]
    for c in range(0, o_ref.shape[1], MXU_COLS):
        acc = _mm(h, w_ref[:, c:c + MXU_COLS])
        for hh in range(MXU_COLS // HEAD_DIM):
            a = acc[:, hh * HEAD_DIM:(hh + 1) * HEAD_DIM]
            ms = jnp.mean(a * a, axis=-1, keepdims=True)
            normed = (a * lax.rsqrt(ms + EPS)) * head_gain
            cols = slice(c + hh * HEAD_DIM, c + (hh + 1) * HEAD_DIM)
            o_ref[:, cols] = jnp.where(is_qk, normed, a).astype(o_ref.dtype)


def _qkv(x, mod, norm, w_qkv, q_norm, k_norm, layer, row0, n_rows, out_dtype):
    tm, tn = 1024, 1024
    jl = layer // 2
    tile0 = row0 // tm
    return pl.pallas_call(
        functools.partial(_qkv_kernel, n_q_blocks=D // tn),
        grid=(n_rows // tm, 3 * D // tn),
        in_specs=[
            pl.BlockSpec((tm, D), lambda i, j: (tile0 + i, 0)),
            pl.BlockSpec((None, 1, D), lambda i, j: (layer, 0, 0)),
            _mod_spec(layer, 1, tm, tile0=tile0), _mod_spec(layer, 0, tm, tile0=tile0),
            pl.BlockSpec((None, D, tn), lambda i, j: (jl, 0, j)),
            pl.BlockSpec((None, 1, HEAD_DIM), lambda i, j: (jl, 0, 0)),
            pl.BlockSpec((None, 1, HEAD_DIM), lambda i, j: (jl, 0, 0)),
        ],
        out_specs=pl.BlockSpec((tm, tn), lambda i, j: (i, j)),
        out_shape=jax.ShapeDtypeStruct((n_rows, 3 * D), out_dtype),
        scratch_shapes=[pltpu.VMEM((tm, D), bf16)],
        compiler_params=_params(("arbitrary", "arbitrary"),
                                _nbytes((tm, D), f32) + _nbytes((D, tn), bf16) + _nbytes((tm, tn), out_dtype),
                                _nbytes((tm, D), bf16) + 2 * _nbytes((tm, tn), f32)),
        name="qkv",
    )(x, norm.reshape(DEPTH, 1, D), mod, mod, w_qkv,
      q_norm.reshape(-1, 1, HEAD_DIM), k_norm.reshape(-1, 1, HEAD_DIM))


def _mm_res_kernel(*refs, n_a, n_ctx_tiles):
    a_refs, (w_ref, *x_refs, gate_ref, o_ref) = refs[:n_a], refs[n_a:]

    def update(a_ref, x_ref):
        o_ref[...] = x_ref[...] + gate_ref[...] * _mm(a_ref[...], w_ref[...])

    if len(a_refs) == 1 and len(x_refs) == 1:
        update(a_refs[0], x_refs[0])
    else:
        i = pl.program_id(0)

        @pl.when(i < n_ctx_tiles)
        def _():
            update(a_refs[0], x_refs[0])

        @pl.when(i >= n_ctx_tiles)
        def _():
            update(a_refs[-1], x_refs[-1])


def _row_part_specs(parts, tm, width, n_ctx_tiles):
    col = (lambda j: 0) if width == parts[0].shape[1] else (lambda j: j)
    if len(parts) == 1:
        return [pl.BlockSpec((tm, width), lambda i, j: (i, col(j)))]
    return [pl.BlockSpec((tm, width), lambda i, j: (jnp.minimum(i, n_ctx_tiles - 1), col(j))),
            pl.BlockSpec((tm, width), lambda i, j: (jnp.maximum(i - n_ctx_tiles, 0), col(j)))]


def _mm_residual(a_parts, w, x_parts, mod, layer, which, tm, tn, name):
    k = a_parts[0].shape[1]
    jl = layer // 2
    n_ctx_tiles = CTX_ROWS // tm
    return pl.pallas_call(
        functools.partial(_mm_res_kernel, n_a=len(a_parts), n_ctx_tiles=n_ctx_tiles),
        grid=(T // tm, D // tn),
        in_specs=_row_part_specs(a_parts, tm, k, n_ctx_tiles) + [
            pl.BlockSpec((None, k, tn), lambda i, j: (jl, 0, j)),
        ] + _row_part_specs(x_parts, tm, tn, n_ctx_tiles) + [
            _mod_spec(layer, which, tm, tn),
        ],
        out_specs=pl.BlockSpec((tm, tn), lambda i, j: (i, j)),
        out_shape=jax.ShapeDtypeStruct((T, D), f32),
        compiler_params=_params(("arbitrary", "arbitrary"),
                                len(a_parts) * _nbytes((tm, k), bf16) + _nbytes((k, tn), bf16)
                                + (1 + len(x_parts)) * _nbytes((tm, tn), f32),
                                2 * _nbytes((tm, tn), f32)),
        name=name,
    )(*a_parts, w, *x_parts, mod)


def _conv_gate_kernel(b_ref, u_ref, up_ref, un_ref, cw_ref, o_ref, *, tm, cols, ctx_rows):
    row0 = pl.program_id(0) * tm
    seq_len = jnp.where(row0 < ctx_rows, CTX_LEN, DEC_LEN)
    local = lax.broadcasted_iota(jnp.int32, (tm, cols), 0)
    pos = (row0 + local) & (seq_len - 1)
    is_first = pos == 0
    is_last = pos == seq_len - 1
    for c in range(D // cols):
        sl = slice(c * cols, (c + 1) * cols)
        u = u_ref[:, sl]
        prev_row = up_ref[SUBLANES - 1:SUBLANES, sl]
        next_row = un_ref[0:1, sl]
        u_prev = jnp.where(local == 0, prev_row, pltpu.roll(u, 1, axis=0))
        u_next = jnp.where(local == tm - 1, next_row, pltpu.roll(u, tm - 1, axis=0))
        u_prev = jnp.where(is_first, 0.0, u_prev)
        u_next = jnp.where(is_last, 0.0, u_next)
        conv = u_prev * cw_ref[0:1, sl] + u * cw_ref[1:2, sl] + u_next * cw_ref[2:3, sl]
        o_ref[:, sl] = (b_ref[:, sl] * conv).astype(bf16)


def _conv_gate(b, u, conv_w, layer, ctx_rows):
    tm, cols = 512, 512
    jl = layer // 2
    blocks_per_tile = tm // SUBLANES
    n_rows = b.shape[0]
    return pl.pallas_call(
        functools.partial(_conv_gate_kernel, tm=tm, cols=cols, ctx_rows=ctx_rows),
        grid=(n_rows // tm,),
        in_specs=[
            pl.BlockSpec((tm, D), lambda i: (i, 0)),
            pl.BlockSpec((tm, D), lambda i: (i, 0)),
            pl.BlockSpec((SUBLANES, D), lambda i: (jnp.maximum(i * blocks_per_tile - 1, 0), 0)),
            pl.BlockSpec((SUBLANES, D),
                         lambda i: (jnp.minimum((i + 1) * blocks_per_tile, n_rows // SUBLANES - 1), 0)),
            pl.BlockSpec((None, 3, D), lambda i: (jl, 0, 0)),
        ],
        out_specs=pl.BlockSpec((tm, D), lambda i: (i, 0)),
        out_shape=jax.ShapeDtypeStruct((n_rows, D), bf16),
        compiler_params=_params(("arbitrary",),
                                2 * _nbytes((tm, D), f32) + _nbytes((tm, D), bf16) + 2 * _nbytes((SUBLANES, D), f32),
                                8 * _nbytes((tm, cols), f32)),
        name="conv_gate",
    )(b, u, u, u, conv_w)


def _softmax_pv(parts):
    m = parts[0][0].max(axis=-1, keepdims=True)
    for s, _ in parts[1:]:
        m = jnp.maximum(m, s.max(axis=-1, keepdims=True))
    acc = None
    for s, v in parts:
        p = jnp.exp2(s - m).astype(bf16)
        v_and_ones = jnp.concatenate([v, jnp.ones_like(v)], axis=1)
        o = _mm(p, v_and_ones)
        acc = o if acc is None else acc + o
    head_dim = acc.shape[1] // 2
    return acc[:, :head_dim] * (1.0 / acc[:, head_dim:head_dim + 1])


def _ctx_attn_kernel(q_ref, k_ref, v_ref, o_ref, k_out_ref, v_out_ref):
    k_out_ref[...] = k_ref[...]
    v_out_ref[...] = v_ref[...]
    for hh in range(q_ref.shape[1] // HEAD_DIM):
        sl = slice(hh * HEAD_DIM, (hh + 1) * HEAD_DIM)
        q = q_ref[:, sl].astype(bf16)
        k = k_ref[:, sl].astype(bf16)
        v = v_ref[:, sl].astype(bf16)
        s = _mm_nt(q, k) * SCORE_SCALE_LOG2
        o_ref[:, sl] = _softmax_pv([(s, v)]).astype(bf16)


def _ctx_attention(qkv):
    tn = 512
    nb = D // tn
    return pl.pallas_call(
        _ctx_attn_kernel,
        grid=(N_CTX_SEQ, nb),
        in_specs=[
            pl.BlockSpec((CTX_LEN, tn), lambda b, g: (b, g)),
            pl.BlockSpec((CTX_LEN, tn), lambda b, g: (b, nb + g)),
            pl.BlockSpec((CTX_LEN, tn), lambda b, g: (b, 2 * nb + g)),
        ],
        out_specs=[pl.BlockSpec((CTX_LEN, tn), lambda b, g: (b, g))] * 3,
        out_shape=[jax.ShapeDtypeStruct((CTX_ROWS, D), bf16),
                   jax.ShapeDtypeStruct((CTX_ROWS, D), f32),
                   jax.ShapeDtypeStruct((CTX_ROWS, D), f32)],
        compiler_params=_params(("arbitrary", "arbitrary"),
                                5 * _nbytes((CTX_LEN, tn), f32) + _nbytes((CTX_LEN, tn), bf16),
                                8 * _nbytes((CTX_LEN, CTX_LEN), f32)),
        name="ctx_attention",
    )(qkv, qkv, qkv)


def _band_start_row(m):
    return jnp.clip(Q_ROWS_PER_BLOCK * m - WIN_ROWS // 2, 0, GRID_ROWS - BAND_ROWS)


def _band_block_ids(m):
    band0 = _band_start_row(m)
    ids = []
    for rho in range(Q_ROWS_PER_BLOCK):
        r = Q_ROWS_PER_BLOCK * m + rho
        win0 = jnp.clip(r - WIN_ROWS // 2, 0, GRID_ROWS - WIN_ROWS)
        row_ids = []
        for kap in range(BAND_ROWS):
            kr = band0 + kap
            inside = (kr >= win0) & (kr < win0 + WIN_ROWS)
            row_ids.append(jnp.where(inside, kr - r + WIN_ROWS - 1, MASKED_BLOCK))
        ids.append(row_ids)
    return ids


def _band_bias(bias_ref, hh, ids):
    left_half = lax.broadcasted_iota(jnp.int32, (GRID_W, 2 * GRID_W), 1) < GRID_W
    rows = []
    for rho in range(Q_ROWS_PER_BLOCK):
        pairs = [jnp.where(left_half, bias_ref[hh, ids[rho][kap]], bias_ref[hh, ids[rho][kap + 1]])
                 for kap in range(0, BAND_ROWS, 2)]
        rows.append(jnp.concatenate(pairs, axis=1))
    return jnp.concatenate(rows, axis=0)


def _loc_attn_kernel(q_ref, k_ref, v_ref, ck_ref, cv_ref, bias_ref, o_ref):
    start = pl.multiple_of(_band_start_row(pl.program_id(2)) * GRID_W, GRID_W)
    ids = _band_block_ids(pl.program_id(2))
    for hh in range(q_ref.shape[1] // HEAD_DIM):
        sl = slice(hh * HEAD_DIM, (hh + 1) * HEAD_DIM)
        q = q_ref[:, sl]
        k_band = k_ref[pl.ds(start, BAND), sl]
        v_band = v_ref[pl.ds(start, BAND), sl]
        k_ctx = ck_ref[:, sl].astype(bf16)
        v_ctx = cv_ref[:, sl].astype(bf16)
        s_loc = _mm_nt(q, k_band) * SCORE_SCALE_LOG2 + _band_bias(bias_ref, hh, ids)
        s_ctx = _mm_nt(q, k_ctx) * SCORE_SCALE_LOG2
        o_ref[:, sl] = _softmax_pv([(s_loc, v_band), (s_ctx, v_ctx)]).astype(bf16)


def _rel_bias_blocks(rpb_all):
    cols = np.arange(GRID_W)
    col_start = np.clip(cols - WIN_COLS // 2, 0, GRID_W - WIN_COLS)
    col_ok = (cols[None, :] >= col_start[:, None]) & (cols[None, :] < col_start[:, None] + WIN_COLS)
    pad = GRID_W - WIN_COLS
    n = rpb_all.shape[0] * rpb_all.shape[1]
    padded = jnp.pad(rpb_all.astype(f32).reshape(n, MASKED_BLOCK, -1), ((0, 0), (0, 0), (pad, pad)))
    skewed = jnp.tile(padded, (1, 1, GRID_W + 1))[:, :, :GRID_W * 2 * GRID_W]
    skewed = skewed.reshape(n, MASKED_BLOCK, GRID_W, 2 * GRID_W)[..., :GRID_W]
    toeplitz = jnp.flip(skewed, axis=2)
    toeplitz = jnp.where(col_ok[None, None], toeplitz * LOG2_E, NEG_INF)
    masked = jnp.full((n, 1, GRID_W, GRID_W), NEG_INF, f32)
    blocks = jnp.concatenate([toeplitz, masked], axis=1)
    return jnp.concatenate([blocks, blocks], axis=3)


def _latent_attention(qkv, cache_k, cache_v, layer_slot, bias_blocks):
    tn = 512
    nb = D // tn
    hb = tn // HEAD_DIM
    return pl.pallas_call(
        _loc_attn_kernel,
        grid=(N_DEC_SEQ, nb, N_Q_BLOCKS),
        in_specs=[
            pl.BlockSpec((Q_BLOCK, tn), lambda b, g, m: (b * N_Q_BLOCKS + m, g)),
            pl.BlockSpec((DEC_LEN, tn), lambda b, g, m: (b, nb + g)),
            pl.BlockSpec((DEC_LEN, tn), lambda b, g, m: (b, 2 * nb + g)),
            pl.BlockSpec((None, None, PAST_LEN, tn), lambda b, g, m: (b, layer_slot, 0, g)),
            pl.BlockSpec((None, None, PAST_LEN, tn), lambda b, g, m: (b, layer_slot, 0, g)),
            pl.BlockSpec((hb, N_BIAS_BLOCKS, GRID_W, 2 * GRID_W), lambda b, g, m: (layer_slot * nb + g, 0, 0, 0)),
        ],
        out_specs=pl.BlockSpec((Q_BLOCK, tn), lambda b, g, m: (b * N_Q_BLOCKS + m, g)),
        out_shape=jax.ShapeDtypeStruct((DEC_ROWS, D), bf16),
        compiler_params=_params(("arbitrary", "arbitrary", "arbitrary"),
                                _nbytes((Q_BLOCK, tn), bf16) + 2 * _nbytes((DEC_LEN, tn), bf16)
                                + 2 * _nbytes((PAST_LEN, tn), f32)
                                + _nbytes((hb, N_BIAS_BLOCKS, GRID_W, 2 * GRID_W), f32)
                                + _nbytes((Q_BLOCK, tn), bf16),
                                10 * _nbytes((Q_BLOCK, BAND), f32)),
        name="latent_attention",
    )(qkv, qkv, qkv, cache_k, cache_v, bias_blocks)


def _router_kernel(x_ref, gain_ref, sc_ref, sh_ref, wr_ref, h_ref, idx_ref, wt_ref, rank_ref, cnt_ref, seen):
    tm = x_ref.shape[0]

    @pl.when(pl.program_id(0) == 0)
    def _():
        seen[...] = jnp.zeros(seen.shape, seen.dtype)

    h = _norm_mod_rows(x_ref[...], gain_ref[...], 1.0 + sc_ref[...], sh_ref[...])
    h_ref[...] = h
    w = wr_ref[...]
    h_hi, w_hi = h.astype(bf16), w.astype(bf16)
    h_lo = (h - h_hi.astype(f32)).astype(bf16)
    w_lo = (w - w_hi.astype(f32)).astype(bf16)
    logits = _mm(h_hi, w_hi) + (_mm(h_lo, w_hi) + _mm(h_hi, w_lo))
    lane = lax.broadcasted_iota(jnp.int32, (tm, LANES), 1)
    l1 = jnp.where(lane < N_EXPERTS, logits, -jnp.inf)
    m1 = l1.max(axis=-1, keepdims=True)
    i1 = jnp.where(l1 == m1, lane, LANES).min(axis=-1, keepdims=True)
    l2 = jnp.where(lane == i1, -jnp.inf, l1)
    m2 = l2.max(axis=-1, keepdims=True)
    i2 = jnp.where(l2 == m2, lane, LANES).min(axis=-1, keepdims=True)
    e = jnp.exp(m2 - m1)
    w1 = 1.0 / (1.0 + e)
    w2 = e / (1.0 + e)
    idx_ref[...] = jnp.where(lane == 0, i1, jnp.where(lane == 1, i2, 0))
    wt_ref[...] = jnp.where(lane == 0, w1, jnp.where(lane == 1, w2, 0.0))

    first = lane == i1
    second = lane == i2
    chosen = (first | second).astype(bf16)
    earlier = (lax.broadcasted_iota(jnp.int32, (tm, tm), 1) < lax.broadcasted_iota(jnp.int32, (tm, tm), 0))
    before = _mm(earlier.astype(bf16), chosen) + seen[...]
    r1 = jnp.where(first, before, 0.0).sum(axis=-1, keepdims=True)
    r2 = jnp.where(second, before, 0.0).sum(axis=-1, keepdims=True)
    rank_ref[...] = jnp.where(lane == 0, r1, jnp.where(lane == 1, r2, 0.0)).astype(jnp.int32)
    seen[...] += chosen.astype(f32).sum(axis=0, keepdims=True)
    cnt_ref[...] = seen[...].astype(jnp.int32)


def _router(x, mod, norm, w_router_padded, layer):
    tm = 512
    jl = layer // 2
    lane_block = pl.BlockSpec((tm, LANES), lambda i: (i, 0))
    return pl.pallas_call(
        _router_kernel,
        grid=(T // tm,),
        in_specs=[
            pl.BlockSpec((tm, D), lambda i: (i, 0)),
            pl.BlockSpec((None, 1, D), lambda i: (layer, 0, 0)),
            _mod_spec(layer, 4, tm), _mod_spec(layer, 3, tm),
            pl.BlockSpec((None, D, LANES), lambda i: (jl, 0, 0)),
        ],
        out_specs=[pl.BlockSpec((tm, D), lambda i: (i, 0)), lane_block, lane_block, lane_block,
                   pl.BlockSpec((1, LANES), lambda i: (0, 0))],
        out_shape=[jax.ShapeDtypeStruct((T, D), f32),
                   jax.ShapeDtypeStruct((T, LANES), jnp.int32),
                   jax.ShapeDtypeStruct((T, LANES), f32),
                   jax.ShapeDtypeStruct((T, LANES), jnp.int32),
                   jax.ShapeDtypeStruct((1, LANES), jnp.int32)],
        scratch_shapes=[pltpu.VMEM((1, LANES), f32)],
        compiler_params=_params(("arbitrary",),
                                2 * _nbytes((tm, D), f32) + 3 * _nbytes((tm, LANES), f32),
                                _nbytes((D, LANES), f32) + 4 * _nbytes((tm, D), f32) + 2 * _nbytes((tm, tm), f32)),
        name="router",
    )(x, norm.reshape(DEPTH, 1, D), mod, mod, w_router_padded)


def _tile_table(tm, group_start, group_end, counts):
    experts = jnp.arange(N_EXPERTS, dtype=jnp.int32)
    tile_row0 = jnp.arange(MOE_ROWS // tm, dtype=jnp.int32) * tm
    tile_expert = jnp.sum((tile_row0[:, None] >= group_end[None, :]).astype(jnp.int32), axis=1)
    tile_expert = jnp.minimum(tile_expert, N_EXPERTS - 1)
    rows_end = jnp.sum(jnp.where(tile_expert[:, None] == experts[None, :], (group_start + counts)[None, :], 0), axis=1)
    rows_here = rows_end - tile_row0
    fill = jnp.where(rows_here <= 0, TILE_EMPTY, jnp.where(rows_here <= tm // 2, TILE_FIRST_HALF, TILE_FULL))
    tile_ids = jnp.arange(MOE_ROWS // tm, dtype=jnp.int32)
    input_block = lax.cummax(jnp.where(fill != TILE_EMPTY, tile_ids, 0), axis=0)
    return tile_expert.astype(jnp.int32), fill.astype(jnp.int32), input_block.astype(jnp.int32)


def _routing_plan(expert_ids, ranks, counts):
    e_flat = expert_ids.reshape(-1)
    span = ((counts + MOE_UP_TM - 1) // MOE_UP_TM) * MOE_UP_TM
    group_end = jnp.cumsum(span)
    group_start = group_end - span
    onehot = e_flat[:, None] == jnp.arange(N_EXPERTS, dtype=jnp.int32)[None, :]
    dest = ranks.reshape(-1) + jnp.sum(jnp.where(onehot, group_start[None, :], 0), axis=1)
    src_token = jnp.zeros((MOE_ROWS,), jnp.int32).at[dest].set(
        jnp.arange(N_ASSIGN, dtype=jnp.int32) // TOP_K, unique_indices=True)
    up_tiles = _tile_table(MOE_UP_TM, group_start, group_end, counts)
    down_tiles = _tile_table(MOE_DOWN_TM, group_start, group_end, counts)
    return src_token, dest.astype(jnp.int32), up_tiles, down_tiles


def _dispatch_kernel(src_ref, fill_ref, h_hbm, o_ref, buf, sem):
    i = pl.program_id(0)
    n_steps = pl.num_programs(0)
    chunk = o_ref.shape[0]

    def tile_copies(tile, slot):
        return [pltpu.make_async_copy(h_hbm.at[pl.ds(src_ref[tile * chunk + r], 1), :],
                                      buf.at[slot, pl.ds(r, 1), :], sem.at[slot]) for r in range(chunk)]

    def start_tile(tile, slot):
        for cp in tile_copies(tile, slot):
            cp.start()

    @pl.when((i == 0) & (fill_ref[0] != TILE_EMPTY))
    def _():
        start_tile(0, 0)

    @pl.when((i + 1 < n_steps) & (fill_ref[jnp.minimum(i + 1, n_steps - 1)] != TILE_EMPTY))
    def _():
        start_tile(i + 1, (i + 1) % 2)

    @pl.when(fill_ref[i] != TILE_EMPTY)
    def _():
        slot = i % 2
        for cp in tile_copies(i, slot):
            cp.wait()
        o_ref[...] = buf[slot].astype(bf16)

    @pl.when(fill_ref[i] == TILE_EMPTY)
    def _():
        o_ref[...] = jnp.zeros(o_ref.shape, o_ref.dtype)


def _dispatch(h, src_token, tile_fill):
    chunk = MOE_DOWN_TM
    return pl.pallas_call(
        _dispatch_kernel,
        grid_spec=pltpu.PrefetchScalarGridSpec(
            num_scalar_prefetch=2,
            grid=(MOE_ROWS // chunk,),
            in_specs=[pl.BlockSpec(memory_space=pl.ANY)],
            out_specs=pl.BlockSpec((chunk, D), lambda i, src, fill: (i, 0)),
            scratch_shapes=[pltpu.VMEM((2, chunk, D), f32), pltpu.SemaphoreType.DMA((2,))],
        ),
        out_shape=jax.ShapeDtypeStruct((MOE_ROWS, D), bf16),
        compiler_params=_params(("arbitrary",), _nbytes((chunk, D), bf16), 3 * _nbytes((chunk, D), f32)),
        name="moe_dispatch",
    )(src_token, tile_fill, h)


def _expert_changed(te_ref, i):
    return (i == 0) | (te_ref[i] != te_ref[jnp.maximum(i - 1, 0)])


def _for_tile_rows(i, fill_ref, o_ref, rows_fn):
    tm = o_ref.shape[0]
    fill = fill_ref[i]

    @pl.when(fill == TILE_FULL)
    def _():
        o_ref[...] = rows_fn(slice(None))

    @pl.when(fill == TILE_FIRST_HALF)
    def _():
        o_ref[:tm // 2] = rows_fn(slice(0, tm // 2))
        o_ref[tm // 2:] = jnp.zeros((tm // 2, o_ref.shape[1]), o_ref.dtype)

    @pl.when(fill == TILE_EMPTY)
    def _():
        o_ref[...] = jnp.zeros(o_ref.shape, o_ref.dtype)


def _moe_up_kernel(te_ref, fill_ref, src_ref, hs_ref, wg_ref, wu_ref, a_ref, wg_s, wu_s):
    i = pl.program_id(1)

    @pl.when(_expert_changed(te_ref, i))
    def _():
        wg_s[...] = wg_ref[...].astype(bf16)
        wu_s[...] = wu_ref[...].astype(bf16)

    def gated(rows):
        h = hs_ref[rows]
        g = _mm(h, wg_s[...])
        u = _mm(h, wu_s[...])
        return ((g * jax.nn.sigmoid(g)) * u).astype(bf16)

    _for_tile_rows(i, fill_ref, a_ref, gated)


def _moe_up(hs, w_gate, w_up, tile_expert, tile_fill, input_block, layer):
    tm, tn = MOE_UP_TM, 512
    jl = layer // 2
    return pl.pallas_call(
        _moe_up_kernel,
        grid_spec=pltpu.PrefetchScalarGridSpec(
            num_scalar_prefetch=3,
            grid=(D_FF // tn, MOE_ROWS // tm),
            in_specs=[
                pl.BlockSpec((tm, D), lambda j, i, te, fill, src: (src[i], 0)),
                pl.BlockSpec((None, None, D, tn), lambda j, i, te, fill, src: (jl, te[i], 0, j)),
                pl.BlockSpec((None, None, D, tn), lambda j, i, te, fill, src: (jl, te[i], 0, j)),
            ],
            out_specs=pl.BlockSpec((tm, tn), lambda j, i, te, fill, src: (i, j)),
            scratch_shapes=[pltpu.VMEM((D, tn), bf16), pltpu.VMEM((D, tn), bf16)],
        ),
        out_shape=jax.ShapeDtypeStruct((MOE_ROWS, D_FF), bf16),
        compiler_params=_params(("arbitrary", "arbitrary"),
                                _nbytes((tm, D), bf16) + 2 * _nbytes((D, tn), f32) + _nbytes((tm, tn), bf16),
                                2 * _nbytes((D, tn), bf16) + 3 * _nbytes((tm, tn), f32)),
        name="moe_up",
    )(tile_expert, tile_fill, input_block, hs, w_gate, w_up)


def _moe_down_kernel(te_ref, fill_ref, src_ref, a_ref, wd_ref, y_ref, wd_s):
    i = pl.program_id(1)

    @pl.when(_expert_changed(te_ref, i))
    def _():
        wd_s[...] = wd_ref[...].astype(bf16)

    _for_tile_rows(i, fill_ref, y_ref, lambda rows: _mm(a_ref[rows], wd_s[...]))


def _moe_down(a, w_down, tile_expert, tile_fill, input_block, layer):
    tm, tn = MOE_DOWN_TM, 512
    jl = layer // 2
    return pl.pallas_call(
        _moe_down_kernel,
        grid_spec=pltpu.PrefetchScalarGridSpec(
            num_scalar_prefetch=3,
            grid=(D // tn, MOE_ROWS // tm),
            in_specs=[
                pl.BlockSpec((tm, D_FF), lambda j, i, te, fill, src: (src[i], 0)),
                pl.BlockSpec((None, None, D_FF, tn), lambda j, i, te, fill, src: (jl, te[i], 0, j)),
            ],
            out_specs=pl.BlockSpec((tm, tn), lambda j, i, te, fill, src: (i, j)),
            scratch_shapes=[pltpu.VMEM((D_FF, tn), bf16)],
        ),
        out_shape=jax.ShapeDtypeStruct((MOE_ROWS, D), f32),
        compiler_params=_params(("arbitrary", "arbitrary"),
                                _nbytes((tm, D_FF), bf16) + _nbytes((D_FF, tn), f32) + _nbytes((tm, tn), f32),
                                _nbytes((D_FF, tn), bf16) + _nbytes((tm, tn), f32)),
        name="moe_down",
    )(tile_expert, tile_fill, input_block, a, w_down)


def _combine_kernel(pos_ref, x_ref, gate_ref, wt_ref, y_hbm, *rest, tm, n_out):
    o_refs, (ybuf, sem) = rest[:n_out], rest[n_out:]
    i = pl.program_id(0)
    n_steps = pl.num_programs(0)

    def tile_copies(tile, slot):
        return [pltpu.make_async_copy(y_hbm.at[pl.ds(pos_ref[(tile * tm + t) * TOP_K + c], 1), :],
                                      ybuf.at[slot, c, pl.ds(t, 1), :], sem.at[slot])
                for t in range(tm) for c in range(TOP_K)]

    def start_tile(tile, slot):
        for cp in tile_copies(tile, slot):
            cp.start()

    @pl.when(i == 0)
    def _():
        start_tile(0, 0)

    @pl.when(i + 1 < n_steps)
    def _():
        start_tile(i + 1, (i + 1) % 2)

    slot = i % 2
    for cp in tile_copies(i, slot):
        cp.wait()
    w1 = jnp.broadcast_to(wt_ref[:, 0:1], (tm, LANES))
    w2 = jnp.broadcast_to(wt_ref[:, 1:2], (tm, LANES))

    def write(o_ref):
        for s in range(ROW_SLABS):
            sl = slice(s * LANES, (s + 1) * LANES)
            mix = w1 * ybuf[slot, 0, :, sl] + w2 * ybuf[slot, 1, :, sl]
            o_ref[:, sl] = x_ref[:, sl] + gate_ref[:, sl] * mix

    if len(o_refs) == 1:
        write(o_refs[0])
    else:
        n_ctx_tiles = CTX_ROWS // tm

        @pl.when(i < n_ctx_tiles)
        def _():
            write(o_refs[0])

        @pl.when(i >= n_ctx_tiles)
        def _():
            write(o_refs[1])


def _combine(x, y_slabs, pos, wts, mod, layer, split_out):
    tm = 256
    n_ctx_tiles = CTX_ROWS // tm
    if split_out:
        out_specs = [pl.BlockSpec((tm, D), lambda i, pos: (jnp.minimum(i, n_ctx_tiles - 1), 0)),
                     pl.BlockSpec((tm, D), lambda i, pos: (jnp.maximum(i - n_ctx_tiles, 0), 0))]
        out_shape = [jax.ShapeDtypeStruct((CTX_ROWS, D), f32), jax.ShapeDtypeStruct((DEC_ROWS, D), f32)]
    else:
        out_specs = [pl.BlockSpec((tm, D), lambda i, pos: (i, 0))]
        out_shape = [jax.ShapeDtypeStruct((T, D), f32)]
    return pl.pallas_call(
        functools.partial(_combine_kernel, tm=tm, n_out=len(out_specs)),
        grid_spec=pltpu.PrefetchScalarGridSpec(
            num_scalar_prefetch=1,
            grid=(T // tm,),
            in_specs=[
                pl.BlockSpec((tm, D), lambda i, pos: (i, 0)),
                _mod_spec(layer, 5, tm),
                pl.BlockSpec((tm, LANES), lambda i, pos: (i, 0)),
                pl.BlockSpec(memory_space=pl.ANY),
            ],
            out_specs=out_specs,
            scratch_shapes=[pltpu.VMEM((2, TOP_K, tm, D), f32),
                            pltpu.SemaphoreType.DMA((2,))],
        ),
        out_shape=out_shape,
        compiler_params=_params(("arbitrary",),
                                (1 + len(out_specs)) * _nbytes((tm, D), f32) + _nbytes((tm, LANES), f32),
                                2 * _nbytes((TOP_K, tm, D), f32) + 4 * _nbytes((tm, LANES), f32)),
        name="moe_combine",
    )(pos, x, mod, wts, y_slabs)


def _moe_layer(x, mod, norm2, w_router, w_gate, w_up, w_down, layer, split_out):
    wr = jnp.pad(w_router, ((0, 0), (0, 0), (0, LANES - N_EXPERTS)))
    h, idx, wts, ranks, counts = _router(x, mod, norm2, wr, layer)
    src_token, pos, up_tiles, down_tiles = _routing_plan(idx[:, :TOP_K], ranks[:, :TOP_K], counts[0, :N_EXPERTS])
    hs = _dispatch(h, src_token, down_tiles[1])
    a = _moe_up(hs, w_gate, w_up, *up_tiles, layer)
    y = _moe_down(a, w_down, *down_tiles, layer)
    return _combine(x, y, pos, wts, mod, layer, split_out)


def kernel(x_prompt, x_sample, cache_k, cache_v, c, c_ctx, norm1, norm2, w_ada, b_ada, w_conv_in, conv_w,
           w_conv_out, w_qkv, q_norm, k_norm, rpb, w_attn_out, w_ffn_gate, w_ffn_up, w_ffn_down, w_router,
           w_moe_gate, w_moe_up, w_moe_down):
    x_parts = [x_prompt.reshape(CTX_ROWS, D), x_sample.reshape(DEC_ROWS, D)]
    cond = jnp.concatenate([c_ctx[None, :], c, jnp.zeros((N_GROUPS_PADDED - 1 - N_DEC_SEQ, D), f32)], axis=0)
    mod = _ada_params(cond, w_ada, b_ada)
    cache_k2 = cache_k.reshape(N_DEC_SEQ, -1, PAST_LEN, D)
    cache_v2 = cache_v.reshape(N_DEC_SEQ, -1, PAST_LEN, D)
    bias_blocks = _rel_bias_blocks(rpb)
    w_conv_in, w_conv_out, w_qkv, w_attn_out, w_ffn_gate, w_ffn_up, w_ffn_down = (
        w.astype(bf16) for w in (w_conv_in, w_conv_out, w_qkv, w_attn_out, w_ffn_gate, w_ffn_up, w_ffn_down))

    new_k, new_v = [], []
    for layer in range(DEPTH):
        jl = layer // 2
        if layer % 2 == 0:
            gated = []
            row0 = 0
            for part in x_parts:
                ctx_rows = min(max(CTX_ROWS - row0, 0), part.shape[0])
                b, u = _conv_in(part, mod, norm1, w_conv_in, layer, row0)
                gated.append(_conv_gate(b, u, conv_w, layer, ctx_rows))
                row0 += part.shape[0]
            conv_out_tn = 1024 // len(x_parts)
            x = _mm_residual(gated, w_conv_out, x_parts, mod, layer, 2, 1024, conv_out_tn, "conv_out")
            a = _ffn_up(x, mod, norm2, w_ffn_gate, w_ffn_up, layer)
            x = _mm_residual([a], w_ffn_down, [x], mod, layer, 5, 1024, 512, "ffn_down")
            x_parts = [x]
        else:
            qkv_ctx = _qkv(x, mod, norm1, w_qkv, q_norm, k_norm, layer, 0, CTX_ROWS, f32)
            qkv_dec = _qkv(x, mod, norm1, w_qkv, q_norm, k_norm, layer, CTX_ROWS, DEC_ROWS, bf16)
            o_ctx, k_ctx, v_ctx = _ctx_attention(qkv_ctx)
            new_k.append(k_ctx.reshape(N_CTX_SEQ, CTX_LEN, N_HEADS, HEAD_DIM))
            new_v.append(v_ctx.reshape(N_CTX_SEQ, CTX_LEN, N_HEADS, HEAD_DIM))
            o_dec = _latent_attention(qkv_dec, cache_k2, cache_v2, jl, bias_blocks)
            x = _mm_residual([o_ctx, o_dec], w_attn_out, [x], mod, layer, 2, 1024, 1024, "attn_out")
            x_parts = _moe_layer(x, mod, norm2, w_router, w_moe_gate, w_moe_up, w_moe_down, layer,
                                 split_out=layer == DEPTH - 1)
            x = x_parts[0]

    y_prompt = x_parts[0].reshape(N_CTX_SEQ, CTX_LEN, D)
    y_sample = x_parts[1].reshape(N_DEC_SEQ, DEC_LEN, D)
    return y_prompt, y_sample, jnp.stack(new_k, axis=1), jnp.stack(new_v, axis=1)
```

```python
import functools

import numpy as np
import jax
import jax.numpy as jnp
from jax import lax
from jax.experimental import pallas as pl
from jax.experimental.pallas import tpu as pltpu

D = 2048
N_CTX_SEQ = 16
CTX_LEN = 256
N_DEC_SEQ = 4
DEC_LEN = 2048
CTX_ROWS = N_CTX_SEQ * CTX_LEN
DEC_ROWS = N_DEC_SEQ * DEC_LEN
T = CTX_ROWS + DEC_ROWS
DEPTH = 4
PAST_LEN = 512
N_HEADS = 16
HEAD_DIM = 128
GRID_W = 64
GRID_ROWS = DEC_LEN // GRID_W
WIN_ROWS = 8
WIN_COLS = 16
D_FF = 5632
N_EXPERTS = 8
TOP_K = 2
EPS = 1e-6
NEG_INF = -1e30
ATTN_SCALE = HEAD_DIM ** -0.5
LOG2_E = 1.4426950408889634
SCORE_SCALE_LOG2 = ATTN_SCALE * LOG2_E

N_GROUPS_PADDED = 8
LANES = 128
SUBLANES = 8
MXU_COLS = 256
VMEM_CAP_BYTES = 56 * 2 ** 20
VMEM_HEADROOM = 4 * 2 ** 20

Q_ROWS_PER_BLOCK = 4
Q_BLOCK = Q_ROWS_PER_BLOCK * GRID_W
N_Q_BLOCKS = GRID_ROWS // Q_ROWS_PER_BLOCK
BAND_ROWS = Q_ROWS_PER_BLOCK + WIN_ROWS
BAND = BAND_ROWS * GRID_W
MASKED_BLOCK = 2 * WIN_ROWS - 1
N_BIAS_BLOCKS = MASKED_BLOCK + 1

MOE_UP_TM = 1024
MOE_DOWN_TM = 512
N_ASSIGN = T * TOP_K
MOE_ROWS = N_ASSIGN + N_EXPERTS * MOE_UP_TM
TILE_EMPTY, TILE_FIRST_HALF, TILE_FULL = 0, 1, 2
ROW_SLABS = D // LANES

bf16 = jnp.bfloat16
f32 = jnp.float32


def _mm(a, b):
    return jnp.dot(a, b, preferred_element_type=f32)


def _mm_nt(a, b):
    return lax.dot_general(a, b, (((1,), (1,)), ((), ())), preferred_element_type=f32)


def _nbytes(shape, dtype):
    return int(np.prod(shape)) * jnp.dtype(dtype).itemsize


def _params(semantics, pipelined_bytes, resident_bytes):
    need = 2 * pipelined_bytes + resident_bytes + VMEM_HEADROOM
    return pltpu.CompilerParams(dimension_semantics=semantics,
                                vmem_limit_bytes=int(min(max(need, 16 * 2 ** 20), VMEM_CAP_BYTES)))


def _group_of_tile(i, tm):
    n_ctx = CTX_ROWS // tm
    per_seq = DEC_LEN // tm
    return jnp.where(i < n_ctx, 0, 1 + (i - n_ctx) // per_seq)


def _mod_spec(layer, which, tm, width=D, tile0=0):
    def index(*grid_and_prefetch):
        i = tile0 + grid_and_prefetch[0]
        j = grid_and_prefetch[1] if width != D else 0
        return ((layer * N_GROUPS_PADDED + _group_of_tile(i, tm)) * 6 + which, 0, j)
    return pl.BlockSpec((None, 1, width), index)


def _norm_mod_rows(x, gain, scale1p, shift):
    ms = jnp.mean(x * x, axis=-1, keepdims=True)
    y = x * lax.rsqrt(ms + EPS)
    return (y * gain) * scale1p + shift


def _norm_mod_to_scratch(x_ref, gain_ref, sc_ref, sh_ref, h_scr, rows=256):
    gain = gain_ref[...]
    scale1p = 1.0 + sc_ref[...]
    shift = sh_ref[...]

    def body(c, carry):
        r = pl.multiple_of(c * rows, rows)
        h = _norm_mod_rows(x_ref[pl.ds(r, rows), :], gain, scale1p, shift)
        h_scr[pl.ds(r, rows), :] = h.astype(bf16)
        return carry

    lax.fori_loop(0, x_ref.shape[0] // rows, body, 0)


def _ada_kernel(cond_ref, w_ref, b_ref, o_ref):
    c = cond_ref[...]
    s = (c * jax.nn.sigmoid(c)).astype(bf16)
    o_ref[...] = _mm(s, w_ref[...].astype(bf16)) + b_ref[...]


def _ada_params(cond, w_ada, b_ada):
    tn = 1024
    out = pl.pallas_call(
        _ada_kernel,
        grid=(DEPTH, 6 * D // tn),
        in_specs=[
            pl.BlockSpec((N_GROUPS_PADDED, D), lambda l, j: (0, 0)),
            pl.BlockSpec((None, D, tn), lambda l, j: (l, 0, j)),
            pl.BlockSpec((None, 1, tn), lambda l, j: (l, 0, j)),
        ],
        out_specs=pl.BlockSpec((None, N_GROUPS_PADDED, tn), lambda l, j: (l, 0, j)),
        out_shape=jax.ShapeDtypeStruct((DEPTH, N_GROUPS_PADDED, 6 * D), f32),
        compiler_params=_params(("arbitrary", "arbitrary"),
                                _nbytes((D, tn), f32) + _nbytes((8, tn), f32) * 2,
                                _nbytes((D, tn), bf16) + _nbytes((8, D), f32) * 2),
        name="ada_params",
    )(cond, w_ada, b_ada.reshape(DEPTH, 1, 6 * D))
    return out.reshape(DEPTH * N_GROUPS_PADDED * 6, 1, D)


def _conv_in_kernel(x_ref, gain_ref, sc_ref, sh_ref, wb_ref, wc_ref, wv_ref, b_ref, u_ref, h_scr):
    @pl.when(pl.program_id(1) == 0)
    def _():
        _norm_mod_to_scratch(x_ref, gain_ref, sc_ref, sh_ref, h_scr)

    h = h_scr[...]
    b_ref[...] = _mm(h, wb_ref[...])
    u_ref[...] = _mm(h, wc_ref[...]) * _mm(h, wv_ref[...])


def _conv_in(x, mod, norm, w_conv_in, layer, row0=0):
    tm, tn = 1024, 512
    jl = layer // 2
    nb = D // tn
    n_rows = x.shape[0]
    tile0 = row0 // tm
    return pl.pallas_call(
        _conv_in_kernel,
        grid=(n_rows // tm, nb),
        in_specs=[
            pl.BlockSpec((tm, D), lambda i, j: (i, 0)),
            pl.BlockSpec((None, 1, D), lambda i, j: (layer, 0, 0)),
            _mod_spec(layer, 1, tm, tile0=tile0), _mod_spec(layer, 0, tm, tile0=tile0),
            pl.BlockSpec((None, D, tn), lambda i, j: (jl, 0, j)),
            pl.BlockSpec((None, D, tn), lambda i, j: (jl, 0, nb + j)),
            pl.BlockSpec((None, D, tn), lambda i, j: (jl, 0, 2 * nb + j)),
        ],
        out_specs=[pl.BlockSpec((tm, tn), lambda i, j: (i, j))] * 2,
        out_shape=[jax.ShapeDtypeStruct((n_rows, D), f32)] * 2,
        scratch_shapes=[pltpu.VMEM((tm, D), bf16)],
        compiler_params=_params(("arbitrary", "arbitrary"),
                                _nbytes((tm, D), f32) + 3 * _nbytes((D, tn), bf16) + 2 * _nbytes((tm, tn), f32),
                                _nbytes((tm, D), bf16) + 3 * _nbytes((tm, tn), f32)),
        name="conv_in",
    )(x, norm.reshape(DEPTH, 1, D), mod, mod, w_conv_in, w_conv_in, w_conv_in)


def _ffn_up_kernel(x_ref, gain_ref, sc_ref, sh_ref, wg_ref, wu_ref, a_ref, h_scr):
    @pl.when(pl.program_id(1) == 0)
    def _():
        _norm_mod_to_scratch(x_ref, gain_ref, sc_ref, sh_ref, h_scr)

    h = h_scr[...]
    g = _mm(h, wg_ref[...])
    u = _mm(h, wu_ref[...])
    a_ref[...] = ((g * jax.nn.sigmoid(g)) * u).astype(bf16)


def _ffn_up(x, mod, norm, w_gate, w_up, layer):
    tm, tn = 1024, 512
    jl = layer // 2
    return pl.pallas_call(
        _ffn_up_kernel,
        grid=(T // tm, D_FF // tn),
        in_specs=[
            pl.BlockSpec((tm, D), lambda i, j: (i, 0)),
            pl.BlockSpec((None, 1, D), lambda i, j: (layer, 0, 0)),
            _mod_spec(layer, 4, tm), _mod_spec(layer, 3, tm),
            pl.BlockSpec((None, D, tn), lambda i, j: (jl, 0, j)),
            pl.BlockSpec((None, D, tn), lambda i, j: (jl, 0, j)),
        ],
        out_specs=pl.BlockSpec((tm, tn), lambda i, j: (i, j)),
        out_shape=jax.ShapeDtypeStruct((T, D_FF), bf16),
        scratch_shapes=[pltpu.VMEM((tm, D), bf16)],
        compiler_params=_params(("arbitrary", "arbitrary"),
                                _nbytes((tm, D), f32) + 2 * _nbytes((D, tn), bf16) + _nbytes((tm, tn), bf16),
                                _nbytes((tm, D), bf16) + 3 * _nbytes((tm, tn), f32)),
        name="ffn_up",
    )(x, norm.reshape(DEPTH, 1, D), mod, mod, w_gate, w_up)


def _qkv_kernel(x_ref, gain_ref, sc_ref, sh_ref, w_ref, qg_ref, kg_ref, o_ref, h_scr, *, n_q_blocks):
    j = pl.program_id(1)

    @pl.when(j == 0)
    def _():
        _norm_mod_to_scratch(x_ref, gain_ref, sc_ref, sh_ref, h_scr)

    is_qk = j < 2 * n_q_blocks
    head_gain = jnp.where(j < n_q_blocks, qg_ref[...], kg_ref[...])
    h = h_scr[...]
    for c in range(0, o_ref.shape[1], MXU_COLS):
        acc = _mm(h, w_ref[:, c:c + MXU_COLS])
        for hh in range(MXU_COLS // HEAD_DIM):
            a = acc[:, hh * HEAD_DIM:(hh + 1) * HEAD_DIM]
            ms = jnp.mean(a * a, axis=-1, keepdims=True)
            normed = (a * lax.rsqrt(ms + EPS)) * head_gain
            cols = slice(c + hh * HEAD_DIM, c + (hh + 1) * HEAD_DIM)
            o_ref[:, cols] = jnp.where(is_qk, normed, a).astype(o_ref.dtype)


def _qkv(x, mod, norm, w_qkv, q_norm, k_norm, layer, row0, n_rows, out_dtype):
    tm, tn = 1024, 1024
    jl = layer // 2
    tile0 = row0 // tm
    return pl.pallas_call(
        functools.partial(_qkv_kernel, n_q_blocks=D // tn),
        grid=(n_rows // tm, 3 * D // tn),
        in_specs=[
            pl.BlockSpec((tm, D), lambda i, j: (tile0 + i, 0)),
            pl.BlockSpec((None, 1, D), lambda i, j: (layer, 0, 0)),
            _mod_spec(layer, 1, tm, tile0=tile0), _mod_spec(layer, 0, tm, tile0=tile0),
            pl.BlockSpec((None, D, tn), lambda i, j: (jl, 0, j)),
            pl.BlockSpec((None, 1, HEAD_DIM), lambda i, j: (jl, 0, 0)),
            pl.BlockSpec((None, 1, HEAD_DIM), lambda i, j: (jl, 0, 0)),
        ],
        out_specs=pl.BlockSpec((tm, tn), lambda i, j: (i, j)),
        out_shape=jax.ShapeDtypeStruct((n_rows, 3 * D), out_dtype),
        scratch_shapes=[pltpu.VMEM((tm, D), bf16)],
        compiler_params=_params(("arbitrary", "arbitrary"),
                                _nbytes((tm, D), f32) + _nbytes((D, tn), bf16) + _nbytes((tm, tn), out_dtype),
                                _nbytes((tm, D), bf16) + 2 * _nbytes((tm, tn), f32)),
        name="qkv",
    )(x, norm.reshape(DEPTH, 1, D), mod, mod, w_qkv,
      q_norm.reshape(-1, 1, HEAD_DIM), k_norm.reshape(-1, 1, HEAD_DIM))


def _mm_res_kernel(*refs, n_a, n_ctx_tiles):
    a_refs, (w_ref, *x_refs, gate_ref, o_ref) = refs[:n_a], refs[n_a:]

    def update(a_ref, x_ref):
        o_ref[...] = x_ref[...] + gate_ref[...] * _mm(a_ref[...], w_ref[...])

    if len(a_refs) == 1 and len(x_refs) == 1:
        update(a_refs[0], x_refs[0])
    else:
        i = pl.program_id(0)

        @pl.when(i < n_ctx_tiles)
        def _():
            update(a_refs[0], x_refs[0])

        @pl.when(i >= n_ctx_tiles)
        def _():
            update(a_refs[-1], x_refs[-1])


def _row_part_specs(parts, tm, width, n_ctx_tiles):
    col = (lambda j: 0) if width == parts[0].shape[1] else (lambda j: j)
    if len(parts) == 1:
        return [pl.BlockSpec((tm, width), lambda i, j: (i, col(j)))]
    return [pl.BlockSpec((tm, width), lambda i, j: (jnp.minimum(i, n_ctx_tiles - 1), col(j))),
            pl.BlockSpec((tm, width), lambda i, j: (jnp.maximum(i - n_ctx_tiles, 0), col(j)))]


def _mm_residual(a_parts, w, x_parts, mod, layer, which, tm, tn, name):
    k = a_parts[0].shape[1]
    jl = layer // 2
    n_ctx_tiles = CTX_ROWS // tm
    return pl.pallas_call(
        functools.partial(_mm_res_kernel, n_a=len(a_parts), n_ctx_tiles=n_ctx_tiles),
        grid=(T // tm, D // tn),
        in_specs=_row_part_specs(a_parts, tm, k, n_ctx_tiles) + [
            pl.BlockSpec((None, k, tn), lambda i, j: (jl, 0, j)),
        ] + _row_part_specs(x_parts, tm, tn, n_ctx_tiles) + [
            _mod_spec(layer, which, tm, tn),
        ],
        out_specs=pl.BlockSpec((tm, tn), lambda i, j: (i, j)),
        out_shape=jax.ShapeDtypeStruct((T, D), f32),
        compiler_params=_params(("arbitrary", "arbitrary"),
                                len(a_parts) * _nbytes((tm, k), bf16) + _nbytes((k, tn), bf16)
                                + (1 + len(x_parts)) * _nbytes((tm, tn), f32),
                                2 * _nbytes((tm, tn), f32)),
        name=name,
    )(*a_parts, w, *x_parts, mod)


def _conv_gate_kernel(b_ref, u_ref, up_ref, un_ref, cw_ref, o_ref, *, tm, cols, ctx_rows):
    row0 = pl.program_id(0) * tm
    seq_len = jnp.where(row0 < ctx_rows, CTX_LEN, DEC_LEN)
    local = lax.broadcasted_iota(jnp.int32, (tm, cols), 0)
    pos = (row0 + local) & (seq_len - 1)
    is_first = pos == 0
    is_last = pos == seq_len - 1
    for c in range(D // cols):
        sl = slice(c * cols, (c + 1) * cols)
        u = u_ref[:, sl]
        prev_row = up_ref[SUBLANES - 1:SUBLANES, sl]
        next_row = un_ref[0:1, sl]
        u_prev = jnp.where(local == 0, prev_row, pltpu.roll(u, 1, axis=0))
        u_next = jnp.where(local == tm - 1, next_row, pltpu.roll(u, tm - 1, axis=0))
        u_prev = jnp.where(is_first, 0.0, u_prev)
        u_next = jnp.where(is_last, 0.0, u_next)
        conv = u_prev * cw_ref[0:1, sl] + u * cw_ref[1:2, sl] + u_next * cw_ref[2:3, sl]
        o_ref[:, sl] = (b_ref[:, sl] * conv).astype(bf16)


def _conv_gate(b, u, conv_w, layer, ctx_rows):
    tm, cols = 512, 512
    jl = layer // 2
    blocks_per_tile = tm // SUBLANES
    n_rows = b.shape[0]
    return pl.pallas_call(
        functools.partial(_conv_gate_kernel, tm=tm, cols=cols, ctx_rows=ctx_rows),
        grid=(n_rows // tm,),
        in_specs=[
            pl.BlockSpec((tm, D), lambda i: (i, 0)),
            pl.BlockSpec((tm, D), lambda i: (i, 0)),
            pl.BlockSpec((SUBLANES, D), lambda i: (jnp.maximum(i * blocks_per_tile - 1, 0), 0)),
            pl.BlockSpec((SUBLANES, D),
                         lambda i: (jnp.minimum((i + 1) * blocks_per_tile, n_rows // SUBLANES - 1), 0)),
            pl.BlockSpec((None, 3, D), lambda i: (jl, 0, 0)),
        ],
        out_specs=pl.BlockSpec((tm, D), lambda i: (i, 0)),
        out_shape=jax.ShapeDtypeStruct((n_rows, D), bf16),
        compiler_params=_params(("arbitrary",),
                                2 * _nbytes((tm, D), f32) + _nbytes((tm, D), bf16) + 2 * _nbytes((SUBLANES, D), f32),
                                8 * _nbytes((tm, cols), f32)),
        name="conv_gate",
    )(b, u, u, u, conv_w)


def _softmax_pv(parts):
    m = parts[0][0].max(axis=-1, keepdims=True)
    for s, _ in parts[1:]:
        m = jnp.maximum(m, s.max(axis=-1, keepdims=True))
    acc = None
    for s, v in parts:
        p = jnp.exp2(s - m).astype(bf16)
        v_and_ones = jnp.concatenate([v, jnp.ones_like(v)], axis=1)
        o = _mm(p, v_and_ones)
        acc = o if acc is None else acc + o
    head_dim = acc.shape[1] // 2
    return acc[:, :head_dim] * (1.0 / acc[:, head_dim:head_dim + 1])


def _ctx_attn_kernel(q_ref, k_ref, v_ref, o_ref, k_out_ref, v_out_ref):
    k_out_ref[...] = k_ref[...]
    v_out_ref[...] = v_ref[...]
    for hh in range(q_ref.shape[1] // HEAD_DIM):
        sl = slice(hh * HEAD_DIM, (hh + 1) * HEAD_DIM)
        q = q_ref[:, sl].astype(bf16)
        k = k_ref[:, sl].astype(bf16)
        v = v_ref[:, sl].astype(bf16)
        s = _mm_nt(q, k) * SCORE_SCALE_LOG2
        o_ref[:, sl] = _softmax_pv([(s, v)]).astype(bf16)


def _ctx_attention(qkv):
    tn = 512
    nb = D // tn
    return pl.pallas_call(
        _ctx_attn_kernel,
        grid=(N_CTX_SEQ, nb),
        in_specs=[
            pl.BlockSpec((CTX_LEN, tn), lambda b, g: (b, g)),
            pl.BlockSpec((CTX_LEN, tn), lambda b, g: (b, nb + g)),
            pl.BlockSpec((CTX_LEN, tn), lambda b, g: (b, 2 * nb + g)),
        ],
        out_specs=[pl.BlockSpec((CTX_LEN, tn), lambda b, g: (b, g))] * 3,
        out_shape=[jax.ShapeDtypeStruct((CTX_ROWS, D), bf16),
                   jax.ShapeDtypeStruct((CTX_ROWS, D), f32),
                   jax.ShapeDtypeStruct((CTX_ROWS, D), f32)],
        compiler_params=_params(("arbitrary", "arbitrary"),
                                5 * _nbytes((CTX_LEN, tn), f32) + _nbytes((CTX_LEN, tn), bf16),
                                8 * _nbytes((CTX_LEN, CTX_LEN), f32)),
        name="ctx_attention",
    )(qkv, qkv, qkv)


def _band_start_row(m):
    return jnp.clip(Q_ROWS_PER_BLOCK * m - WIN_ROWS // 2, 0, GRID_ROWS - BAND_ROWS)


def _band_block_ids(m):
    band0 = _band_start_row(m)
    ids = []
    for rho in range(Q_ROWS_PER_BLOCK):
        r = Q_ROWS_PER_BLOCK * m + rho
        win0 = jnp.clip(r - WIN_ROWS // 2, 0, GRID_ROWS - WIN_ROWS)
        row_ids = []
        for kap in range(BAND_ROWS):
            kr = band0 + kap
            inside = (kr >= win0) & (kr < win0 + WIN_ROWS)
            row_ids.append(jnp.where(inside, kr - r + WIN_ROWS - 1, MASKED_BLOCK))
        ids.append(row_ids)
    return ids


def _band_bias(bias_ref, hh, ids):
    left_half = lax.broadcasted_iota(jnp.int32, (GRID_W, 2 * GRID_W), 1) < GRID_W
    rows = []
    for rho in range(Q_ROWS_PER_BLOCK):
        pairs = [jnp.where(left_half, bias_ref[hh, ids[rho][kap]], bias_ref[hh, ids[rho][kap + 1]])
                 for kap in range(0, BAND_ROWS, 2)]
        rows.append(jnp.concatenate(pairs, axis=1))
    return jnp.concatenate(rows, axis=0)


def _loc_attn_kernel(q_ref, k_ref, v_ref, ck_ref, cv_ref, bias_ref, o_ref):
    start = pl.multiple_of(_band_start_row(pl.program_id(2)) * GRID_W, GRID_W)
    ids = _band_block_ids(pl.program_id(2))
    for hh in range(q_ref.shape[1] // HEAD_DIM):
        sl = slice(hh * HEAD_DIM, (hh + 1) * HEAD_DIM)
        q = q_ref[:, sl]
        k_band = k_ref[pl.ds(start, BAND), sl]
        v_band = v_ref[pl.ds(start, BAND), sl]
        k_ctx = ck_ref[:, sl].astype(bf16)
        v_ctx = cv_ref[:, sl].astype(bf16)
        s_loc = _mm_nt(q, k_band) * SCORE_SCALE_LOG2 + _band_bias(bias_ref, hh, ids)
        s_ctx = _mm_nt(q, k_ctx) * SCORE_SCALE_LOG2
        o_ref[:, sl] = _softmax_pv([(s_loc, v_band), (s_ctx, v_ctx)]).astype(bf16)


def _rel_bias_blocks(rpb_all):
    cols = np.arange(GRID_W)
    col_start = np.clip(cols - WIN_COLS // 2, 0, GRID_W - WIN_COLS)
    col_ok = (cols[None, :] >= col_start[:, None]) & (cols[None, :] < col_start[:, None] + WIN_COLS)
    pad = GRID_W - WIN_COLS
    n = rpb_all.shape[0] * rpb_all.shape[1]
    padded = jnp.pad(rpb_all.astype(f32).reshape(n, MASKED_BLOCK, -1), ((0, 0), (0, 0), (pad, pad)))
    skewed = jnp.tile(padded, (1, 1, GRID_W + 1))[:, :, :GRID_W * 2 * GRID_W]
    skewed = skewed.reshape(n, MASKED_BLOCK, GRID_W, 2 * GRID_W)[..., :GRID_W]
    toeplitz = jnp.flip(skewed, axis=2)
    toeplitz = jnp.where(col_ok[None, None], toeplitz * LOG2_E, NEG_INF)
    masked = jnp.full((n, 1, GRID_W, GRID_W), NEG_INF, f32)
    blocks = jnp.concatenate([toeplitz, masked], axis=1)
    return jnp.concatenate([blocks, blocks], axis=3)


def _latent_attention(qkv, cache_k, cache_v, layer_slot, bias_blocks):
    tn = 512
    nb = D // tn
    hb = tn // HEAD_DIM
    return pl.pallas_call(
        _loc_attn_kernel,
        grid=(N_DEC_SEQ, nb, N_Q_BLOCKS),
        in_specs=[
            pl.BlockSpec((Q_BLOCK, tn), lambda b, g, m: (b * N_Q_BLOCKS + m, g)),
            pl.BlockSpec((DEC_LEN, tn), lambda b, g, m: (b, nb + g)),
            pl.BlockSpec((DEC_LEN, tn), lambda b, g, m: (b, 2 * nb + g)),
            pl.BlockSpec((None, None, PAST_LEN, tn), lambda b, g, m: (b, layer_slot, 0, g)),
            pl.BlockSpec((None, None, PAST_LEN, tn), lambda b, g, m: (b, layer_slot, 0, g)),
            pl.BlockSpec((hb, N_BIAS_BLOCKS, GRID_W, 2 * GRID_W), lambda b, g, m: (layer_slot * nb + g, 0, 0, 0)),
        ],
        out_specs=pl.BlockSpec((Q_BLOCK, tn), lambda b, g, m: (b * N_Q_BLOCKS + m, g)),
        out_shape=jax.ShapeDtypeStruct((DEC_ROWS, D), bf16),
        compiler_params=_params(("arbitrary", "arbitrary", "arbitrary"),
                                _nbytes((Q_BLOCK, tn), bf16) + 2 * _nbytes((DEC_LEN, tn), bf16)
                                + 2 * _nbytes((PAST_LEN, tn), f32)
                                + _nbytes((hb, N_BIAS_BLOCKS, GRID_W, 2 * GRID_W), f32)
                                + _nbytes((Q_BLOCK, tn), bf16),
                                10 * _nbytes((Q_BLOCK, BAND), f32)),
        name="latent_attention",
    )(qkv, qkv, qkv, cache_k, cache_v, bias_blocks)


def _router_kernel(x_ref, gain_ref, sc_ref, sh_ref, wr_ref, h_ref, idx_ref, wt_ref, rank_ref, cnt_ref, seen):
    tm = x_ref.shape[0]

    @pl.when(pl.program_id(0) == 0)
    def _():
        seen[...] = jnp.zeros(seen.shape, seen.dtype)

    h = _norm_mod_rows(x_ref[...], gain_ref[...], 1.0 + sc_ref[...], sh_ref[...])
    h_ref[...] = h
    w = wr_ref[...]
    h_hi, w_hi = h.astype(bf16), w.astype(bf16)
    h_lo = (h - h_hi.astype(f32)).astype(bf16)
    w_lo = (w - w_hi.astype(f32)).astype(bf16)
    logits = _mm(h_hi, w_hi) + (_mm(h_lo, w_hi) + _mm(h_hi, w_lo))
    lane = lax.broadcasted_iota(jnp.int32, (tm, LANES), 1)
    l1 = jnp.where(lane < N_EXPERTS, logits, -jnp.inf)
    m1 = l1.max(axis=-1, keepdims=True)
    i1 = jnp.where(l1 == m1, lane, LANES).min(axis=-1, keepdims=True)
    l2 = jnp.where(lane == i1, -jnp.inf, l1)
    m2 = l2.max(axis=-1, keepdims=True)
    i2 = jnp.where(l2 == m2, lane, LANES).min(axis=-1, keepdims=True)
    e = jnp.exp(m2 - m1)
    w1 = 1.0 / (1.0 + e)
    w2 = e / (1.0 + e)
    idx_ref[...] = jnp.where(lane == 0, i1, jnp.where(lane == 1, i2, 0))
    wt_ref[...] = jnp.where(lane == 0, w1, jnp.where(lane == 1, w2, 0.0))

    first = lane == i1
    second = lane == i2
    chosen = (first | second).astype(bf16)
    earlier = (lax.broadcasted_iota(jnp.int32, (tm, tm), 1) < lax.broadcasted_iota(jnp.int32, (tm, tm), 0))
    before = _mm(earlier.astype(bf16), chosen) + seen[...]
    r1 = jnp.where(first, before, 0.0).sum(axis=-1, keepdims=True)
    r2 = jnp.where(second, before, 0.0).sum(axis=-1, keepdims=True)
    rank_ref[...] = jnp.where(lane == 0, r1, jnp.where(lane == 1, r2, 0.0)).astype(jnp.int32)
    seen[...] += chosen.astype(f32).sum(axis=0, keepdims=True)
    cnt_ref[...] = seen[...].astype(jnp.int32)


def _router(x, mod, norm, w_router_padded, layer):
    tm = 512
    jl = layer // 2
    lane_block = pl.BlockSpec((tm, LANES), lambda i: (i, 0))
    return pl.pallas_call(
        _router_kernel,
        grid=(T // tm,),
        in_specs=[
            pl.BlockSpec((tm, D), lambda i: (i, 0)),
            pl.BlockSpec((None, 1, D), lambda i: (layer, 0, 0)),
            _mod_spec(layer, 4, tm), _mod_spec(layer, 3, tm),
            pl.BlockSpec((None, D, LANES), lambda i: (jl, 0, 0)),
        ],
        out_specs=[pl.BlockSpec((tm, D), lambda i: (i, 0)), lane_block, lane_block, lane_block,
                   pl.BlockSpec((1, LANES), lambda i: (0, 0))],
        out_shape=[jax.ShapeDtypeStruct((T, D), f32),
                   jax.ShapeDtypeStruct((T, LANES), jnp.int32),
                   jax.ShapeDtypeStruct((T, LANES), f32),
                   jax.ShapeDtypeStruct((T, LANES), jnp.int32),
                   jax.ShapeDtypeStruct((1, LANES), jnp.int32)],
        scratch_shapes=[pltpu.VMEM((1, LANES), f32)],
        compiler_params=_params(("arbitrary",),
                                2 * _nbytes((tm, D), f32) + 3 * _nbytes((tm, LANES), f32),
                                _nbytes((D, LANES), f32) + 4 * _nbytes((tm, D), f32) + 2 * _nbytes((tm, tm), f32)),
        name="router",
    )(x, norm.reshape(DEPTH, 1, D), mod, mod, w_router_padded)


def _tile_table(tm, group_start, group_end, counts):
    experts = jnp.arange(N_EXPERTS, dtype=jnp.int32)
    tile_row0 = jnp.arange(MOE_ROWS // tm, dtype=jnp.int32) * tm
    tile_expert = jnp.sum((tile_row0[:, None] >= group_end[None, :]).astype(jnp.int32), axis=1)
    tile_expert = jnp.minimum(tile_expert, N_EXPERTS - 1)
    rows_end = jnp.sum(jnp.where(tile_expert[:, None] == experts[None, :], (group_start + counts)[None, :], 0), axis=1)
    rows_here = rows_end - tile_row0
    fill = jnp.where(rows_here <= 0, TILE_EMPTY, jnp.where(rows_here <= tm // 2, TILE_FIRST_HALF, TILE_FULL))
    tile_ids = jnp.arange(MOE_ROWS // tm, dtype=jnp.int32)
    input_block = lax.cummax(jnp.where(fill != TILE_EMPTY, tile_ids, 0), axis=0)
    return tile_expert.astype(jnp.int32), fill.astype(jnp.int32), input_block.astype(jnp.int32)


def _routing_plan(expert_ids, ranks, counts):
    e_flat = expert_ids.reshape(-1)
    span = ((counts + MOE_UP_TM - 1) // MOE_UP_TM) * MOE_UP_TM
    group_end = jnp.cumsum(span)
    group_start = group_end - span
    onehot = e_flat[:, None] == jnp.arange(N_EXPERTS, dtype=jnp.int32)[None, :]
    dest = ranks.reshape(-1) + jnp.sum(jnp.where(onehot, group_start[None, :], 0), axis=1)
    src_token = jnp.zeros((MOE_ROWS,), jnp.int32).at[dest].set(
        jnp.arange(N_ASSIGN, dtype=jnp.int32) // TOP_K, unique_indices=True)
    up_tiles = _tile_table(MOE_UP_TM, group_start, group_end, counts)
    down_tiles = _tile_table(MOE_DOWN_TM, group_start, group_end, counts)
    return src_token, dest.astype(jnp.int32), up_tiles, down_tiles


def _dispatch_kernel(src_ref, fill_ref, h_hbm, o_ref, buf, sem):
    i = pl.program_id(0)
    n_steps = pl.num_programs(0)
    chunk = o_ref.shape[0]

    def tile_copies(tile, slot):
        return [pltpu.make_async_copy(h_hbm.at[pl.ds(src_ref[tile * chunk + r], 1), :],
                                      buf.at[slot, pl.ds(r, 1), :], sem.at[slot]) for r in range(chunk)]

    def start_tile(tile, slot):
        for cp in tile_copies(tile, slot):
            cp.start()

    @pl.when((i == 0) & (fill_ref[0] != TILE_EMPTY))
    def _():
        start_tile(0, 0)

    @pl.when((i + 1 < n_steps) & (fill_ref[jnp.minimum(i + 1, n_steps - 1)] != TILE_EMPTY))
    def _():
        start_tile(i + 1, (i + 1) % 2)

    @pl.when(fill_ref[i] != TILE_EMPTY)
    def _():
        slot = i % 2
        for cp in tile_copies(i, slot):
            cp.wait()
        o_ref[...] = buf[slot].astype(bf16)

    @pl.when(fill_ref[i] == TILE_EMPTY)
    def _():
        o_ref[...] = jnp.zeros(o_ref.shape, o_ref.dtype)


def _dispatch(h, src_token, tile_fill):
    chunk = MOE_DOWN_TM
    return pl.pallas_call(
        _dispatch_kernel,
        grid_spec=pltpu.PrefetchScalarGridSpec(
            num_scalar_prefetch=2,
            grid=(MOE_ROWS // chunk,),
            in_specs=[pl.BlockSpec(memory_space=pl.ANY)],
            out_specs=pl.BlockSpec((chunk, D), lambda i, src, fill: (i, 0)),
            scratch_shapes=[pltpu.VMEM((2, chunk, D), f32), pltpu.SemaphoreType.DMA((2,))],
        ),
        out_shape=jax.ShapeDtypeStruct((MOE_ROWS, D), bf16),
        compiler_params=_params(("arbitrary",), _nbytes((chunk, D), bf16), 3 * _nbytes((chunk, D), f32)),
        name="moe_dispatch",
    )(src_token, tile_fill, h)


def _expert_changed(te_ref, i):
    return (i == 0) | (te_ref[i] != te_ref[jnp.maximum(i - 1, 0)])


def _for_tile_rows(i, fill_ref, o_ref, rows_fn):
    tm = o_ref.shape[0]
    fill = fill_ref[i]

    @pl.when(fill == TILE_FULL)
    def _():
        o_ref[...] = rows_fn(slice(None))

    @pl.when(fill == TILE_FIRST_HALF)
    def _():
        o_ref[:tm // 2] = rows_fn(slice(0, tm // 2))
        o_ref[tm // 2:] = jnp.zeros((tm // 2, o_ref.shape[1]), o_ref.dtype)

    @pl.when(fill == TILE_EMPTY)
    def _():
        o_ref[...] = jnp.zeros(o_ref.shape, o_ref.dtype)


def _moe_up_kernel(te_ref, fill_ref, src_ref, hs_ref, wg_ref, wu_ref, a_ref, wg_s, wu_s):
    i = pl.program_id(1)

    @pl.when(_expert_changed(te_ref, i))
    def _():
        wg_s[...] = wg_ref[...].astype(bf16)
        wu_s[...] = wu_ref[...].astype(bf16)

    def gated(rows):
        h = hs_ref[rows]
        g = _mm(h, wg_s[...])
        u = _mm(h, wu_s[...])
        return ((g * jax.nn.sigmoid(g)) * u).astype(bf16)

    _for_tile_rows(i, fill_ref, a_ref, gated)


def _moe_up(hs, w_gate, w_up, tile_expert, tile_fill, input_block, layer):
    tm, tn = MOE_UP_TM, 512
    jl = layer // 2
    return pl.pallas_call(
        _moe_up_kernel,
        grid_spec=pltpu.PrefetchScalarGridSpec(
            num_scalar_prefetch=3,
            grid=(D_FF // tn, MOE_ROWS // tm),
            in_specs=[
                pl.BlockSpec((tm, D), lambda j, i, te, fill, src: (src[i], 0)),
                pl.BlockSpec((None, None, D, tn), lambda j, i, te, fill, src: (jl, te[i], 0, j)),
                pl.BlockSpec((None, None, D, tn), lambda j, i, te, fill, src: (jl, te[i], 0, j)),
            ],
            out_specs=pl.BlockSpec((tm, tn), lambda j, i, te, fill, src: (i, j)),
            scratch_shapes=[pltpu.VMEM((D, tn), bf16), pltpu.VMEM((D, tn), bf16)],
        ),
        out_shape=jax.ShapeDtypeStruct((MOE_ROWS, D_FF), bf16),
        compiler_params=_params(("arbitrary", "arbitrary"),
                                _nbytes((tm, D), bf16) + 2 * _nbytes((D, tn), f32) + _nbytes((tm, tn), bf16),
                                2 * _nbytes((D, tn), bf16) + 3 * _nbytes((tm, tn), f32)),
        name="moe_up",
    )(tile_expert, tile_fill, input_block, hs, w_gate, w_up)


def _moe_down_kernel(te_ref, fill_ref, src_ref, a_ref, wd_ref, y_ref, wd_s):
    i = pl.program_id(1)

    @pl.when(_expert_changed(te_ref, i))
    def _():
        wd_s[...] = wd_ref[...].astype(bf16)

    _for_tile_rows(i, fill_ref, y_ref, lambda rows: _mm(a_ref[rows], wd_s[...]))


def _moe_down(a, w_down, tile_expert, tile_fill, input_block, layer):
    tm, tn = MOE_DOWN_TM, 512
    jl = layer // 2
    return pl.pallas_call(
        _moe_down_kernel,
        grid_spec=pltpu.PrefetchScalarGridSpec(
            num_scalar_prefetch=3,
            grid=(D // tn, MOE_ROWS // tm),
            in_specs=[
                pl.BlockSpec((tm, D_FF), lambda j, i, te, fill, src: (src[i], 0)),
                pl.BlockSpec((None, None, D_FF, tn), lambda j, i, te, fill, src: (jl, te[i], 0, j)),
            ],
            out_specs=pl.BlockSpec((tm, tn), lambda j, i, te, fill, src: (i, j)),
            scratch_shapes=[pltpu.VMEM((D_FF, tn), bf16)],
        ),
        out_shape=jax.ShapeDtypeStruct((MOE_ROWS, D), f32),
        compiler_params=_params(("arbitrary", "arbitrary"),
                                _nbytes((tm, D_FF), bf16) + _nbytes((D_FF, tn), f32) + _nbytes((tm, tn), f32),
                                _nbytes((D_FF, tn), bf16) + _nbytes((tm, tn), f32)),
        name="moe_down",
    )(tile_expert, tile_fill, input_block, a, w_down)


def _combine_kernel(pos_ref, x_ref, gate_ref, wt_ref, y_hbm, *rest, tm, n_out):
    o_refs, (ybuf, sem) = rest[:n_out], rest[n_out:]
    i = pl.program_id(0)
    n_steps = pl.num_programs(0)

    def tile_copies(tile, slot):
        return [pltpu.make_async_copy(y_hbm.at[pl.ds(pos_ref[(tile * tm + t) * TOP_K + c], 1), :],
                                      ybuf.at[slot, c, pl.ds(t, 1), :], sem.at[slot])
                for t in range(tm) for c in range(TOP_K)]

    def start_tile(tile, slot):
        for cp in tile_copies(tile, slot):
            cp.start()

    @pl.when(i == 0)
    def _():
        start_tile(0, 0)

    @pl.when(i + 1 < n_steps)
    def _():
        start_tile(i + 1, (i + 1) % 2)

    slot = i % 2
    for cp in tile_copies(i, slot):
        cp.wait()
    w1 = jnp.broadcast_to(wt_ref[:, 0:1], (tm, LANES))
    w2 = jnp.broadcast_to(wt_ref[:, 1:2], (tm, LANES))

    def write(o_ref):
        for s in range(ROW_SLABS):
            sl = slice(s * LANES, (s + 1) * LANES)
            mix = w1 * ybuf[slot, 0, :, sl] + w2 * ybuf[slot, 1, :, sl]
            o_ref[:, sl] = x_ref[:, sl] + gate_ref[:, sl] * mix

    if len(o_refs) == 1:
        write(o_refs[0])
    else:
        n_ctx_tiles = CTX_ROWS // tm

        @pl.when(i < n_ctx_tiles)
        def _():
            write(o_refs[0])

        @pl.when(i >= n_ctx_tiles)
        def _():
            write(o_refs[1])


def _combine(x, y_slabs, pos, wts, mod, layer, split_out):
    tm = 256
    n_ctx_tiles = CTX_ROWS // tm
    if split_out:
        out_specs = [pl.BlockSpec((tm, D), lambda i, pos: (jnp.minimum(i, n_ctx_tiles - 1), 0)),
                     pl.BlockSpec((tm, D), lambda i, pos: (jnp.maximum(i - n_ctx_tiles, 0), 0))]
        out_shape = [jax.ShapeDtypeStruct((CTX_ROWS, D), f32), jax.ShapeDtypeStruct((DEC_ROWS, D), f32)]
    else:
        out_specs = [pl.BlockSpec((tm, D), lambda i, pos: (i, 0))]
        out_shape = [jax.ShapeDtypeStruct((T, D), f32)]
    return pl.pallas_call(
        functools.partial(_combine_kernel, tm=tm, n_out=len(out_specs)),
        grid_spec=pltpu.PrefetchScalarGridSpec(
            num_scalar_prefetch=1,
            grid=(T // tm,),
            in_specs=[
                pl.BlockSpec((tm, D), lambda i, pos: (i, 0)),
                _mod_spec(layer, 5, tm),
                pl.BlockSpec((tm, LANES), lambda i, pos: (i, 0)),
                pl.BlockSpec(memory_space=pl.ANY),
            ],
            out_specs=out_specs,
            scratch_shapes=[pltpu.VMEM((2, TOP_K, tm, D), f32),
                            pltpu.SemaphoreType.DMA((2,))],
        ),
        out_shape=out_shape,
        compiler_params=_params(("arbitrary",),
                                (1 + len(out_specs)) * _nbytes((tm, D), f32) + _nbytes((tm, LANES), f32),
                                2 * _nbytes((TOP_K, tm, D), f32) + 4 * _nbytes((tm, LANES), f32)),
        name="moe_combine",
    )(pos, x, mod, wts, y_slabs)


def _moe_layer(x, mod, norm2, w_router, w_gate, w_up, w_down, layer, split_out):
    wr = jnp.pad(w_router, ((0, 0), (0, 0), (0, LANES - N_EXPERTS)))
    h, idx, wts, ranks, counts = _router(x, mod, norm2, wr, layer)
    src_token, pos, up_tiles, down_tiles = _routing_plan(idx[:, :TOP_K], ranks[:, :TOP_K], counts[0, :N_EXPERTS])
    hs = _dispatch(h, src_token, down_tiles[1])
    a = _moe_up(hs, w_gate, w_up, *up_tiles, layer)
    y = _moe_down(a, w_down, *down_tiles, layer)
    return _combine(x, y, pos, wts, mod, layer, split_out)


def kernel(x_prompt, x_sample, cache_k, cache_v, c, c_ctx, norm1, norm2, w_ada, b_ada, w_conv_in, conv_w,
           w_conv_out, w_qkv, q_norm, k_norm, rpb, w_attn_out, w_ffn_gate, w_ffn_up, w_ffn_down, w_router,
           w_moe_gate, w_moe_up, w_moe_down):
    x_parts = [x_prompt.reshape(CTX_ROWS, D), x_sample.reshape(DEC_ROWS, D)]
    cond = jnp.concatenate([c_ctx[None, :], c, jnp.zeros((N_GROUPS_PADDED - 1 - N_DEC_SEQ, D), f32)], axis=0)
    mod = _ada_params(cond, w_ada, b_ada)
    cache_k2 = cache_k.reshape(N_DEC_SEQ, -1, PAST_LEN, D)
    cache_v2 = cache_v.reshape(N_DEC_SEQ, -1, PAST_LEN, D)
    bias_blocks = _rel_bias_blocks(rpb)
    w_conv_in, w_conv_out, w_qkv, w_attn_out, w_ffn_gate, w_ffn_up, w_ffn_down = (
        w.astype(bf16) for w in (w_conv_in, w_conv_out, w_qkv, w_attn_out, w_ffn_gate, w_ffn_up, w_ffn_down))

    new_k, new_v = [], []
    for layer in range(DEPTH):
        jl = layer // 2
        if layer % 2 == 0:
            gated = []
            row0 = 0
            for part in x_parts:
                ctx_rows = min(max(CTX_ROWS - row0, 0), part.shape[0])
                b, u = _conv_in(part, mod, norm1, w_conv_in, layer, row0)
                gated.append(_conv_gate(b, u, conv_w, layer, ctx_rows))
                row0 += part.shape[0]
            conv_out_tn = 1024 // len(x_parts)
            x = _mm_residual(gated, w_conv_out, x_parts, mod, layer, 2, 1024, conv_out_tn, "conv_out")
            a = _ffn_up(x, mod, norm2, w_ffn_gate, w_ffn_up, layer)
            x = _mm_residual([a], w_ffn_down, [x], mod, layer, 5, 1024, 512, "ffn_down")
            x_parts = [x]
        else:
            qkv_ctx = _qkv(x, mod, norm1, w_qkv, q_norm, k_norm, layer, 0, CTX_ROWS, f32)
            qkv_dec = _qkv(x, mod, norm1, w_qkv, q_norm, k_norm, layer, CTX_ROWS, DEC_ROWS, bf16)
            o_ctx, k_ctx, v_ctx = _ctx_attention(qkv_ctx)
            new_k.append(k_ctx.reshape(N_CTX_SEQ, CTX_LEN, N_HEADS, HEAD_DIM))
            new_v.append(v_ctx.reshape(N_CTX_SEQ, CTX_LEN, N_HEADS, HEAD_DIM))
            o_dec = _latent_attention(qkv_dec, cache_k2, cache_v2, jl, bias_blocks)
            x = _mm_residual([o_ctx, o_dec], w_attn_out, [x], mod, layer, 2, 1024, 1024, "attn_out")
            x_parts = _moe_layer(x, mod, norm2, w_router, w_moe_gate, w_moe_up, w_moe_down, layer,
                                 split_out=layer == DEPTH - 1)
            x = x_parts[0]

    y_prompt = x_parts[0].reshape(N_CTX_SEQ, CTX_LEN, D)
    y_sample = x_parts[1].reshape(N_DEC_SEQ, DEC_LEN, D)
    return y_prompt, y_sample, jnp.stack(new_k, axis=1), jnp.stack(new_v, axis=1)
```

```python
import functools

import numpy as np
import jax
import jax.numpy as jnp
from jax import lax
from jax.experimental import pallas as pl
from jax.experimental.pallas import tpu as pltpu

D = 2048
N_CTX_SEQ = 16
CTX_LEN = 256
N_DEC_SEQ = 4
DEC_LEN = 2048
CTX_ROWS = N_CTX_SEQ * CTX_LEN
DEC_ROWS = N_DEC_SEQ * DEC_LEN
T = CTX_ROWS + DEC_ROWS
DEPTH = 4
PAST_LEN = 512
N_HEADS = 16
HEAD_DIM = 128
GRID_W = 64
GRID_ROWS = DEC_LEN // GRID_W
WIN_ROWS = 8
WIN_COLS = 16
D_FF = 5632
N_EXPERTS = 8
TOP_K = 2
EPS = 1e-6
NEG_INF = -1e30
ATTN_SCALE = HEAD_DIM ** -0.5
LOG2_E = 1.4426950408889634
SCORE_SCALE_LOG2 = ATTN_SCALE * LOG2_E

N_GROUPS_PADDED = 8
LANES = 128
SUBLANES = 8
MXU_COLS = 256
VMEM_CAP_BYTES = 56 * 2 ** 20
VMEM_HEADROOM = 4 * 2 ** 20

Q_ROWS_PER_BLOCK = 4
Q_BLOCK = Q_ROWS_PER_BLOCK * GRID_W
N_Q_BLOCKS = GRID_ROWS // Q_ROWS_PER_BLOCK
BAND_ROWS = Q_ROWS_PER_BLOCK + WIN_ROWS
BAND = BAND_ROWS * GRID_W
MASKED_BLOCK = 2 * WIN_ROWS - 1
N_BIAS_BLOCKS = MASKED_BLOCK + 1

MOE_UP_TM = 1024
MOE_DOWN_TM = 512
N_ASSIGN = T * TOP_K
MOE_ROWS = N_ASSIGN + N_EXPERTS * MOE_UP_TM
TILE_EMPTY, TILE_FIRST_HALF, TILE_FULL = 0, 1, 2
ROW_SLABS = D // LANES

bf16 = jnp.bfloat16
f32 = jnp.float32


def _mm(a, b):
    return jnp.dot(a, b, preferred_element_type=f32)


def _mm_nt(a, b):
    return lax.dot_general(a, b, (((1,), (1,)), ((), ())), preferred_element_type=f32)


def _nbytes(shape, dtype):
    return int(np.prod(shape)) * jnp.dtype(dtype).itemsize


def _params(semantics, pipelined_bytes, resident_bytes):
    need = 2 * pipelined_bytes + resident_bytes + VMEM_HEADROOM
    return pltpu.CompilerParams(dimension_semantics=semantics,
                                vmem_limit_bytes=int(min(max(need, 16 * 2 ** 20), VMEM_CAP_BYTES)))


def _group_of_tile(i, tm):
    n_ctx = CTX_ROWS // tm
    per_seq = DEC_LEN // tm
    return jnp.where(i < n_ctx, 0, 1 + (i - n_ctx) // per_seq)


def _mod_spec(layer, which, tm, width=D, tile0=0):
    def index(*grid_and_prefetch):
        i = tile0 + grid_and_prefetch[0]
        j = grid_and_prefetch[1] if width != D else 0
        return ((layer * N_GROUPS_PADDED + _group_of_tile(i, tm)) * 6 + which, 0, j)
    return pl.BlockSpec((None, 1, width), index)


def _norm_mod_rows(x, gain, scale1p, shift):
    ms = jnp.mean(x * x, axis=-1, keepdims=True)
    y = x * lax.rsqrt(ms + EPS)
    return (y * gain) * scale1p + shift


def _norm_mod_to_scratch(x_ref, gain_ref, sc_ref, sh_ref, h_scr, rows=256):
    gain = gain_ref[...]
    scale1p = 1.0 + sc_ref[...]
    shift = sh_ref[...]

    def body(c, carry):
        r = pl.multiple_of(c * rows, rows)
        h = _norm_mod_rows(x_ref[pl.ds(r, rows), :], gain, scale1p, shift)
        h_scr[pl.ds(r, rows), :] = h.astype(bf16)
        return carry

    lax.fori_loop(0, x_ref.shape[0] // rows, body, 0)


def _ada_kernel(cond_ref, w_ref, b_ref, o_ref):
    c = cond_ref[...]
    s = (c * jax.nn.sigmoid(c)).astype(bf16)
    o_ref[...] = _mm(s, w_ref[...].astype(bf16)) + b_ref[...]


def _ada_params(cond, w_ada, b_ada):
    tn = 1024
    out = pl.pallas_call(
        _ada_kernel,
        grid=(DEPTH, 6 * D // tn),
        in_specs=[
            pl.BlockSpec((N_GROUPS_PADDED, D), lambda l, j: (0, 0)),
            pl.BlockSpec((None, D, tn), lambda l, j: (l, 0, j)),
            pl.BlockSpec((None, 1, tn), lambda l, j: (l, 0, j)),
        ],
        out_specs=pl.BlockSpec((None, N_GROUPS_PADDED, tn), lambda l, j: (l, 0, j)),
        out_shape=jax.ShapeDtypeStruct((DEPTH, N_GROUPS_PADDED, 6 * D), f32),
        compiler_params=_params(("arbitrary", "arbitrary"),
                                _nbytes((D, tn), f32) + _nbytes((8, tn), f32) * 2,
                                _nbytes((D, tn), bf16) + _nbytes((8, D), f32) * 2),
        name="ada_params",
    )(cond, w_ada, b_ada.reshape(DEPTH, 1, 6 * D))
    return out.reshape(DEPTH * N_GROUPS_PADDED * 6, 1, D)


def _conv_in_kernel(x_ref, gain_ref, sc_ref, sh_ref, wb_ref, wc_ref, wv_ref, b_ref, u_ref, h_scr):
    @pl.when(pl.program_id(1) == 0)
    def _():
        _norm_mod_to_scratch(x_ref, gain_ref, sc_ref, sh_ref, h_scr)

    h = h_scr[...]
    b_ref[...] = _mm(h, wb_ref[...])
    u_ref[...] = _mm(h, wc_ref[...]) * _mm(h, wv_ref[...])


def _conv_in(x, mod, norm, w_conv_in, layer, row0=0):
    tm, tn = 1024, 512
    jl = layer // 2
    nb = D // tn
    n_rows = x.shape[0]
    tile0 = row0 // tm
    return pl.pallas_call(
        _conv_in_kernel,
        grid=(n_rows // tm, nb),
        in_specs=[
            pl.BlockSpec((tm, D), lambda i, j: (i, 0)),
            pl.BlockSpec((None, 1, D), lambda i, j: (layer, 0, 0)),
            _mod_spec(layer, 1, tm, tile0=tile0), _mod_spec(layer, 0, tm, tile0=tile0),
            pl.BlockSpec((None, D, tn), lambda i, j: (jl, 0, j)),
            pl.BlockSpec((None, D, tn), lambda i, j: (jl, 0, nb + j)),
            pl.BlockSpec((None, D, tn), lambda i, j: (jl, 0, 2 * nb + j)),
        ],
        out_specs=[pl.BlockSpec((tm, tn), lambda i, j: (i, j))] * 2,
        out_shape=[jax.ShapeDtypeStruct((n_rows, D), f32)] * 2,
        scratch_shapes=[pltpu.VMEM((tm, D), bf16)],
        compiler_params=_params(("arbitrary", "arbitrary"),
                                _nbytes((tm, D), f32) + 3 * _nbytes((D, tn), bf16) + 2 * _nbytes((tm, tn), f32),
                                _nbytes((tm, D), bf16) + 3 * _nbytes((tm, tn), f32)),
        name="conv_in",
    )(x, norm.reshape(DEPTH, 1, D), mod, mod, w_conv_in, w_conv_in, w_conv_in)


def _ffn_up_kernel(x_ref, gain_ref, sc_ref, sh_ref, wg_ref, wu_ref, a_ref, h_scr):
    @pl.when(pl.program_id(1) == 0)
    def _():
        _norm_mod_to_scratch(x_ref, gain_ref, sc_ref, sh_ref, h_scr)

    h = h_scr[...]
    g = _mm(h, wg_ref[...])
    u = _mm(h, wu_ref[...])
    a_ref[...] = ((g * jax.nn.sigmoid(g)) * u).astype(bf16)


def _ffn_up(x, mod, norm, w_gate, w_up, layer):
    tm, tn = 1024, 512
    jl = layer // 2
    return pl.pallas_call(
        _ffn_up_kernel,
        grid=(T // tm, D_FF // tn),
        in_specs=[
            pl.BlockSpec((tm, D), lambda i, j: (i, 0)),
            pl.BlockSpec((None, 1, D), lambda i, j: (layer, 0, 0)),
            _mod_spec(layer, 4, tm), _mod_spec(layer, 3, tm),
            pl.BlockSpec((None, D, tn), lambda i, j: (jl, 0, j)),
            pl.BlockSpec((None, D, tn), lambda i, j: (jl, 0, j)),
        ],
        out_specs=pl.BlockSpec((tm, tn), lambda i, j: (i, j)),
        out_shape=jax.ShapeDtypeStruct((T, D_FF), bf16),
        scratch_shapes=[pltpu.VMEM((tm, D), bf16)],
        compiler_params=_params(("arbitrary", "arbitrary"),
                                _nbytes((tm, D), f32) + 2 * _nbytes((D, tn), bf16) + _nbytes((tm, tn), bf16),
                                _nbytes((tm, D), bf16) + 3 * _nbytes((tm, tn), f32)),
        name="ffn_up",
    )(x, norm.reshape(DEPTH, 1, D), mod, mod, w_gate, w_up)


def _qkv_kernel(x_ref, gain_ref, sc_ref, sh_ref, w_ref, qg_ref, kg_ref, o_ref, h_scr, *, n_q_blocks):
    j = pl.program_id(1)

    @pl.when(j == 0)
    def _():
        _norm_mod_to_scratch(x_ref, gain_ref, sc_ref, sh_ref, h_scr)

    is_qk = j < 2 * n_q_blocks
    head_gain = jnp.where(j < n_q_blocks, qg_ref[...], kg_ref[...])
    h = h_scr[...]
    for c in range(0, o_ref.shape[1], MXU_COLS):
        acc = _mm(h, w_ref[:, c:c + MXU_COLS])
        for hh in range(MXU_COLS // HEAD_DIM):
            a = acc[:, hh * HEAD_DIM:(hh + 1) * HEAD_DIM]
            ms = jnp.mean(a * a, axis=-1, keepdims=True)
            normed = (a * lax.rsqrt(ms + EPS)) * head_gain
            cols = slice(c + hh * HEAD_DIM, c + (hh + 1) * HEAD_DIM)
            o_ref[:, cols] = jnp.where(is_qk, normed, a).astype(o_ref.dtype)


def _qkv(x, mod, norm, w_qkv, q_norm, k_norm, layer, row0, n_rows, out_dtype):
    tm, tn = 1024, 1024
    jl = layer // 2
    tile0 = row0 // tm
    return pl.pallas_call(
        functools.partial(_qkv_kernel, n_q_blocks=D // tn),
        grid=(n_rows // tm, 3 * D // tn),
        in_specs=[
            pl.BlockSpec((tm, D), lambda i, j: (tile0 + i, 0)),
            pl.BlockSpec((None, 1, D), lambda i, j: (layer, 0, 0)),
            _mod_spec(layer, 1, tm, tile0=tile0), _mod_spec(layer, 0, tm, tile0=tile0),
            pl.BlockSpec((None, D, tn), lambda i, j: (jl, 0, j)),
            pl.BlockSpec((None, 1, HEAD_DIM), lambda i, j: (jl, 0, 0)),
            pl.BlockSpec((None, 1, HEAD_DIM), lambda i, j: (jl, 0, 0)),
        ],
        out_specs=pl.BlockSpec((tm, tn), lambda i, j: (i, j)),
        out_shape=jax.ShapeDtypeStruct((n_rows, 3 * D), out_dtype),
        scratch_shapes=[pltpu.VMEM((tm, D), bf16)],
        compiler_params=_params(("arbitrary", "arbitrary"),
                                _nbytes((tm, D), f32) + _nbytes((D, tn), bf16) + _nbytes((tm, tn), out_dtype),
                                _nbytes((tm, D), bf16) + 2 * _nbytes((tm, tn), f32)),
        name="qkv",
    )(x, norm.reshape(DEPTH, 1, D), mod, mod, w_qkv,
      q_norm.reshape(-1, 1, HEAD_DIM), k_norm.reshape(-1, 1, HEAD_DIM))


def _mm_res_kernel(*refs, n_a, n_ctx_tiles):
    a_refs, (w_ref, *x_refs, gate_ref, o_ref) = refs[:n_a], refs[n_a:]

    def update(a_ref, x_ref):
        o_ref[...] = x_ref[...] + gate_ref[...] * _mm(a_ref[...], w_ref[...])

    if len(a_refs) == 1 and len(x_refs) == 1:
        update(a_refs[0], x_refs[0])
    else:
        i = pl.program_id(0)

        @pl.when(i < n_ctx_tiles)
        def _():
            update(a_refs[0], x_refs[0])

        @pl.when(i >= n_ctx_tiles)
        def _():
            update(a_refs[-1], x_refs[-1])


def _row_part_specs(parts, tm, width, n_ctx_tiles):
    col = (lambda j: 0) if width == parts[0].shape[1] else (lambda j: j)
    if len(parts) == 1:
        return [pl.BlockSpec((tm, width), lambda i, j: (i, col(j)))]
    return [pl.BlockSpec((tm, width), lambda i, j: (jnp.minimum(i, n_ctx_tiles - 1), col(j))),
            pl.BlockSpec((tm, width), lambda i, j: (jnp.maximum(i - n_ctx_tiles, 0), col(j)))]


def _mm_residual(a_parts, w, x_parts, mod, layer, which, tm, tn, name):
    k = a_parts[0].shape[1]
    jl = layer // 2
    n_ctx_tiles = CTX_ROWS // tm
    return pl.pallas_call(
        functools.partial(_mm_res_kernel, n_a=len(a_parts), n_ctx_tiles=n_ctx_tiles),
        grid=(T // tm, D // tn),
        in_specs=_row_part_specs(a_parts, tm, k, n_ctx_tiles) + [
            pl.BlockSpec((None, k, tn), lambda i, j: (jl, 0, j)),
        ] + _row_part_specs(x_parts, tm, tn, n_ctx_tiles) + [
            _mod_spec(layer, which, tm, tn),
        ],
        out_specs=pl.BlockSpec((tm, tn), lambda i, j: (i, j)),
        out_shape=jax.ShapeDtypeStruct((T, D), f32),
        compiler_params=_params(("arbitrary", "arbitrary"),
                                len(a_parts) * _nbytes((tm, k), bf16) + _nbytes((k, tn), bf16)
                                + (1 + len(x_parts)) * _nbytes((tm, tn), f32),
                                2 * _nbytes((tm, tn), f32)),
        name=name,
    )(*a_parts, w, *x_parts, mod)


def _conv_gate_kernel(b_ref, u_ref, up_ref, un_ref, cw_ref, o_ref, *, tm, cols, ctx_rows):
    row0 = pl.program_id(0) * tm
    seq_len = jnp.where(row0 < ctx_rows, CTX_LEN, DEC_LEN)
    local = lax.broadcasted_iota(jnp.int32, (tm, cols), 0)
    pos = (row0 + local) & (seq_len - 1)
    is_first = pos == 0
    is_last = pos == seq_len - 1
    for c in range(D // cols):
        sl = slice(c * cols, (c + 1) * cols)
        u = u_ref[:, sl]
        prev_row = up_ref[SUBLANES - 1:SUBLANES, sl]
        next_row = un_ref[0:1, sl]
        u_prev = jnp.where(local == 0, prev_row, pltpu.roll(u, 1, axis=0))
        u_next = jnp.where(local == tm - 1, next_row, pltpu.roll(u, tm - 1, axis=0))
        u_prev = jnp.where(is_first, 0.0, u_prev)
        u_next = jnp.where(is_last, 0.0, u_next)
        conv = u_prev * cw_ref[0:1, sl] + u * cw_ref[1:2, sl] + u_next * cw_ref[2:3, sl]
        o_ref[:, sl] = (b_ref[:, sl] * conv).astype(bf16)


def _conv_gate(b, u, conv_w, layer, ctx_rows):
    tm, cols = 512, 512
    jl = layer // 2
    blocks_per_tile = tm // SUBLANES
    n_rows = b.shape[0]
    return pl.pallas_call(
        functools.partial(_conv_gate_kernel, tm=tm, cols=cols, ctx_rows=ctx_rows),
        grid=(n_rows // tm,),
        in_specs=[
            pl.BlockSpec((tm, D), lambda i: (i, 0)),
            pl.BlockSpec((tm, D), lambda i: (i, 0)),
            pl.BlockSpec((SUBLANES, D), lambda i: (jnp.maximum(i * blocks_per_tile - 1, 0), 0)),
            pl.BlockSpec((SUBLANES, D),
                         lambda i: (jnp.minimum((i + 1) * blocks_per_tile, n_rows // SUBLANES - 1), 0)),
            pl.BlockSpec((None, 3, D), lambda i: (jl, 0, 0)),
        ],
        out_specs=pl.BlockSpec((tm, D), lambda i: (i, 0)),
        out_shape=jax.ShapeDtypeStruct((n_rows, D), bf16),
        compiler_params=_params(("arbitrary",),
                                2 * _nbytes((tm, D), f32) + _nbytes((tm, D), bf16) + 2 * _nbytes((SUBLANES, D), f32),
                                8 * _nbytes((tm, cols), f32)),
        name="conv_gate",
    )(b, u, u, u, conv_w)


def _softmax_pv(parts):
    m = parts[0][0].max(axis=-1, keepdims=True)
    for s, _ in parts[1:]:
        m = jnp.maximum(m, s.max(axis=-1, keepdims=True))
    acc = None
    for s, v in parts:
        p = jnp.exp2(s - m).astype(bf16)
        v_and_ones = jnp.concatenate([v, jnp.ones_like(v)], axis=1)
        o = _mm(p, v_and_ones)
        acc = o if acc is None else acc + o
    head_dim = acc.shape[1] // 2
    return acc[:, :head_dim] * (1.0 / acc[:, head_dim:head_dim + 1])


def _ctx_attn_kernel(q_ref, k_ref, v_ref, o_ref, k_out_ref, v_out_ref):
    k_out_ref[...] = k_ref[...]
    v_out_ref[...] = v_ref[...]
    for hh in range(q_ref.shape[1] // HEAD_DIM):
        sl = slice(hh * HEAD_DIM, (hh + 1) * HEAD_DIM)
        q = q_ref[:, sl].astype(bf16)
        k = k_ref[:, sl].astype(bf16)
        v = v_ref[:, sl].astype(bf16)
        s = _mm_nt(q, k) * SCORE_SCALE_LOG2
        o_ref[:, sl] = _softmax_pv([(s, v)]).astype(bf16)


def _ctx_attention(qkv):
    tn = 512
    nb = D // tn
    return pl.pallas_call(
        _ctx_attn_kernel,
        grid=(N_CTX_SEQ, nb),
        in_specs=[
            pl.BlockSpec((CTX_LEN, tn), lambda b, g: (b, g)),
            pl.BlockSpec((CTX_LEN, tn), lambda b, g: (b, nb + g)),
            pl.BlockSpec((CTX_LEN, tn), lambda b, g: (b, 2 * nb + g)),
        ],
        out_specs=[pl.BlockSpec((CTX_LEN, tn), lambda b, g: (b, g))] * 3,
        out_shape=[jax.ShapeDtypeStruct((CTX_ROWS, D), bf16),
                   jax.ShapeDtypeStruct((CTX_ROWS, D), f32),
                   jax.ShapeDtypeStruct((CTX_ROWS, D), f32)],
        compiler_params=_params(("arbitrary", "arbitrary"),
                                5 * _nbytes((CTX_LEN, tn), f32) + _nbytes((CTX_LEN, tn), bf16),
                                8 * _nbytes((CTX_LEN, CTX_LEN), f32)),
        name="ctx_attention",
    )(qkv, qkv, qkv)


def _band_start_row(m):
    return jnp.clip(Q_ROWS_PER_BLOCK * m - WIN_ROWS // 2, 0, GRID_ROWS - BAND_ROWS)


def _band_block_ids(m):
    band0 = _band_start_row(m)
    ids = []
    for rho in range(Q_ROWS_PER_BLOCK):
        r = Q_ROWS_PER_BLOCK * m + rho
        win0 = jnp.clip(r - WIN_ROWS // 2, 0, GRID_ROWS - WIN_ROWS)
        row_ids = []
        for kap in range(BAND_ROWS):
            kr = band0 + kap
            inside = (kr >= win0) & (kr < win0 + WIN_ROWS)
            row_ids.append(jnp.where(inside, kr - r + WIN_ROWS - 1, MASKED_BLOCK))
        ids.append(row_ids)
    return ids


def _band_bias(bias_ref, hh, ids):
    left_half = lax.broadcasted_iota(jnp.int32, (GRID_W, 2 * GRID_W), 1) < GRID_W
    rows = []
    for rho in range(Q_ROWS_PER_BLOCK):
        pairs = [jnp.where(left_half, bias_ref[hh, ids[rho][kap]], bias_ref[hh, ids[rho][kap + 1]])
                 for kap in range(0, BAND_ROWS, 2)]
        rows.append(jnp.concatenate(pairs, axis=1))
    return jnp.concatenate(rows, axis=0)


def _loc_attn_kernel(q_ref, k_ref, v_ref, ck_ref, cv_ref, bias_ref, o_ref):
    start = pl.multiple_of(_band_start_row(pl.program_id(2)) * GRID_W, GRID_W)
    ids = _band_block_ids(pl.program_id(2))
    for hh in range(q_ref.shape[1] // HEAD_DIM):
        sl = slice(hh * HEAD_DIM, (hh + 1) * HEAD_DIM)
        q = q_ref[:, sl]
        k_band = k_ref[pl.ds(start, BAND), sl]
        v_band = v_ref[pl.ds(start, BAND), sl]
        k_ctx = ck_ref[:, sl].astype(bf16)
        v_ctx = cv_ref[:, sl].astype(bf16)
        s_loc = _mm_nt(q, k_band) * SCORE_SCALE_LOG2 + _band_bias(bias_ref, hh, ids)
        s_ctx = _mm_nt(q, k_ctx) * SCORE_SCALE_LOG2
        o_ref[:, sl] = _softmax_pv([(s_loc, v_band), (s_ctx, v_ctx)]).astype(bf16)


def _rel_bias_blocks(rpb_all):
    cols = np.arange(GRID_W)
    col_start = np.clip(cols - WIN_COLS // 2, 0, GRID_W - WIN_COLS)
    col_ok = (cols[None, :] >= col_start[:, None]) & (cols[None, :] < col_start[:, None] + WIN_COLS)
    pad = GRID_W - WIN_COLS
    n = rpb_all.shape[0] * rpb_all.shape[1]
    padded = jnp.pad(rpb_all.astype(f32).reshape(n, MASKED_BLOCK, -1), ((0, 0), (0, 0), (pad, pad)))
    skewed = jnp.tile(padded, (1, 1, GRID_W + 1))[:, :, :GRID_W * 2 * GRID_W]
    skewed = skewed.reshape(n, MASKED_BLOCK, GRID_W, 2 * GRID_W)[..., :GRID_W]
    toeplitz = jnp.flip(skewed, axis=2)
    toeplitz = jnp.where(col_ok[None, None], toeplitz * LOG2_E, NEG_INF)
    masked = jnp.full((n, 1, GRID_W, GRID_W), NEG_INF, f32)
    blocks = jnp.concatenate([toeplitz, masked], axis=1)
    return jnp.concatenate([blocks, blocks], axis=3)


def _latent_attention(qkv, cache_k, cache_v, layer_slot, bias_blocks):
    tn = 512
    nb = D // tn
    hb = tn // HEAD_DIM
    return pl.pallas_call(
        _loc_attn_kernel,
        grid=(N_DEC_SEQ, nb, N_Q_BLOCKS),
        in_specs=[
            pl.BlockSpec((Q_BLOCK, tn), lambda b, g, m: (b * N_Q_BLOCKS + m, g)),
            pl.BlockSpec((DEC_LEN, tn), lambda b, g, m: (b, nb + g)),
            pl.BlockSpec((DEC_LEN, tn), lambda b, g, m: (b, 2 * nb + g)),
            pl.BlockSpec((None, None, PAST_LEN, tn), lambda b, g, m: (b, layer_slot, 0, g)),
            pl.BlockSpec((None, None, PAST_LEN, tn), lambda b, g, m: (b, layer_slot, 0, g)),
            pl.BlockSpec((hb, N_BIAS_BLOCKS, GRID_W, 2 * GRID_W), lambda b, g, m: (layer_slot * nb + g, 0, 0, 0)),
        ],
        out_specs=pl.BlockSpec((Q_BLOCK, tn), lambda b, g, m: (b * N_Q_BLOCKS + m, g)),
        out_shape=jax.ShapeDtypeStruct((DEC_ROWS, D), bf16),
        compiler_params=_params(("arbitrary", "arbitrary", "arbitrary"),
                                _nbytes((Q_BLOCK, tn), bf16) + 2 * _nbytes((DEC_LEN, tn), bf16)
                                + 2 * _nbytes((PAST_LEN, tn), f32)
                                + _nbytes((hb, N_BIAS_BLOCKS, GRID_W, 2 * GRID_W), f32)
                                + _nbytes((Q_BLOCK, tn), bf16),
                                10 * _nbytes((Q_BLOCK, BAND), f32)),
        name="latent_attention",
    )(qkv, qkv, qkv, cache_k, cache_v, bias_blocks)


def _router_kernel(x_ref, gain_ref, sc_ref, sh_ref, wr_ref, h_ref, idx_ref, wt_ref, rank_ref, cnt_ref, seen):
    tm = x_ref.shape[0]

    @pl.when(pl.program_id(0) == 0)
    def _():
        seen[...] = jnp.zeros(seen.shape, seen.dtype)

    h = _norm_mod_rows(x_ref[...], gain_ref[...], 1.0 + sc_ref[...], sh_ref[...])
    h_ref[...] = h
    w = wr_ref[...]
    h_hi, w_hi = h.astype(bf16), w.astype(bf16)
    h_lo = (h - h_hi.astype(f32)).astype(bf16)
    w_lo = (w - w_hi.astype(f32)).astype(bf16)
    logits = _mm(h_hi, w_hi) + (_mm(h_lo, w_hi) + _mm(h_hi, w_lo))
    lane = lax.broadcasted_iota(jnp.int32, (tm, LANES), 1)
    l1 = jnp.where(lane < N_EXPERTS, logits, -jnp.inf)
    m1 = l1.max(axis=-1, keepdims=True)
    i1 = jnp.where(l1 == m1, lane, LANES).min(axis=-1, keepdims=True)
    l2 = jnp.where(lane == i1, -jnp.inf, l1)
    m2 = l2.max(axis=-1, keepdims=True)
    i2 = jnp.where(l2 == m2, lane, LANES).min(axis=-1, keepdims=True)
    e = jnp.exp(m2 - m1)
    w1 = 1.0 / (1.0 + e)
    w2 = e / (1.0 + e)
    idx_ref[...] = jnp.where(lane == 0, i1, jnp.where(lane == 1, i2, 0))
    wt_ref[...] = jnp.where(lane == 0, w1, jnp.where(lane == 1, w2, 0.0))

    first = lane == i1
    second = lane == i2
    chosen = (first | second).astype(bf16)
    earlier = (lax.broadcasted_iota(jnp.int32, (tm, tm), 1) < lax.broadcasted_iota(jnp.int32, (tm, tm), 0))
    before = _mm(earlier.astype(bf16), chosen) + seen[...]
    r1 = jnp.where(first, before, 0.0).sum(axis=-1, keepdims=True)
    r2 = jnp.where(second, before, 0.0).sum(axis=-1, keepdims=True)
    rank_ref[...] = jnp.where(lane == 0, r1, jnp.where(lane == 1, r2, 0.0)).astype(jnp.int32)
    seen[...] += chosen.astype(f32).sum(axis=0, keepdims=True)
    cnt_ref[...] = seen[...].astype(jnp.int32)


def _router(x, mod, norm, w_router_padded, layer):
    tm = 512
    jl = layer // 2
    lane_block = pl.BlockSpec((tm, LANES), lambda i: (i, 0))
    return pl.pallas_call(
        _router_kernel,
        grid=(T // tm,),
        in_specs=[
            pl.BlockSpec((tm, D), lambda i: (i, 0)),
            pl.BlockSpec((None, 1, D), lambda i: (layer, 0, 0)),
            _mod_spec(layer, 4, tm), _mod_spec(layer, 3, tm),
            pl.BlockSpec((None, D, LANES), lambda i: (jl, 0, 0)),
        ],
        out_specs=[pl.BlockSpec((tm, D), lambda i: (i, 0)), lane_block, lane_block, lane_block,
                   pl.BlockSpec((1, LANES), lambda i: (0, 0))],
        out_shape=[jax.ShapeDtypeStruct((T, D), f32),
                   jax.ShapeDtypeStruct((T, LANES), jnp.int32),
                   jax.ShapeDtypeStruct((T, LANES), f32),
                   jax.ShapeDtypeStruct((T, LANES), jnp.int32),
                   jax.ShapeDtypeStruct((1, LANES), jnp.int32)],
        scratch_shapes=[pltpu.VMEM((1, LANES), f32)],
        compiler_params=_params(("arbitrary",),
                                2 * _nbytes((tm, D), f32) + 3 * _nbytes((tm, LANES), f32),
                                _nbytes((D, LANES), f32) + 4 * _nbytes((tm, D), f32) + 2 * _nbytes((tm, tm), f32)),
        name="router",
    )(x, norm.reshape(DEPTH, 1, D), mod, mod, w_router_padded)


def _tile_table(tm, group_start, group_end, counts):
    experts = jnp.arange(N_EXPERTS, dtype=jnp.int32)
    tile_row0 = jnp.arange(MOE_ROWS // tm, dtype=jnp.int32) * tm
    tile_expert = jnp.sum((tile_row0[:, None] >= group_end[None, :]).astype(jnp.int32), axis=1)
    tile_expert = jnp.minimum(tile_expert, N_EXPERTS - 1)
    rows_end = jnp.sum(jnp.where(tile_expert[:, None] == experts[None, :], (group_start + counts)[None, :], 0), axis=1)
    rows_here = rows_end - tile_row0
    fill = jnp.where(rows_here <= 0, TILE_EMPTY, jnp.where(rows_here <= tm // 2, TILE_FIRST_HALF, TILE_FULL))
    tile_ids = jnp.arange(MOE_ROWS // tm, dtype=jnp.int32)
    input_block = lax.cummax(jnp.where(fill != TILE_EMPTY, tile_ids, 0), axis=0)
    return tile_expert.astype(jnp.int32), fill.astype(jnp.int32), input_block.astype(jnp.int32)


def _routing_plan(expert_ids, ranks, counts):
    e_flat = expert_ids.reshape(-1)
    span = ((counts + MOE_UP_TM - 1) // MOE_UP_TM) * MOE_UP_TM
    group_end = jnp.cumsum(span)
    group_start = group_end - span
    onehot = e_flat[:, None] == jnp.arange(N_EXPERTS, dtype=jnp.int32)[None, :]
    dest = ranks.reshape(-1) + jnp.sum(jnp.where(onehot, group_start[None, :], 0), axis=1)
    src_token = jnp.zeros((MOE_ROWS,), jnp.int32).at[dest].set(
        jnp.arange(N_ASSIGN, dtype=jnp.int32) // TOP_K, unique_indices=True)
    up_tiles = _tile_table(MOE_UP_TM, group_start, group_end, counts)
    down_tiles = _tile_table(MOE_DOWN_TM, group_start, group_end, counts)
    return src_token, dest.astype(jnp.int32), up_tiles, down_tiles


def _dispatch_kernel(src_ref, fill_ref, h_hbm, o_ref, buf, sem):
    i = pl.program_id(0)
    n_steps = pl.num_programs(0)
    chunk = o_ref.shape[0]

    def tile_copies(tile, slot):
        return [pltpu.make_async_copy(h_hbm.at[pl.ds(src_ref[tile * chunk + r], 1), :],
                                      buf.at[slot, pl.ds(r, 1), :], sem.at[slot]) for r in range(chunk)]

    def start_tile(tile, slot):
        for k, cp in enumerate(tile_copies(tile, slot)):
            cp.start(priority=k % 2)

    @pl.when((i == 0) & (fill_ref[0] != TILE_EMPTY))
    def _():
        start_tile(0, 0)

    @pl.when((i + 1 < n_steps) & (fill_ref[jnp.minimum(i + 1, n_steps - 1)] != TILE_EMPTY))
    def _():
        start_tile(i + 1, (i + 1) % 2)

    @pl.when(fill_ref[i] != TILE_EMPTY)
    def _():
        slot = i % 2
        for cp in tile_copies(i, slot):
            cp.wait()
        o_ref[...] = buf[slot].astype(bf16)

    @pl.when(fill_ref[i] == TILE_EMPTY)
    def _():
        o_ref[...] = jnp.zeros(o_ref.shape, o_ref.dtype)


def _dispatch(h, src_token, tile_fill):
    chunk = MOE_DOWN_TM
    return pl.pallas_call(
        _dispatch_kernel,
        grid_spec=pltpu.PrefetchScalarGridSpec(
            num_scalar_prefetch=2,
            grid=(MOE_ROWS // chunk,),
            in_specs=[pl.BlockSpec(memory_space=pl.ANY)],
            out_specs=pl.BlockSpec((chunk, D), lambda i, src, fill: (i, 0)),
            scratch_shapes=[pltpu.VMEM((2, chunk, D), f32), pltpu.SemaphoreType.DMA((2,))],
        ),
        out_shape=jax.ShapeDtypeStruct((MOE_ROWS, D), bf16),
        compiler_params=_params(("arbitrary",), _nbytes((chunk, D), bf16), 3 * _nbytes((chunk, D), f32)),
        name="moe_dispatch",
    )(src_token, tile_fill, h)


def _expert_changed(te_ref, i):
    return (i == 0) | (te_ref[i] != te_ref[jnp.maximum(i - 1, 0)])


def _for_tile_rows(i, fill_ref, o_ref, rows_fn):
    tm = o_ref.shape[0]
    fill = fill_ref[i]

    @pl.when(fill == TILE_FULL)
    def _():
        o_ref[...] = rows_fn(slice(None))

    @pl.when(fill == TILE_FIRST_HALF)
    def _():
        o_ref[:tm // 2] = rows_fn(slice(0, tm // 2))
        o_ref[tm // 2:] = jnp.zeros((tm // 2, o_ref.shape[1]), o_ref.dtype)

    @pl.when(fill == TILE_EMPTY)
    def _():
        o_ref[...] = jnp.zeros(o_ref.shape, o_ref.dtype)


def _moe_up_kernel(te_ref, fill_ref, src_ref, hs_ref, wg_ref, wu_ref, a_ref, wg_s, wu_s):
    i = pl.program_id(1)

    @pl.when(_expert_changed(te_ref, i))
    def _():
        wg_s[...] = wg_ref[...].astype(bf16)
        wu_s[...] = wu_ref[...].astype(bf16)

    def gated(rows):
        h = hs_ref[rows]
        g = _mm(h, wg_s[...])
        u = _mm(h, wu_s[...])
        return ((g * jax.nn.sigmoid(g)) * u).astype(bf16)

    _for_tile_rows(i, fill_ref, a_ref, gated)


def _moe_up(hs, w_gate, w_up, tile_expert, tile_fill, input_block, layer):
    tm, tn = MOE_UP_TM, 512
    jl = layer // 2
    return pl.pallas_call(
        _moe_up_kernel,
        grid_spec=pltpu.PrefetchScalarGridSpec(
            num_scalar_prefetch=3,
            grid=(D_FF // tn, MOE_ROWS // tm),
            in_specs=[
                pl.BlockSpec((tm, D), lambda j, i, te, fill, src: (src[i], 0)),
                pl.BlockSpec((None, None, D, tn), lambda j, i, te, fill, src: (jl, te[i], 0, j)),
                pl.BlockSpec((None, None, D, tn), lambda j, i, te, fill, src: (jl, te[i], 0, j)),
            ],
            out_specs=pl.BlockSpec((tm, tn), lambda j, i, te, fill, src: (i, j)),
            scratch_shapes=[pltpu.VMEM((D, tn), bf16), pltpu.VMEM((D, tn), bf16)],
        ),
        out_shape=jax.ShapeDtypeStruct((MOE_ROWS, D_FF), bf16),
        compiler_params=_params(("arbitrary", "arbitrary"),
                                _nbytes((tm, D), bf16) + 2 * _nbytes((D, tn), f32) + _nbytes((tm, tn), bf16),
                                2 * _nbytes((D, tn), bf16) + 3 * _nbytes((tm, tn), f32)),
        name="moe_up",
    )(tile_expert, tile_fill, input_block, hs, w_gate, w_up)


def _moe_down_kernel(te_ref, fill_ref, src_ref, a_ref, wd_ref, y_ref, wd_s):
    i = pl.program_id(1)

    @pl.when(_expert_changed(te_ref, i))
    def _():
        wd_s[...] = wd_ref[...].astype(bf16)

    _for_tile_rows(i, fill_ref, y_ref, lambda rows: _mm(a_ref[rows], wd_s[...]))


def _moe_down(a, w_down, tile_expert, tile_fill, input_block, layer):
    tm, tn = MOE_DOWN_TM, 512
    jl = layer // 2
    return pl.pallas_call(
        _moe_down_kernel,
        grid_spec=pltpu.PrefetchScalarGridSpec(
            num_scalar_prefetch=3,
            grid=(D // tn, MOE_ROWS // tm),
            in_specs=[
                pl.BlockSpec((tm, D_FF), lambda j, i, te, fill, src: (src[i], 0)),
                pl.BlockSpec((None, None, D_FF, tn), lambda j, i, te, fill, src: (jl, te[i], 0, j)),
            ],
            out_specs=pl.BlockSpec((tm, tn), lambda j, i, te, fill, src: (i, j)),
            scratch_shapes=[pltpu.VMEM((D_FF, tn), bf16)],
        ),
        out_shape=jax.ShapeDtypeStruct((MOE_ROWS, D), f32),
        compiler_params=_params(("arbitrary", "arbitrary"),
                                _nbytes((tm, D_FF), bf16) + _nbytes((D_FF, tn), f32) + _nbytes((tm, tn), f32),
                                _nbytes((D_FF, tn), bf16) + _nbytes((tm, tn), f32)),
        name="moe_down",
    )(tile_expert, tile_fill, input_block, a, w_down)


def _combine_kernel(pos_ref, x_ref, gate_ref, wt_ref, y_hbm, *rest, tm, n_out):
    o_refs, (ybuf, sem) = rest[:n_out], rest[n_out:]
    i = pl.program_id(0)
    n_steps = pl.num_programs(0)

    def tile_copies(tile, slot):
        return [pltpu.make_async_copy(y_hbm.at[pl.ds(pos_ref[(tile * tm + t) * TOP_K + c], 1), :],
                                      ybuf.at[slot, c, pl.ds(t, 1), :], sem.at[slot])
                for t in range(tm) for c in range(TOP_K)]

    def start_tile(tile, slot):
        for k, cp in enumerate(tile_copies(tile, slot)):
            cp.start(priority=k % 2)

    @pl.when(i == 0)
    def _():
        start_tile(0, 0)

    @pl.when(i + 1 < n_steps)
    def _():
        start_tile(i + 1, (i + 1) % 2)

    slot = i % 2
    for cp in tile_copies(i, slot):
        cp.wait()
    w1 = jnp.broadcast_to(wt_ref[:, 0:1], (tm, LANES))
    w2 = jnp.broadcast_to(wt_ref[:, 1:2], (tm, LANES))

    def write(o_ref):
        for s in range(ROW_SLABS):
            sl = slice(s * LANES, (s + 1) * LANES)
            mix = w1 * ybuf[slot, 0, :, sl] + w2 * ybuf[slot, 1, :, sl]
            o_ref[:, sl] = x_ref[:, sl] + gate_ref[:, sl] * mix

    if len(o_refs) == 1:
        write(o_refs[0])
    else:
        n_ctx_tiles = CTX_ROWS // tm

        @pl.when(i < n_ctx_tiles)
        def _():
            write(o_refs[0])

        @pl.when(i >= n_ctx_tiles)
        def _():
            write(o_refs[1])


def _combine(x, y_slabs, pos, wts, mod, layer, split_out):
    tm = 256
    n_ctx_tiles = CTX_ROWS // tm
    if split_out:
        out_specs = [pl.BlockSpec((tm, D), lambda i, pos: (jnp.minimum(i, n_ctx_tiles - 1), 0)),
                     pl.BlockSpec((tm, D), lambda i, pos: (jnp.maximum(i - n_ctx_tiles, 0), 0))]
        out_shape = [jax.ShapeDtypeStruct((CTX_ROWS, D), f32), jax.ShapeDtypeStruct((DEC_ROWS, D), f32)]
    else:
        out_specs = [pl.BlockSpec((tm, D), lambda i, pos: (i, 0))]
        out_shape = [jax.ShapeDtypeStruct((T, D), f32)]
    return pl.pallas_call(
        functools.partial(_combine_kernel, tm=tm, n_out=len(out_specs)),
        grid_spec=pltpu.PrefetchScalarGridSpec(
            num_scalar_prefetch=1,
            grid=(T // tm,),
            in_specs=[
                pl.BlockSpec((tm, D), lambda i, pos: (i, 0)),
                _mod_spec(layer, 5, tm),
                pl.BlockSpec((tm, LANES), lambda i, pos: (i, 0)),
                pl.BlockSpec(memory_space=pl.ANY),
            ],
            out_specs=out_specs,
            scratch_shapes=[pltpu.VMEM((2, TOP_K, tm, D), f32),
                            pltpu.SemaphoreType.DMA((2,))],
        ),
        out_shape=out_shape,
        compiler_params=_params(("arbitrary",),
                                (1 + len(out_specs)) * _nbytes((tm, D), f32) + _nbytes((tm, LANES), f32),
                                2 * _nbytes((TOP_K, tm, D), f32) + 4 * _nbytes((tm, LANES), f32)),
        name="moe_combine",
    )(pos, x, mod, wts, y_slabs)


def _moe_layer(x, mod, norm2, w_router, w_gate, w_up, w_down, layer, split_out):
    wr = jnp.pad(w_router, ((0, 0), (0, 0), (0, LANES - N_EXPERTS)))
    h, idx, wts, ranks, counts = _router(x, mod, norm2, wr, layer)
    src_token, pos, up_tiles, down_tiles = _routing_plan(idx[:, :TOP_K], ranks[:, :TOP_K], counts[0, :N_EXPERTS])
    hs = _dispatch(h, src_token, down_tiles[1])
    a = _moe_up(hs, w_gate, w_up, *up_tiles, layer)
    y = _moe_down(a, w_down, *down_tiles, layer)
    return _combine(x, y, pos, wts, mod, layer, split_out)


def kernel(x_prompt, x_sample, cache_k, cache_v, c, c_ctx, norm1, norm2, w_ada, b_ada, w_conv_in, conv_w,
           w_conv_out, w_qkv, q_norm, k_norm, rpb, w_attn_out, w_ffn_gate, w_ffn_up, w_ffn_down, w_router,
           w_moe_gate, w_moe_up, w_moe_down):
    x_parts = [x_prompt.reshape(CTX_ROWS, D), x_sample.reshape(DEC_ROWS, D)]
    cond = jnp.concatenate([c_ctx[None, :], c, jnp.zeros((N_GROUPS_PADDED - 1 - N_DEC_SEQ, D), f32)], axis=0)
    mod = _ada_params(cond, w_ada, b_ada)
    cache_k2 = cache_k.reshape(N_DEC_SEQ, -1, PAST_LEN, D)
    cache_v2 = cache_v.reshape(N_DEC_SEQ, -1, PAST_LEN, D)
    bias_blocks = _rel_bias_blocks(rpb)
    w_conv_in, w_conv_out, w_qkv, w_attn_out, w_ffn_gate, w_ffn_up, w_ffn_down = (
        w.astype(bf16) for w in (w_conv_in, w_conv_out, w_qkv, w_attn_out, w_ffn_gate, w_ffn_up, w_ffn_down))

    new_k, new_v = [], []
    for layer in range(DEPTH):
        jl = layer // 2
        if layer % 2 == 0:
            gated = []
            row0 = 0
            for part in x_parts:
                ctx_rows = min(max(CTX_ROWS - row0, 0), part.shape[0])
                b, u = _conv_in(part, mod, norm1, w_conv_in, layer, row0)
                gated.append(_conv_gate(b, u, conv_w, layer, ctx_rows))
                row0 += part.shape[0]
            conv_out_tn = 1024 // len(x_parts)
            x = _mm_residual(gated, w_conv_out, x_parts, mod, layer, 2, 1024, conv_out_tn, "conv_out")
            a = _ffn_up(x, mod, norm2, w_ffn_gate, w_ffn_up, layer)
            x = _mm_residual([a], w_ffn_down, [x], mod, layer, 5, 1024, 512, "ffn_down")
            x_parts = [x]
        else:
            qkv_ctx = _qkv(x, mod, norm1, w_qkv, q_norm, k_norm, layer, 0, CTX_ROWS, f32)
            qkv_dec = _qkv(x, mod, norm1, w_qkv, q_norm, k_norm, layer, CTX_ROWS, DEC_ROWS, bf16)
            o_ctx, k_ctx, v_ctx = _ctx_attention(qkv_ctx)
            new_k.append(k_ctx.reshape(N_CTX_SEQ, CTX_LEN, N_HEADS, HEAD_DIM))
            new_v.append(v_ctx.reshape(N_CTX_SEQ, CTX_LEN, N_HEADS, HEAD_DIM))
            o_dec = _latent_attention(qkv_dec, cache_k2, cache_v2, jl, bias_blocks)
            x = _mm_residual([o_ctx, o_dec], w_attn_out, [x], mod, layer, 2, 1024, 1024, "attn_out")
            x_parts = _moe_layer(x, mod, norm2, w_router, w_moe_gate, w_moe_up, w_moe_down, layer,
                                 split_out=layer == DEPTH - 1)
            x = x_parts[0]

    y_prompt = x_parts[0].reshape(N_CTX_SEQ, CTX_LEN, D)
    y_sample = x_parts[1].reshape(N_DEC_SEQ, DEC_LEN, D)
    return y_prompt, y_sample, jnp.stack(new_k, axis=1), jnp.stack(new_v, axis=1)
```
